```python
import math
import jax, jax.numpy as jnp
from jax import lax
import numpy as np

D_MODEL = 4096
BATCH = 16
SEQ = 256
DEPTH = 4
DEC_BATCH = 4
DEC_SEQ = 4096
PAST_LEN = 256

GRID_W = 64
N_MIXERS = 2
N_ATTN_LAYERS = (DEPTH + 1) // 2
N_FOURIER_LAYERS = DEPTH // 2
HEAD_DIM = 128
N_HEADS = D_MODEL // HEAD_DIM
N_KV_HEADS = N_HEADS // 4
GQA_GROUP = N_HEADS // N_KV_HEADS
ROPE_AXIS_DIM = HEAD_DIM // 2
ROPE_THETA = 10000.0
Q_BLOCK = 128
FGROUP_DIM = 128
N_FGROUPS = D_MODEL // FGROUP_DIM
N_EXPERTS = 16
CAPACITY_FACTOR = 2
D_EXPERT = D_MODEL // 2
N_MOD = 6
EPS = 1e-6

kernel_name = "hybrid_dit_gqa_fnet_ecmoe_step"


def rmsnorm(x, g):
    xf = x.astype(jnp.float32)
    y = xf * lax.rsqrt(jnp.mean(xf * xf, axis=-1, keepdims=True) + EPS)
    return (y * g.astype(jnp.float32)).astype(x.dtype)


def modulation(cvec, w, b):
    m = jax.nn.silu(cvec) @ w + b
    return [p[:, None, :] for p in jnp.split(m, N_MOD, axis=-1)]


def ada(x, g, shift, scale):
    return rmsnorm(x, g) * (1 + scale) + shift


def rope_tables(pos, dtype):
    inv_freq = ROPE_THETA ** (-jnp.arange(0, ROPE_AXIS_DIM, 2, dtype=jnp.float32) / ROPE_AXIS_DIM)
    ang = pos[:, None] * inv_freq[None, :]
    return jnp.cos(ang)[None, :, None, :].astype(dtype), jnp.sin(ang)[None, :, None, :].astype(dtype)


def rope_1d(x, cos, sin):
    half = x.shape[-1] // 2
    x1, x2 = x[..., :half], x[..., half:]
    return jnp.concatenate([x1 * cos - x2 * sin, x2 * cos + x1 * sin], axis=-1)


def axial_rope(x, rope):
    cos_r, sin_r, cos_c, sin_c = rope
    return jnp.concatenate([rope_1d(x[..., :ROPE_AXIS_DIM], cos_r, sin_r),
                            rope_1d(x[..., ROPE_AXIS_DIM:], cos_c, sin_c)], axis=-1)


def attn_qkv(h, wq, wk, wv, gq, gk):
    B, L, _ = h.shape
    q = rmsnorm((h @ wq).reshape(B, L, N_HEADS, HEAD_DIM), gq)
    k = rmsnorm((h @ wk).reshape(B, L, N_KV_HEADS, HEAD_DIM), gk)
    v = (h @ wv).reshape(B, L, N_KV_HEADS, HEAD_DIM)
    return q, k, v


def attend(q, k, v):
    B, Lq, _, _ = q.shape
    nb = Lq // Q_BLOCK
    scale = 1.0 / math.sqrt(HEAD_DIM)
    qb = q.reshape(B, nb, Q_BLOCK, N_KV_HEADS, GQA_GROUP, HEAD_DIM).transpose(1, 0, 2, 3, 4, 5)

    def one_block(qblk):
        s = jnp.einsum('bqkgd,bskd->bkgqs', qblk, k, preferred_element_type=jnp.float32) * scale
        p = jax.nn.softmax(s, axis=-1).astype(v.dtype)
        return jnp.einsum('bkgqs,bskd->bqkgd', p, v)

    o = lax.map(one_block, qb)
    return o.transpose(1, 0, 2, 3, 4, 5).reshape(B, Lq, N_HEADS * HEAD_DIM)


def fourier_mix(h, w_out):
    B, L, D = h.shape
    u = h.astype(jnp.float32).reshape(B, L, N_FGROUPS, FGROUP_DIM)
    f = jnp.fft.fft2(u, axes=(1, 3), norm='ortho').real.astype(h.dtype)
    return f.reshape(B, L, D) @ w_out


def expert_choice_ffn(h, w_router, w_gate, w_up, w_down):
    B, L, D = h.shape
    t = h.reshape(B * L, D)
    n_tok = B * L
    cap = (CAPACITY_FACTOR * n_tok) // N_EXPERTS
    aff = jax.nn.softmax((t @ w_router).astype(jnp.float32), axis=-1)
    gates, idx = lax.top_k(aff.T, cap)
    xe = t[idx]
    hid = jax.nn.silu(jnp.einsum('ecd,edf->ecf', xe, w_gate)) * jnp.einsum('ecd,edf->ecf', xe, w_up)
    ye = jnp.einsum('ecf,efd->ecd', hid, w_down) * gates[..., None].astype(h.dtype)
    out = jnp.zeros_like(t).at[idx.reshape(-1)].add(ye.reshape(-1, D))
    return out.reshape(B, L, D)


def setup_inputs(seed: int = 0) -> dict:
    key = jax.random.key(seed)
    ks = jax.random.split(key, 24)
    D = D_MODEL
    nrm = jax.random.normal
    return {
        'x_prompt': nrm(ks[0], (BATCH, SEQ, D), jnp.float32),
        'x_sample': nrm(ks[1], (DEC_BATCH, DEC_SEQ, D), jnp.float32),
        'cache_k': nrm(ks[2], (DEC_BATCH, N_ATTN_LAYERS, PAST_LEN, N_KV_HEADS, HEAD_DIM), jnp.float32),
        'cache_v': nrm(ks[3], (DEC_BATCH, N_ATTN_LAYERS, PAST_LEN, N_KV_HEADS, HEAD_DIM), jnp.float32),
        'c': nrm(ks[4], (DEC_BATCH, D), jnp.float32),
        'c_ctx': nrm(ks[5], (D,), jnp.float32),
        'norm_g': 1.0 + 0.02 * nrm(ks[6], (DEPTH, 2, D), jnp.float32),
        'w_mod': 0.5 * D ** -0.5 * nrm(ks[7], (DEPTH, D, N_MOD * D), jnp.float32),
        'b_mod': 0.01 * nrm(ks[8], (DEPTH, N_MOD * D), jnp.float32),
        'w_q': D ** -0.5 * nrm(ks[9], (N_ATTN_LAYERS, D, N_HEADS * HEAD_DIM), jnp.float32),
        'w_k': D ** -0.5 * nrm(ks[10], (N_ATTN_LAYERS, D, N_KV_HEADS * HEAD_DIM), jnp.float32),
        'w_v': D ** -0.5 * nrm(ks[11], (N_ATTN_LAYERS, D, N_KV_HEADS * HEAD_DIM), jnp.float32),
        'w_o': D ** -0.5 * nrm(ks[12], (N_ATTN_LAYERS, N_HEADS * HEAD_DIM, D), jnp.float32),
        'q_norm_g': 1.0 + 0.02 * nrm(ks[13], (N_ATTN_LAYERS, HEAD_DIM), jnp.float32),
        'k_norm_g': 1.0 + 0.02 * nrm(ks[14], (N_ATTN_LAYERS, HEAD_DIM), jnp.float32),
        'w_fourier': D ** -0.5 * nrm(ks[15], (N_FOURIER_LAYERS, D, D), jnp.float32),
        'w_router': D ** -0.5 * nrm(ks[16], (DEPTH, D, N_EXPERTS), jnp.float32),
        'w_gate': D ** -0.5 * nrm(ks[17], (DEPTH, N_EXPERTS, D, D_EXPERT), jnp.float32),
        'w_up': D ** -0.5 * nrm(ks[18], (DEPTH, N_EXPERTS, D, D_EXPERT), jnp.float32),
        'w_down': D_EXPERT ** -0.5 * nrm(ks[19], (DEPTH, N_EXPERTS, D_EXPERT, D), jnp.float32),
        'final_norm_g': 1.0 + 0.02 * nrm(ks[20], (D,), jnp.float32),
    }


def reference(x_prompt, x_sample, cache_k, cache_v, c, c_ctx, norm_g, w_mod, b_mod,
              w_q, w_k, w_v, w_o, q_norm_g, k_norm_g, w_fourier, w_router,
              w_gate, w_up, w_down, final_norm_g):
    n_lat = x_sample.shape[1]
    rows = n_lat // GRID_W
    row_pos = jnp.repeat(jnp.arange(rows, dtype=jnp.float32), GRID_W)
    col_pos = jnp.tile(jnp.arange(GRID_W, dtype=jnp.float32), rows)
    cos_r, sin_r = rope_tables(row_pos, x_sample.dtype)
    cos_c, sin_c = rope_tables(col_pos, x_sample.dtype)
    rope = (cos_r, sin_r, cos_c, sin_c)

    xp, xs = x_prompt, x_sample
    ctx_cond = c_ctx[None, :]
    new_k, new_v = [], []
    for i in range(DEPTH):
        p_sh1, p_sc1, p_g1, p_sh2, p_sc2, p_g2 = modulation(ctx_cond, w_mod[i], b_mod[i])
        s_sh1, s_sc1, s_g1, s_sh2, s_sc2, s_g2 = modulation(c, w_mod[i], b_mod[i])
        hp = ada(xp, norm_g[i, 0], p_sh1, p_sc1)
        hs = ada(xs, norm_g[i, 0], s_sh1, s_sc1)
        j = i // N_MIXERS
        if i % N_MIXERS == 0:
            qp, kp, vp = attn_qkv(hp, w_q[j], w_k[j], w_v[j], q_norm_g[j], k_norm_g[j])
            new_k.append(kp)
            new_v.append(vp)
            mp = attend(qp, kp, vp) @ w_o[j]
            qs, ks_, vs = attn_qkv(hs, w_q[j], w_k[j], w_v[j], q_norm_g[j], k_norm_g[j])
            qs = axial_rope(qs, rope)
            ks_ = axial_rope(ks_, rope)
            k_all = jnp.concatenate([cache_k[:, j], ks_], axis=1)
            v_all = jnp.concatenate([cache_v[:, j], vs], axis=1)
            ms = attend(qs, k_all, v_all) @ w_o[j]
        else:
            mp = fourier_mix(hp, w_fourier[j])
            ms = fourier_mix(hs, w_fourier[j])
        xp = xp + p_g1 * mp
        xs = xs + s_g1 * ms
        hp = ada(xp, norm_g[i, 1], p_sh2, p_sc2)
        hs = ada(xs, norm_g[i, 1], s_sh2, s_sc2)
        xp = xp + p_g2 * expert_choice_ffn(hp, w_router[i], w_gate[i], w_up[i], w_down[i])
        xs = xs + s_g2 * expert_choice_ffn(hs, w_router[i], w_gate[i], w_up[i], w_down[i])

    y_prompt = rmsnorm(xp, final_norm_g)
    y_sample = rmsnorm(xs, final_norm_g)
    new_cache_k = jnp.stack(new_k, axis=1)
    new_cache_v = jnp.stack(new_v, axis=1)
    return (y_prompt, y_sample, new_cache_k, new_cache_v)
```

```python
import functools
import math

import jax
import jax.numpy as jnp
from jax import lax
from jax.experimental import pallas as pl
from jax.experimental.pallas import tpu as pltpu

F32 = jnp.float32
BF16 = jnp.bfloat16

GRID_W = 64
HEAD_DIM = 128
FGROUP_DIM = 128
ROPE_THETA = 10000.0
CAPACITY_FACTOR = 2
N_MOD = 6
EPS = 1e-6

LANES = 128
COND_ROWS = 8
VMEM_LIMIT_BYTES = 56 * 1024 * 1024


def _params(*sem):
    return pltpu.CompilerParams(dimension_semantics=sem, vmem_limit_bytes=VMEM_LIMIT_BYTES)


def _tile(n, want):
    return math.gcd(n, want)


def _mod_kernel(c_ref, w_ref, b_ref, o_ref):
    c = c_ref[...]
    a = (c * jax.nn.sigmoid(c)).astype(BF16)
    o_ref[...] = jnp.dot(a, w_ref[...].astype(BF16), preferred_element_type=F32) + b_ref[...]


def modulation_all(cond, w_mod, b_mod):
    depth, d, n = w_mod.shape
    tn = _tile(n, 512)
    return pl.pallas_call(
        _mod_kernel,
        out_shape=jax.ShapeDtypeStruct((depth, COND_ROWS, n), F32),
        grid=(depth, n // tn),
        in_specs=[
            pl.BlockSpec((COND_ROWS, d), lambda l, j: (0, 0)),
            pl.BlockSpec((None, d, tn), lambda l, j: (l, 0, j)),
            pl.BlockSpec((None, 1, tn), lambda l, j: (l, 0, j)),
        ],
        out_specs=pl.BlockSpec((None, COND_ROWS, tn), lambda l, j: (l, 0, j)),
        compiler_params=_params("parallel", "parallel"),
        name="modulation",
    )(cond, w_mod, b_mod.reshape(depth, 1, n))


def _rms(x, g):
    return x * lax.rsqrt(jnp.mean(x * x, axis=-1, keepdims=True) + EPS) * g


def _ada_kernel(x_ref, g_ref, sh_ref, sc_ref, *rest, mode):
    h = _rms(x_ref[...], g_ref[...]) * (1.0 + sc_ref[...]) + sh_ref[...]
    hb = h.astype(BF16)
    if mode == "plain":
        (h_ref,) = rest
        h_ref[...] = hb
    elif mode == "router":
        wr_ref, h_ref, aff_ref = rest
        h_ref[...] = hb
        logits = lax.dot_general(wr_ref[...].astype(BF16), hb, (((1,), (1,)), ((), ())),
                                 preferred_element_type=F32)
        m = jnp.max(logits, axis=0, keepdims=True)
        p = jnp.exp(logits - m)
        aff_ref[...] = p / jnp.sum(p, axis=0, keepdims=True)
    else:
        cs_ref, u_ref = rest
        n_groups = hb.shape[1] // FGROUP_DIM
        for grp in range(n_groups):
            lo = grp * FGROUP_DIM
            r = jnp.dot(hb[:, lo:lo + FGROUP_DIM], cs_ref[...], preferred_element_type=F32)
            u_ref[0, :, lo:lo + FGROUP_DIM] = r[:, :FGROUP_DIM].astype(BF16)
            u_ref[1, :, lo:lo + FGROUP_DIM] = r[:, FGROUP_DIM:].astype(BF16)


def ada_norm(x, g, mod_l, shift_idx, scale_idx, cond_row, *, mode="plain", w_router_t=None, cs=None):
    nt, d = x.shape
    tm = _tile(nt, 256)
    grid = (nt // tm,)
    row = lambda i: (i, 0)
    in_specs = [
        pl.BlockSpec((tm, d), row),
        pl.BlockSpec((1, d), lambda i: (0, 0)),
        pl.BlockSpec((None, 1, d), lambda i: (cond_row(i * tm) * N_MOD + shift_idx, 0, 0)),
        pl.BlockSpec((None, 1, d), lambda i: (cond_row(i * tm) * N_MOD + scale_idx, 0, 0)),
    ]
    args = [x, g.reshape(1, d), mod_l, mod_l]
    if mode == "plain":
        out_shape = jax.ShapeDtypeStruct((nt, d), BF16)
        out_specs = pl.BlockSpec((tm, d), row)
    elif mode == "router":
        e = w_router_t.shape[0]
        in_specs.append(pl.BlockSpec((e, d), lambda i: (0, 0)))
        args.append(w_router_t)
        out_shape = (jax.ShapeDtypeStruct((nt, d), BF16), jax.ShapeDtypeStruct((e, nt), F32))
        out_specs = (pl.BlockSpec((tm, d), row), pl.BlockSpec((e, tm), lambda i: (0, i)))
    else:
        in_specs.append(pl.BlockSpec(cs.shape, lambda i: (0, 0)))
        args.append(cs)
        out_shape = jax.ShapeDtypeStruct((2, nt, d), BF16)
        out_specs = pl.BlockSpec((2, tm, d), lambda i: (0, i, 0))
    return pl.pallas_call(
        functools.partial(_ada_kernel, mode=mode),
        out_shape=out_shape, grid=grid, in_specs=in_specs, out_specs=out_specs,
        compiler_params=_params("parallel"), name="ada_norm_" + mode,
    )(*args)


def _final_norm_kernel(x_ref, g_ref, o_ref):
    o_ref[...] = _rms(x_ref[...], g_ref[...])


def final_norm(x, g):
    nt, d = x.shape
    tm = _tile(nt, 256)
    return pl.pallas_call(
        _final_norm_kernel,
        out_shape=jax.ShapeDtypeStruct((nt, d), F32),
        grid=(nt // tm,),
        in_specs=[pl.BlockSpec((tm, d), lambda i: (i, 0)), pl.BlockSpec((1, d), lambda i: (0, 0))],
        out_specs=pl.BlockSpec((tm, d), lambda i: (i, 0)),
        compiler_params=_params("parallel"), name="final_norm",
    )(x, g.reshape(1, d))


def _cast_weights(w_refs, wb_refs, inner_axis):
    @pl.when(pl.program_id(inner_axis) == 0)
    def _():
        for w_ref, wb_ref in zip(w_refs, wb_refs):
            wb_ref[...] = w_ref[...].astype(BF16)


def _mm_plain_kernel(a_ref, w_ref, *rest, n_out):
    outs, (wb_ref,) = rest[:n_out], rest[n_out:]
    _cast_weights([w_ref], [wb_ref], 1)
    acc = jnp.dot(a_ref[...], wb_ref[...], preferred_element_type=F32)
    for o_ref in outs:
        o_ref[...] = acc.astype(o_ref.dtype)


def _swap32(y):
    lane = lax.broadcasted_iota(jnp.int32, y.shape, 1)
    first = (lane % 64) < 32
    return jnp.where(first, pltpu.roll(y, 96, 1), pltpu.roll(y, 32, 1))


def _mm_headnorm_kernel(a_ref, w_ref, g_ref, cos_ref, sin_ref, *rest, n_out):
    outs, (wb_ref,) = rest[:n_out], rest[n_out:]
    _cast_weights([w_ref], [wb_ref], 1)
    acc = jnp.dot(a_ref[...], wb_ref[...], preferred_element_type=F32)
    cos, sin = cos_ref[...], sin_ref[...]
    for hd in range(acc.shape[1] // HEAD_DIM):
        lo = hd * HEAD_DIM
        y = _rms(acc[:, lo:lo + HEAD_DIM], g_ref[...])
        y = y * cos + _swap32(y) * sin
        for o_ref in outs:
            o_ref[:, lo:lo + HEAD_DIM] = y.astype(o_ref.dtype)


def _mm_residual_kernel(a_ref, w_ref, res_ref, gate_ref, o_ref, wb_ref):
    _cast_weights([w_ref], [wb_ref], 1)
    acc = jnp.dot(a_ref[...], wb_ref[...], preferred_element_type=F32)
    o_ref[...] = res_ref[...] + gate_ref[...] * acc


def project(a, w_stack, layer, *, out_dtypes, head=None, residual=None, tm=512, tn=512):
    m, k = a.shape
    n = w_stack.shape[-1]
    tm, tn = _tile(m, tm), _tile(n, tn)
    grid = (n // tn, m // tm)
    in_specs = [
        pl.BlockSpec((tm, k), lambda j, i: (i, 0)),
        pl.BlockSpec((None, k, tn), lambda j, i: (layer, 0, j)),
    ]
    args = [a, w_stack]
    out_block = pl.BlockSpec((tm, tn), lambda j, i: (i, j))
    if head is not None:
        g, cos, sin = head
        in_specs += [pl.BlockSpec((1, HEAD_DIM), lambda j, i: (0, 0)),
                     pl.BlockSpec((tm, HEAD_DIM), lambda j, i: (i, 0)),
                     pl.BlockSpec((tm, HEAD_DIM), lambda j, i: (i, 0))]
        args += [g, cos, sin]
        body = functools.partial(_mm_headnorm_kernel, n_out=len(out_dtypes))
    elif residual is not None:
        res, mod_l, gate_idx, cond_row = residual
        in_specs += [pl.BlockSpec((tm, tn), lambda j, i: (i, j)),
                     pl.BlockSpec((None, 1, tn), lambda j, i: (cond_row(i * tm) * N_MOD + gate_idx, 0, j))]
        args += [res, mod_l]
        body = _mm_residual_kernel
    else:
        body = functools.partial(_mm_plain_kernel, n_out=len(out_dtypes))
    outs = pl.pallas_call(
        body,
        out_shape=tuple(jax.ShapeDtypeStruct((m, n), dt) for dt in out_dtypes),
        grid=grid, in_specs=in_specs, out_specs=tuple(out_block for _ in out_dtypes),
        scratch_shapes=[pltpu.VMEM((k, tn), BF16)],
        compiler_params=_params("parallel", "arbitrary"), name="project",
    )(*args)
    return outs if len(outs) > 1 else outs[0]


def _swiglu_kernel(a_ref, wg_ref, wu_ref, o_ref, wgb_ref, wub_ref):
    _cast_weights([wg_ref, wu_ref], [wgb_ref, wub_ref], 2)
    a = a_ref[...]
    gte = jnp.dot(a, wgb_ref[...], preferred_element_type=F32)
    up = jnp.dot(a, wub_ref[...], preferred_element_type=F32)
    o_ref[...] = (gte * jax.nn.sigmoid(gte) * up).astype(o_ref.dtype)


def expert_swiglu(xe, w_gate, w_up, layer, *, tm=512, tn=256):
    e, c, k = xe.shape
    f = w_gate.shape[-1]
    tm, tn = _tile(c, tm), _tile(f, tn)
    w_spec = pl.BlockSpec((None, None, k, tn), lambda x, j, i: (layer, x, 0, j))
    return pl.pallas_call(
        _swiglu_kernel,
        out_shape=jax.ShapeDtypeStruct((e, c, f), BF16),
        grid=(e, f // tn, c // tm),
        in_specs=[pl.BlockSpec((None, tm, k), lambda x, j, i: (x, i, 0)), w_spec, w_spec],
        out_specs=pl.BlockSpec((None, tm, tn), lambda x, j, i: (x, i, j)),
        scratch_shapes=[pltpu.VMEM((k, tn), BF16), pltpu.VMEM((k, tn), BF16)],
        compiler_params=_params("parallel", "parallel", "arbitrary"), name="expert_swiglu",
    )(xe, w_gate, w_up)


def _down_kernel(a_ref, w_ref, gate_ref, o_ref, wb_ref):
    _cast_weights([w_ref], [wb_ref], 2)
    acc = jnp.dot(a_ref[...], wb_ref[...], preferred_element_type=F32)
    o_ref[...] = acc * gate_ref[...]


def expert_down(hid, w_down, gates, layer, *, tm=512, tn=512):
    e, c, f = hid.shape
    d = w_down.shape[-1]
    tm, tn = _tile(c, tm), _tile(d, tn)
    return pl.pallas_call(
        _down_kernel,
        out_shape=jax.ShapeDtypeStruct((e, c, d), F32),
        grid=(e, d // tn, c // tm),
        in_specs=[pl.BlockSpec((None, tm, f), lambda x, j, i: (x, i, 0)),
                  pl.BlockSpec((None, None, f, tn), lambda x, j, i: (layer, x, 0, j)),
                  pl.BlockSpec((None, tm, 1), lambda x, j, i: (x, i, 0))],
        out_specs=pl.BlockSpec((None, tm, tn), lambda x, j, i: (x, i, j)),
        scratch_shapes=[pltpu.VMEM((f, tn), BF16)],
        compiler_params=_params("parallel", "parallel", "arbitrary"), name="expert_down",
    )(hid, w_down, gates.reshape(e, c, 1))


def _gated_add_kernel(x_ref, y_ref, gate_ref, o_ref):
    o_ref[...] = x_ref[...] + gate_ref[...] * y_ref[...]


def gated_add(x, y, mod_l, gate_idx, cond_row):
    nt, d = x.shape
    tm = _tile(nt, 256)
    return pl.pallas_call(
        _gated_add_kernel,
        out_shape=jax.ShapeDtypeStruct((nt, d), F32),
        grid=(nt // tm,),
        in_specs=[pl.BlockSpec((tm, d), lambda i: (i, 0)), pl.BlockSpec((tm, d), lambda i: (i, 0)),
                  pl.BlockSpec((None, 1, d), lambda i: (cond_row(i * tm) * N_MOD + gate_idx, 0, 0))],
        out_specs=pl.BlockSpec((tm, d), lambda i: (i, 0)),
        compiler_params=_params("parallel"), name="gated_add",
    )(x, y, mod_l)


def _attn_kernel(q_ref, k_ref, v_ref, o_ref, *, group):
    tq = q_ref.shape[0]
    q = jnp.concatenate([q_ref[:, g * HEAD_DIM:(g + 1) * HEAD_DIM] for g in range(group)], axis=0)
    s = lax.dot_general(q, k_ref[...], (((1,), (1,)), ((), ())), preferred_element_type=F32)
    m = jnp.max(s, axis=-1, keepdims=True)
    p = jnp.exp(s - m)
    denom = jnp.sum(p, axis=-1, keepdims=True)
    o = jnp.dot(p.astype(BF16), v_ref[...], preferred_element_type=F32) / denom
    for g in range(group):
        o_ref[:, g * HEAD_DIM:(g + 1) * HEAD_DIM] = o[g * tq:(g + 1) * tq].astype(o_ref.dtype)


def attention(q, k, v, *, n_req, q_row0, lq, tq):
    n_kv = k.shape[-1] // HEAD_DIM
    group = q.shape[-1] // HEAD_DIM // n_kv
    lk = k.shape[1]
    tq = _tile(lq, tq)
    qb0, nqb = q_row0 // tq, lq // tq
    qw = group * HEAD_DIM
    return pl.pallas_call(
        functools.partial(_attn_kernel, group=group),
        out_shape=jax.ShapeDtypeStruct((n_req * lq, q.shape[-1]), BF16),
        grid=(n_req, n_kv, nqb),
        in_specs=[pl.BlockSpec((tq, qw), lambda r, h, t: (qb0 + r * nqb + t, h)),
                  pl.BlockSpec((None, lk, HEAD_DIM), lambda r, h, t: (r, 0, h)),
                  pl.BlockSpec((None, lk, HEAD_DIM), lambda r, h, t: (r, 0, h))],
        out_specs=pl.BlockSpec((tq, qw), lambda r, h, t: (r * nqb + t, h)),
        compiler_params=_params("parallel", "parallel", "arbitrary"), name="attention",
    )(q, k, v)


def _seq_dft_kernel(m_ref, u_ref, o_ref):
    acc = jnp.dot(m_ref[0], u_ref[0], preferred_element_type=F32)
    acc += jnp.dot(m_ref[1], u_ref[1], preferred_element_type=F32)
    o_ref[...] = acc.astype(o_ref.dtype)


def seq_dft(mats, u, *, n_req, row0, tm=512, tn=512):
    length = mats.shape[1]
    d = u.shape[-1]
    tm, tn = _tile(length, tm), _tile(d, tn)
    rb0 = row0 // length
    return pl.pallas_call(
        _seq_dft_kernel,
        out_shape=jax.ShapeDtypeStruct((n_req * length, d), BF16),
        grid=(n_req, d // tn, length // tm),
        in_specs=[pl.BlockSpec((2, tm, length), lambda r, j, i: (0, i, 0)),
                  pl.BlockSpec((2, length, tn), lambda r, j, i: (0, rb0 + r, j))],
        out_specs=pl.BlockSpec((tm, tn), lambda r, j, i: (r * (length // tm) + i, j)),
        compiler_params=_params("parallel", "parallel", "arbitrary"), name="seq_dft",
    )(mats, u)


def _dft_mats(length):
    idx = jnp.arange(length, dtype=jnp.int32)
    ang = ((idx[:, None] * idx[None, :]) % length).astype(F32) * (2.0 * math.pi / length)
    norm = 1.0 / math.sqrt(length)
    return jnp.cos(ang) * norm, jnp.sin(ang) * norm


def _rope_tables(n_prompt, n_req, n_lat):
    half = HEAD_DIM // 2
    rows = n_lat // GRID_W
    row_pos = jnp.repeat(jnp.arange(rows, dtype=F32), GRID_W)
    col_pos = jnp.tile(jnp.arange(GRID_W, dtype=F32), rows)
    inv_freq = ROPE_THETA ** (-jnp.arange(0, half, 2, dtype=F32) / half)

    def cs(pos):
        ang = pos[:, None] * inv_freq[None, :]
        return jnp.cos(ang), jnp.sin(ang)

    cr, sr = cs(row_pos)
    cc, sc = cs(col_pos)
    cos = jnp.concatenate([cr, cr, cc, cc], axis=-1)
    sin = jnp.concatenate([-sr, sr, -sc, sc], axis=-1)
    cos = jnp.concatenate([jnp.ones((n_prompt, HEAD_DIM), F32), jnp.tile(cos, (n_req, 1))], axis=0)
    sin = jnp.concatenate([jnp.zeros((n_prompt, HEAD_DIM), F32), jnp.tile(sin, (n_req, 1))], axis=0)
    return cos, sin


def kernel(x_prompt, x_sample, cache_k, cache_v, c, c_ctx, norm_g, w_mod, b_mod, w_q, w_k, w_v, w_o,
           q_norm_g, k_norm_g, w_fourier, w_router, w_gate, w_up, w_down, final_norm_g):
    batch, seq, d = x_prompt.shape
    n_req, n_lat, _ = x_sample.shape
    depth = w_mod.shape[0]
    n_kv = cache_k.shape[3]
    n_exp = w_router.shape[-1]
    n_prompt = batch * seq
    nt = n_prompt + n_req * n_lat
    assert n_req + 1 <= COND_ROWS

    def cond_row(r):
        return jnp.where(r < n_prompt, 0, 1 + (r - n_prompt) // n_lat)

    cond = jnp.zeros((COND_ROWS, d), F32).at[0].set(c_ctx).at[1:1 + n_req].set(c)
    mod = modulation_all(cond, w_mod, b_mod)
    cos, sin = _rope_tables(n_prompt, n_req, n_lat)
    scale = 1.0 / math.sqrt(HEAD_DIM)

    cc, sc = _dft_mats(FGROUP_DIM)
    cs_chan = jnp.concatenate([cc, sc], axis=1).astype(BF16)
    mats_p = mats_s = None
    if depth > 1:
        cp, sp = _dft_mats(seq)
        mats_p = jnp.stack([cp, -sp]).astype(BF16)
        cl, sl = _dft_mats(n_lat)
        mats_s = jnp.stack([cl, -sl]).astype(BF16)

    x = jnp.concatenate([x_prompt.reshape(n_prompt, d), x_sample.reshape(n_req * n_lat, d)], axis=0)
    groups = ((0, n_prompt), (n_prompt, nt))
    new_k, new_v = [], []
    for i in range(depth):
        mod_l = mod[i].reshape(COND_ROWS * N_MOD, 1, d)
        j = i // 2
        if i % 2 == 0:
            h = ada_norm(x, norm_g[i, 0], mod_l, 0, 1, cond_row)
            q = project(h, w_q, j, out_dtypes=(BF16,),
                        head=(q_norm_g[j].reshape(1, HEAD_DIM) * scale, cos, sin))
            kf, kb = project(h, w_k, j, out_dtypes=(F32, BF16),
                             head=(k_norm_g[j].reshape(1, HEAD_DIM), cos, sin))
            vf, vb = project(h, w_v, j, out_dtypes=(F32, BF16))
            kvw = n_kv * HEAD_DIM
            new_k.append(kf[:n_prompt].reshape(batch, seq, n_kv, HEAD_DIM))
            new_v.append(vf[:n_prompt].reshape(batch, seq, n_kv, HEAD_DIM))
            att_p = attention(q, kb[:n_prompt].reshape(batch, seq, kvw), vb[:n_prompt].reshape(batch, seq, kvw),
                              n_req=batch, q_row0=0, lq=seq, tq=256)
            past = cache_k.shape[2]
            k_all = jnp.concatenate([cache_k[:, j].reshape(n_req, past, kvw).astype(BF16),
                                     kb[n_prompt:].reshape(n_req, n_lat, kvw)], axis=1)
            v_all = jnp.concatenate([cache_v[:, j].reshape(n_req, past, kvw).astype(BF16),
                                     vb[n_prompt:].reshape(n_req, n_lat, kvw)], axis=1)
            att_s = attention(q, k_all, v_all, n_req=n_req, q_row0=n_prompt, lq=n_lat, tq=128)
            mix = jnp.concatenate([att_p, att_s], axis=0)
            x = project(mix, w_o, j, out_dtypes=(F32,), residual=(x, mod_l, 2, cond_row))
        else:
            u = ada_norm(x, norm_g[i, 0], mod_l, 0, 1, cond_row, mode="dft", cs=cs_chan)
            f_p = seq_dft(mats_p, u, n_req=batch, row0=0, tn=d)
            f_s = seq_dft(mats_s, u, n_req=n_req, row0=n_prompt)
            mix = jnp.concatenate([f_p, f_s], axis=0)
            x = project(mix, w_fourier, j, out_dtypes=(F32,), residual=(x, mod_l, 2, cond_row))

        h, aff_t = ada_norm(x, norm_g[i, 1], mod_l, 3, 4, cond_row, mode="router",
                            w_router_t=w_router[i].T)
        xes, gts, ids = [], [], []
        for lo, hi in groups:
            cap = (CAPACITY_FACTOR * (hi - lo)) // n_exp
            gates, idx = lax.top_k(aff_t[:, lo:hi], cap)
            idx = idx + lo
            xes.append(h[idx])
            gts.append(gates)
            ids.append(idx)
        xe = jnp.concatenate(xes, axis=1)
        gates = jnp.concatenate(gts, axis=1)
        idx = jnp.concatenate(ids, axis=1)
        hid = expert_swiglu(xe, w_gate, w_up, i)
        ye = expert_down(hid, w_down, gates, i)
        comb = jnp.zeros((nt, d), F32).at[idx.reshape(-1)].add(ye.reshape(-1, d))
        x = gated_add(x, comb, mod_l, 5, cond_row)

    y = final_norm(x, final_norm_g)
    y_prompt = y[:n_prompt].reshape(batch, seq, d)
    y_sample = y[n_prompt:].reshape(n_req, n_lat, d)
    return (y_prompt, y_sample, jnp.stack(new_k, axis=1), jnp.stack(new_v, axis=1))
```

```python
import functools
import math

import jax
import jax.numpy as jnp
from jax import lax
from jax.experimental import pallas as pl
from jax.experimental.pallas import tpu as pltpu

F32 = jnp.float32
BF16 = jnp.bfloat16

GRID_W = 64
HEAD_DIM = 128
FGROUP_DIM = 128
ROPE_THETA = 10000.0
CAPACITY_FACTOR = 2
N_MOD = 6
EPS = 1e-6

LANES = 128
SUBLANES = 8
COND_ROWS = 8
VMEM_LIMIT_BYTES = 56 * 1024 * 1024


def _params(*sem):
    return pltpu.CompilerParams(dimension_semantics=sem, vmem_limit_bytes=VMEM_LIMIT_BYTES)


def _tile(n, want):
    return math.gcd(n, want)


def _mod_kernel(c_ref, w_ref, b_ref, o_ref):
    c = c_ref[...]
    a = (c * jax.nn.sigmoid(c)).astype(BF16)
    o_ref[...] = jnp.dot(a, w_ref[...].astype(BF16), preferred_element_type=F32) + b_ref[...]


def modulation_all(cond, w_mod, b_mod):
    depth, d, n = w_mod.shape
    tn = _tile(n, 512)
    return pl.pallas_call(
        _mod_kernel,
        out_shape=jax.ShapeDtypeStruct((depth, COND_ROWS, n), F32),
        grid=(depth, n // tn),
        in_specs=[
            pl.BlockSpec((COND_ROWS, d), lambda l, j: (0, 0)),
            pl.BlockSpec((None, d, tn), lambda l, j: (l, 0, j)),
            pl.BlockSpec((None, 1, tn), lambda l, j: (l, 0, j)),
        ],
        out_specs=pl.BlockSpec((None, COND_ROWS, tn), lambda l, j: (l, 0, j)),
        compiler_params=_params("parallel", "parallel"),
        name="modulation",
    )(cond, w_mod, b_mod.reshape(depth, 1, n))


def _rms(x, g):
    return x * lax.rsqrt(jnp.mean(x * x, axis=-1, keepdims=True) + EPS) * g


def _ada_kernel(x_ref, g_ref, sh_ref, sc_ref, *rest, mode):
    h = _rms(x_ref[...], g_ref[...]) * (1.0 + sc_ref[...]) + sh_ref[...]
    hb = h.astype(BF16)
    if mode == "plain":
        (h_ref,) = rest
        h_ref[...] = hb
    elif mode == "router":
        wr_ref, h_ref, aff_ref = rest
        h_ref[...] = hb
        logits = lax.dot_general(wr_ref[...].astype(BF16), hb, (((1,), (1,)), ((), ())),
                                 preferred_element_type=F32)
        m = jnp.max(logits, axis=0, keepdims=True)
        p = jnp.exp(logits - m)
        aff = p / jnp.sum(p, axis=0, keepdims=True)
        for blk in range(aff_ref.shape[0]):
            aff_ref[blk] = aff[:, blk * LANES:(blk + 1) * LANES]
    else:
        cs_ref, u_ref = rest
        n_groups = hb.shape[1] // FGROUP_DIM
        for grp in range(n_groups):
            lo = grp * FGROUP_DIM
            r = jnp.dot(hb[:, lo:lo + FGROUP_DIM], cs_ref[...], preferred_element_type=F32)
            u_ref[0, :, lo:lo + FGROUP_DIM] = r[:, :FGROUP_DIM].astype(BF16)
            u_ref[1, :, lo:lo + FGROUP_DIM] = r[:, FGROUP_DIM:].astype(BF16)


def ada_norm(x, g, mod_l, shift_idx, scale_idx, cond_row, *, mode="plain", w_router_t=None, cs=None):
    nt, d = x.shape
    tm = _tile(nt, 256)
    grid = (nt // tm,)
    row = lambda i: (i, 0)
    in_specs = [
        pl.BlockSpec((tm, d), row),
        pl.BlockSpec((1, d), lambda i: (0, 0)),
        pl.BlockSpec((None, 1, d), lambda i: (cond_row(i * tm) * N_MOD + shift_idx, 0, 0)),
        pl.BlockSpec((None, 1, d), lambda i: (cond_row(i * tm) * N_MOD + scale_idx, 0, 0)),
    ]
    args = [x, g.reshape(1, d), mod_l, mod_l]
    if mode == "plain":
        out_shape = jax.ShapeDtypeStruct((nt, d), BF16)
        out_specs = pl.BlockSpec((tm, d), row)
    elif mode == "router":
        e = w_router_t.shape[0]
        in_specs.append(pl.BlockSpec((e, d), lambda i: (0, 0)))
        args.append(w_router_t)
        out_shape = (jax.ShapeDtypeStruct((nt, d), BF16), jax.ShapeDtypeStruct((nt // LANES, e, LANES), F32))
        out_specs = (pl.BlockSpec((tm, d), row), pl.BlockSpec((tm // LANES, e, LANES), lambda i: (i, 0, 0)))
    else:
        in_specs.append(pl.BlockSpec(cs.shape, lambda i: (0, 0)))
        args.append(cs)
        out_shape = jax.ShapeDtypeStruct((2, nt, d), BF16)
        out_specs = pl.BlockSpec((2, tm, d), lambda i: (0, i, 0))
    return pl.pallas_call(
        functools.partial(_ada_kernel, mode=mode),
        out_shape=out_shape, grid=grid, in_specs=in_specs, out_specs=out_specs,
        compiler_params=_params("parallel"), name="ada_norm_" + mode,
    )(*args)


def _final_norm_kernel(x_ref, g_ref, o_ref):
    o_ref[...] = _rms(x_ref[...], g_ref[...])


def final_norm(x, g):
    nt, d = x.shape
    tm = _tile(nt, 256)
    return pl.pallas_call(
        _final_norm_kernel,
        out_shape=jax.ShapeDtypeStruct((nt, d), F32),
        grid=(nt // tm,),
        in_specs=[pl.BlockSpec((tm, d), lambda i: (i, 0)), pl.BlockSpec((1, d), lambda i: (0, 0))],
        out_specs=pl.BlockSpec((tm, d), lambda i: (i, 0)),
        compiler_params=_params("parallel"), name="final_norm",
    )(x, g.reshape(1, d))


def _cast_weights(w_refs, wb_refs, inner_axis):
    @pl.when(pl.program_id(inner_axis) == 0)
    def _():
        for w_ref, wb_ref in zip(w_refs, wb_refs):
            wb_ref[...] = w_ref[...].astype(BF16)


def _mm_plain_kernel(a_ref, w_ref, *rest, n_out):
    outs, (wb_ref,) = rest[:n_out], rest[n_out:]
    _cast_weights([w_ref], [wb_ref], 1)
    acc = jnp.dot(a_ref[...], wb_ref[...], preferred_element_type=F32)
    for o_ref in outs:
        o_ref[...] = acc.astype(o_ref.dtype)


def _swap32(y):
    lane = lax.broadcasted_iota(jnp.int32, y.shape, 1)
    first = (lane % 64) < 32
    return jnp.where(first, pltpu.roll(y, 96, 1), pltpu.roll(y, 32, 1))


def _mm_headnorm_kernel(a_ref, w_ref, g_ref, cos_ref, sin_ref, *rest, n_out):
    outs, (wb_ref,) = rest[:n_out], rest[n_out:]
    _cast_weights([w_ref], [wb_ref], 1)
    acc = jnp.dot(a_ref[...], wb_ref[...], preferred_element_type=F32)
    cos, sin = cos_ref[...], sin_ref[...]
    for hd in range(acc.shape[1] // HEAD_DIM):
        lo = hd * HEAD_DIM
        y = _rms(acc[:, lo:lo + HEAD_DIM], g_ref[...])
        y = y * cos + _swap32(y) * sin
        for o_ref in outs:
            o_ref[:, lo:lo + HEAD_DIM] = y.astype(o_ref.dtype)


def _mm_residual_kernel(a_ref, w_ref, res_ref, gate_ref, o_ref, wb_ref):
    _cast_weights([w_ref], [wb_ref], 1)
    acc = jnp.dot(a_ref[...], wb_ref[...], preferred_element_type=F32)
    o_ref[...] = res_ref[...] + gate_ref[...] * acc


def project(a, w_stack, layer, *, out_dtypes, head=None, residual=None, tm=512, tn=512):
    m, k = a.shape
    n = w_stack.shape[-1]
    tm, tn = _tile(m, tm), _tile(n, tn)
    grid = (n // tn, m // tm)
    in_specs = [
        pl.BlockSpec((tm, k), lambda j, i: (i, 0)),
        pl.BlockSpec((None, k, tn), lambda j, i: (layer, 0, j)),
    ]
    args = [a, w_stack]
    out_block = pl.BlockSpec((tm, tn), lambda j, i: (i, j))
    if head is not None:
        g, cos, sin = head
        in_specs += [pl.BlockSpec((1, HEAD_DIM), lambda j, i: (0, 0)),
                     pl.BlockSpec((tm, HEAD_DIM), lambda j, i: (i, 0)),
                     pl.BlockSpec((tm, HEAD_DIM), lambda j, i: (i, 0))]
        args += [g, cos, sin]
        body = functools.partial(_mm_headnorm_kernel, n_out=len(out_dtypes))
    elif residual is not None:
        res, mod_l, gate_idx, cond_row = residual
        in_specs += [pl.BlockSpec((tm, tn), lambda j, i: (i, j)),
                     pl.BlockSpec((None, 1, tn), lambda j, i: (cond_row(i * tm) * N_MOD + gate_idx, 0, j))]
        args += [res, mod_l]
        body = _mm_residual_kernel
    else:
        body = functools.partial(_mm_plain_kernel, n_out=len(out_dtypes))
    outs = pl.pallas_call(
        body,
        out_shape=tuple(jax.ShapeDtypeStruct((m, n), dt) for dt in out_dtypes),
        grid=grid, in_specs=in_specs, out_specs=tuple(out_block for _ in out_dtypes),
        scratch_shapes=[pltpu.VMEM((k, tn), BF16)],
        compiler_params=_params("parallel", "arbitrary"), name="project",
    )(*args)
    return outs if len(outs) > 1 else outs[0]


PICK_ROWS = 512


def _ones_where(mask, dtype):
    return jnp.where(mask, 1.0, 0.0).astype(dtype)


def _select_kernel(aff_ref, pick_ref, tot_ref, offs_ref, w_scr, o_scr, *, groups):
    n_exp = aff_ref.shape[1]
    li = lax.broadcasted_iota(jnp.int32, (LANES, LANES), 0)
    lj = lax.broadcasted_iota(jnp.int32, (LANES, LANES), 1)
    tri = _ones_where(li <= lj, BF16)

    def chunk_cumsum(mask, nb):
        w = jnp.dot(_ones_where(mask, BF16).reshape(nb * n_exp, LANES), tri,
                    preferred_element_type=F32).reshape(nb, n_exp, LANES)
        w_scr[0:nb] = w

        def body(j, run):
            o_scr[j] = jnp.broadcast_to(run, (n_exp, LANES))
            return run + w_scr[j][:, LANES - 1:LANES]

        lax.fori_loop(0, nb, body, jnp.zeros((n_exp, 1), F32))
        return w, o_scr[0:nb]

    for j0, nb, cap, slot0 in groups:
        keys = pltpu.bitcast(aff_ref[j0:j0 + nb], jnp.int32)

        def count(mask):
            return jnp.sum(jnp.sum(_ones_where(mask, F32), axis=0), axis=1, keepdims=True)

        def search(it, prefix):
            cand = prefix | lax.shift_left(jnp.int32(1), 30 - it)
            return jnp.where(count(keys >= cand[None]) >= cap, cand, prefix)

        thr = lax.fori_loop(0, 31, search, jnp.zeros((n_exp, 1), jnp.int32))[None]
        above, ties = keys > thr, keys == thr
        need = (cap - count(above))[None]
        w_eq, o_eq = chunk_cumsum(ties, nb)
        sel = above | (ties & (w_eq + o_eq <= need))
        w, offs = chunk_cumsum(sel, nb)
        tot = jnp.broadcast_to(w[:, :, LANES - 1:LANES], w.shape)
        lead = offs - SUBLANES * jnp.floor(offs / SUBLANES)
        seg = SUBLANES * jnp.floor((lead + tot + (SUBLANES - 1)) / SUBLANES)
        rows = nb * n_exp
        blk = math.gcd(rows, 256)
        bi = lax.broadcasted_iota(jnp.int32, (blk, blk), 0)
        bj = lax.broadcasted_iota(jnp.int32, (blk, blk), 1)
        lower = _ones_where((bi // n_exp == bj // n_exp) & (bj % n_exp < bi % n_exp), BF16)
        seg2 = seg.reshape(rows, LANES).astype(BF16)
        base = jnp.concatenate(
            [jnp.dot(lower, seg2[r0:r0 + blk], preferred_element_type=F32) for r0 in range(0, rows, blk)],
            axis=0).reshape(nb, n_exp, LANES)
        pick_ref[j0:j0 + nb] = jnp.where(sel, base + lead + w - 1.0, -1.0).astype(jnp.int32)
        tot_ref[j0:j0 + nb] = tot.astype(jnp.int32)
        offs_ref[j0:j0 + nb] = offs.astype(jnp.int32) + slot0


def route_select(aff, groups):
    nbt, e, _ = aff.shape
    shape = jax.ShapeDtypeStruct(aff.shape, jnp.int32)
    max_nb = max(g[1] for g in groups)
    pick, tot, offs = pl.pallas_call(
        functools.partial(_select_kernel, groups=groups),
        out_shape=(shape, shape, shape),
        scratch_shapes=[pltpu.VMEM((max_nb, e, LANES), F32), pltpu.VMEM((max_nb, e, LANES), F32)],
        compiler_params=pltpu.CompilerParams(vmem_limit_bytes=VMEM_LIMIT_BYTES), name="route_select",
    )(aff)
    return pick, tot[:, :, 0].reshape(nbt * e), offs[:, :, 0].reshape(nbt * e)


def _pick_onehot(pick, row0, rows):
    ids = lax.broadcasted_iota(jnp.int32, (rows, LANES), 0) + row0
    hit = ids == pick[0:1, :]
    for e in range(1, pick.shape[0]):
        hit = hit | (ids == pick[e:e + 1, :])
    return hit


def _segment(tot_ref, offs_ref, j, n_exp, e):
    t = tot_ref[j * n_exp + e]
    o = offs_ref[j * n_exp + e]
    r = o & (SUBLANES - 1)
    return o, r, ((r + t + SUBLANES - 1) // SUBLANES) * SUBLANES


def _for_each_piece(tot_ref, offs_ref, j, n_exp, row0, rows, fn):
    max_groups = (SUBLANES - 1 + LANES + SUBLANES - 1) // SUBLANES

    def per_expert(e, base):
        o, r, seg = _segment(tot_ref, offs_ref, j, n_exp, e)
        lo = jnp.maximum(base, row0)
        hi = jnp.minimum(base + seg, row0 + rows)
        n = jnp.maximum(hi - lo, 0)
        slot = (o - r) + (lo - base)
        done = jnp.int32(0)
        for bit in range(max_groups.bit_length() - 1, -1, -1):
            size = SUBLANES << bit

            @pl.when((n & size) != 0)
            def _():
                fn(e, pl.multiple_of(slot + done, SUBLANES), pl.multiple_of(lo - row0 + done, SUBLANES), size)

            done = done + (n & size)
        return base + seg

    lax.fori_loop(0, n_exp, per_expert, jnp.int32(0))


def _staging_rows(tot_ref, offs_ref, j, n_exp):
    return lax.fori_loop(0, n_exp, lambda e, s: s + _segment(tot_ref, offs_ref, j, n_exp, e)[2], jnp.int32(0))


def _dispatch_kernel(tot_ref, offs_ref, h_ref, pick_ref, aff_ref, xe_ref, stage, tail, sem):
    j = pl.program_id(0)
    n_exp = pick_ref.shape[0]
    half = h_ref.shape[1] // 2
    pick, aff, h = pick_ref[...], aff_ref[...], h_ref[...]
    total = _staging_rows(tot_ref, offs_ref, j, n_exp)

    @pl.when(j == 0)
    def _():
        tail[...] = jnp.zeros_like(tail)

    def copy(e, slot, row, size):
        return pltpu.make_async_copy(stage.at[pl.ds(row, size)], xe_ref.at[e, pl.ds(slot, size)], sem)

    def splice_tails(row0):
        def per_expert(e, base):
            _, r, seg = _segment(tot_ref, offs_ref, j, n_exp, e)
            last = base + seg - SUBLANES

            @pl.when((r > 0) & (base >= row0) & (base < row0 + PICK_ROWS))
            def _():
                row = pl.multiple_of(base - row0, SUBLANES)
                keep = lax.broadcasted_iota(jnp.int32, (SUBLANES, 1), 0) < r
                stage[pl.ds(row, SUBLANES), :] = jnp.where(keep, tail[e], stage[pl.ds(row, SUBLANES), :])

            @pl.when((seg > 0) & (last >= row0) & (last < row0 + PICK_ROWS))
            def _():
                tail[e] = stage[pl.ds(pl.multiple_of(last - row0, SUBLANES), SUBLANES), :]

            return base + seg

        lax.fori_loop(0, n_exp, per_expert, jnp.int32(0))

    def one_round(r, carry):
        row0 = r * PICK_ROWS
        ids = lax.broadcasted_iota(jnp.int32, (PICK_ROWS, LANES), 0) + row0
        hit = jnp.zeros((PICK_ROWS, LANES), jnp.bool_)
        gate = jnp.zeros((PICK_ROWS, LANES), F32)
        for e in range(n_exp):
            he = ids == pick[e:e + 1, :]
            hit = hit | he
            gate = jnp.where(he, aff[e:e + 1, :], gate)
        rows = jnp.dot(_ones_where(hit, BF16), h, preferred_element_type=F32)
        bits = pltpu.bitcast(rows, jnp.uint32)
        stage[:, 0:half] = (bits[:, half:] & jnp.uint32(0xFFFF0000)) | (bits[:, :half] >> 16)
        g = jnp.sum(gate, axis=1, keepdims=True)
        stage[:, half:half + LANES] = pltpu.bitcast(jnp.broadcast_to(g, (PICK_ROWS, LANES)), jnp.uint32)
        splice_tails(row0)
        _for_each_piece(tot_ref, offs_ref, j, n_exp, row0, PICK_ROWS,
                        lambda e, slot, row, size: copy(e, slot, row, size).start())
        _for_each_piece(tot_ref, offs_ref, j, n_exp, row0, PICK_ROWS,
                        lambda e, slot, row, size: copy(e, slot, row, size).wait())
        return carry

    lax.fori_loop(0, (total + PICK_ROWS - 1) // PICK_ROWS, one_round, 0)


def dispatch(h, pick, aff, tot, offs, slots):
    nt, d = h.shape
    nbt, e, _ = pick.shape
    width = d // 2 + LANES
    chunk = pl.BlockSpec((None, e, LANES), lambda j, *_: (j, 0, 0))
    return pl.pallas_call(
        _dispatch_kernel,
        out_shape=jax.ShapeDtypeStruct((e, slots, width), jnp.uint32),
        grid_spec=pltpu.PrefetchScalarGridSpec(
            num_scalar_prefetch=2, grid=(nbt,),
            in_specs=[pl.BlockSpec((LANES, d), lambda j, *_: (j, 0)), chunk, chunk],
            out_specs=pl.BlockSpec(memory_space=pl.ANY),
            scratch_shapes=[pltpu.VMEM((PICK_ROWS, width), jnp.uint32),
                            pltpu.VMEM((e, SUBLANES, width), jnp.uint32), pltpu.SemaphoreType.DMA(())]),
        compiler_params=_params("arbitrary"), name="dispatch",
    )(tot, offs, h, pick, aff)


def _combine_kernel(tot_ref, offs_ref, x_ref, pick_ref, gate_ref, ye_ref, o_ref, stage, sem, *, slots):
    j = pl.program_id(0)
    n_exp = pick_ref.shape[0]
    pick = pick_ref[...]
    total = _staging_rows(tot_ref, offs_ref, j, n_exp)
    o_ref[...] = jnp.zeros_like(o_ref)

    def copy(e, slot, row, size):
        return pltpu.make_async_copy(ye_ref.at[pl.ds(e * slots + slot, size)], stage.at[pl.ds(row, size)], sem)

    def one_round(r, carry):
        row0 = r * PICK_ROWS
        _for_each_piece(tot_ref, offs_ref, j, n_exp, row0, PICK_ROWS,
                        lambda e, slot, row, size: copy(e, slot, row, size).start())
        _for_each_piece(tot_ref, offs_ref, j, n_exp, row0, PICK_ROWS,
                        lambda e, slot, row, size: copy(e, slot, row, size).wait())
        live = lax.broadcasted_iota(jnp.int32, (PICK_ROWS, 1), 0) < total - row0
        y = jnp.where(live, stage[...], 0.0)
        onehot = _ones_where(_pick_onehot(pick, row0, PICK_ROWS), F32).T.astype(BF16)
        acc = o_ref[...]
        for _ in range(3):
            part = y.astype(BF16)
            acc = acc + jnp.dot(onehot, part, preferred_element_type=F32)
            y = y - part.astype(F32)
        o_ref[...] = acc
        return carry

    lax.fori_loop(0, (total + PICK_ROWS - 1) // PICK_ROWS, one_round, 0)
    o_ref[...] = x_ref[...] + gate_ref[...] * o_ref[...]


def combine(x, ye, pick, tot, offs, mod_l, gate_idx, cond_row):
    nt, d = x.shape
    nbt, e, _ = pick.shape
    slots = ye.shape[1]
    return pl.pallas_call(
        functools.partial(_combine_kernel, slots=slots),
        out_shape=jax.ShapeDtypeStruct((nt, d), F32),
        grid_spec=pltpu.PrefetchScalarGridSpec(
            num_scalar_prefetch=2, grid=(nbt,),
            in_specs=[pl.BlockSpec((LANES, d), lambda j, *_: (j, 0)),
                      pl.BlockSpec((None, e, LANES), lambda j, *_: (j, 0, 0)),
                      pl.BlockSpec((None, 1, d), lambda j, *_: (cond_row(j * LANES) * N_MOD + gate_idx, 0, 0)),
                      pl.BlockSpec(memory_space=pl.ANY)],
            out_specs=pl.BlockSpec((LANES, d), lambda j, *_: (j, 0)),
            scratch_shapes=[pltpu.VMEM((PICK_ROWS, d), F32), pltpu.SemaphoreType.DMA(())]),
        compiler_params=_params("arbitrary"), name="combine",
    )(tot, offs, x, pick, mod_l, ye.reshape(e * slots, d))


def _unpack_rows(words):
    lo = pltpu.bitcast(words << 16, F32)
    hi = pltpu.bitcast(words & jnp.uint32(0xFFFF0000), F32)
    return jnp.concatenate([lo, hi], axis=1).astype(BF16)


def _swiglu_kernel(a_ref, wg_ref, wu_ref, o_ref, wgb_ref, wub_ref):
    _cast_weights([wg_ref, wu_ref], [wgb_ref, wub_ref], 2)
    a = _unpack_rows(a_ref[...])
    gte = jnp.dot(a, wgb_ref[...], preferred_element_type=F32)
    up = jnp.dot(a, wub_ref[...], preferred_element_type=F32)
    o_ref[...] = (gte * jax.nn.sigmoid(gte) * up).astype(o_ref.dtype)


def expert_swiglu(xe, w_gate, w_up, layer, *, tm=512, tn=256):
    e, c, _ = xe.shape
    k, f = w_gate.shape[-2:]
    tm, tn = _tile(c, tm), _tile(f, tn)
    w_spec = pl.BlockSpec((None, None, k, tn), lambda x, j, i: (layer, x, 0, j))
    return pl.pallas_call(
        _swiglu_kernel,
        out_shape=jax.ShapeDtypeStruct((e, c, f), BF16),
        grid=(e, f // tn, c // tm),
        in_specs=[pl.BlockSpec((None, tm, k // 2), lambda x, j, i: (x, i, 0)), w_spec, w_spec],
        out_specs=pl.BlockSpec((None, tm, tn), lambda x, j, i: (x, i, j)),
        scratch_shapes=[pltpu.VMEM((k, tn), BF16), pltpu.VMEM((k, tn), BF16)],
        compiler_params=_params("parallel", "parallel", "arbitrary"), name="expert_swiglu",
    )(xe, w_gate, w_up)


def _down_kernel(a_ref, w_ref, gate_ref, o_ref, wb_ref):
    _cast_weights([w_ref], [wb_ref], 2)
    acc = jnp.dot(a_ref[...], wb_ref[...], preferred_element_type=F32)
    o_ref[...] = acc * pltpu.bitcast(gate_ref[...], F32)[:, 0:1]


def expert_down(hid, w_down, xe, layer, *, tm=512, tn=512):
    e, c, f = hid.shape
    d = w_down.shape[-1]
    tm, tn = _tile(c, tm), _tile(d, tn)
    gate_blk = (xe.shape[-1] - LANES) // LANES
    return pl.pallas_call(
        _down_kernel,
        out_shape=jax.ShapeDtypeStruct((e, c, d), F32),
        grid=(e, d // tn, c // tm),
        in_specs=[pl.BlockSpec((None, tm, f), lambda x, j, i: (x, i, 0)),
                  pl.BlockSpec((None, None, f, tn), lambda x, j, i: (layer, x, 0, j)),
                  pl.BlockSpec((None, tm, LANES), lambda x, j, i: (x, i, gate_blk))],
        out_specs=pl.BlockSpec((None, tm, tn), lambda x, j, i: (x, i, j)),
        scratch_shapes=[pltpu.VMEM((f, tn), BF16)],
        compiler_params=_params("parallel", "parallel", "arbitrary"), name="expert_down",
    )(hid, w_down, xe)


def _attn_kernel(q_ref, *rest, group, n_seg, chunk):
    kv_refs, o_ref = rest[:2 * n_seg], rest[2 * n_seg]
    tq = q_ref.shape[0]
    q = jnp.concatenate([q_ref[:, g * HEAD_DIM:(g + 1) * HEAD_DIM] for g in range(group)], axis=0)
    m = jnp.full((group * tq, 1), -jnp.inf, F32)
    acc = jnp.zeros((group * tq, 2 * HEAD_DIM), F32)
    for seg in range(n_seg):
        k_ref, v_ref = kv_refs[2 * seg], kv_refs[2 * seg + 1]
        length = k_ref.shape[0]
        step = math.gcd(length, chunk)
        for lo in range(0, length, step):
            k = k_ref[lo:lo + step, :].astype(BF16)
            v = v_ref[lo:lo + step, :].astype(BF16)
            v1 = jnp.concatenate([v, jnp.ones_like(v)], axis=1)
            s = lax.dot_general(q, k, (((1,), (1,)), ((), ())), preferred_element_type=F32)
            m_new = jnp.maximum(m, jnp.max(s, axis=-1, keepdims=True))
            p = jnp.exp(s - m_new).astype(BF16)
            acc = jnp.exp(m - m_new) * acc + jnp.dot(p, v1, preferred_element_type=F32)
            m = m_new
    o = acc[:, :HEAD_DIM] / acc[:, HEAD_DIM:HEAD_DIM + 1]
    for g in range(group):
        o_ref[:, g * HEAD_DIM:(g + 1) * HEAD_DIM] = o[g * tq:(g + 1) * tq].astype(o_ref.dtype)


def attention(q, segments, *, n_kv, n_req, q_row0, lq, tq, chunk=512):
    group = q.shape[-1] // HEAD_DIM // n_kv
    tq = _tile(lq, tq)
    qb0, nqb = q_row0 // tq, lq // tq
    qw = group * HEAD_DIM
    in_specs = [pl.BlockSpec((tq, qw), lambda r, h, t: (qb0 + r * nqb + t, h))]
    args = [q]
    for k_arr, v_arr, block, imap in segments:
        spec = pl.BlockSpec(block, lambda r, h, t, imap=imap: imap(r, h))
        in_specs += [spec, spec]
        args += [k_arr, v_arr]
    return pl.pallas_call(
        functools.partial(_attn_kernel, group=group, n_seg=len(segments), chunk=chunk),
        out_shape=jax.ShapeDtypeStruct((n_req * lq, q.shape[-1]), BF16),
        grid=(n_req, n_kv, nqb),
        in_specs=in_specs,
        out_specs=pl.BlockSpec((tq, qw), lambda r, h, t: (r * nqb + t, h)),
        compiler_params=_params("parallel", "parallel", "arbitrary"), name="attention",
    )(*args)


def _seq_dft_kernel(m_ref, u_ref, o_ref):
    acc = jnp.dot(m_ref[0], u_ref[0], preferred_element_type=F32)
    acc += jnp.dot(m_ref[1], u_ref[1], preferred_element_type=F32)
    o_ref[...] = acc.astype(o_ref.dtype)


def seq_dft(mats, u, *, n_req, row0, tm=512, tn=512):
    length = mats.shape[1]
    d = u.shape[-1]
    tm, tn = _tile(length, tm), _tile(d, tn)
    rb0 = row0 // length
    return pl.pallas_call(
        _seq_dft_kernel,
        out_shape=jax.ShapeDtypeStruct((n_req * length, d), BF16),
        grid=(n_req, d // tn, length // tm),
        in_specs=[pl.BlockSpec((2, tm, length), lambda r, j, i: (0, i, 0)),
                  pl.BlockSpec((2, length, tn), lambda r, j, i: (0, rb0 + r, j))],
        out_specs=pl.BlockSpec((tm, tn), lambda r, j, i: (r * (length // tm) + i, j)),
        compiler_params=_params("parallel", "parallel", "arbitrary"), name="seq_dft",
    )(mats, u)


def _dft_mats(length):
    idx = jnp.arange(length, dtype=jnp.int32)
    ang = ((idx[:, None] * idx[None, :]) % length).astype(F32) * (2.0 * math.pi / length)
    norm = 1.0 / math.sqrt(length)
    return jnp.cos(ang) * norm, jnp.sin(ang) * norm


def _rope_tables(n_prompt, n_req, n_lat):
    half = HEAD_DIM // 2
    rows = n_lat // GRID_W
    row_pos = jnp.repeat(jnp.arange(rows, dtype=F32), GRID_W)
    col_pos = jnp.tile(jnp.arange(GRID_W, dtype=F32), rows)
    inv_freq = ROPE_THETA ** (-jnp.arange(0, half, 2, dtype=F32) / half)

    def cs(pos):
        ang = pos[:, None] * inv_freq[None, :]
        return jnp.cos(ang), jnp.sin(ang)

    cr, sr = cs(row_pos)
    cc, sc = cs(col_pos)
    cos = jnp.concatenate([cr, cr, cc, cc], axis=-1)
    sin = jnp.concatenate([-sr, sr, -sc, sc], axis=-1)
    cos = jnp.concatenate([jnp.ones((n_prompt, HEAD_DIM), F32), jnp.tile(cos, (n_req, 1))], axis=0)
    sin = jnp.concatenate([jnp.zeros((n_prompt, HEAD_DIM), F32), jnp.tile(sin, (n_req, 1))], axis=0)
    return cos, sin


def kernel(x_prompt, x_sample, cache_k, cache_v, c, c_ctx, norm_g, w_mod, b_mod, w_q, w_k, w_v, w_o,
           q_norm_g, k_norm_g, w_fourier, w_router, w_gate, w_up, w_down, final_norm_g):
    batch, seq, d = x_prompt.shape
    n_req, n_lat, _ = x_sample.shape
    depth = w_mod.shape[0]
    n_kv = cache_k.shape[3]
    n_exp = w_router.shape[-1]
    n_prompt = batch * seq
    nt = n_prompt + n_req * n_lat
    assert n_req + 1 <= COND_ROWS
    assert n_prompt % n_lat == 0

    def cond_row(r):
        return jnp.where(r < n_prompt, 0, 1 + (r - n_prompt) // n_lat)

    cond = jnp.zeros((COND_ROWS, d), F32).at[0].set(c_ctx).at[1:1 + n_req].set(c)
    mod = modulation_all(cond, w_mod, b_mod)
    cos, sin = _rope_tables(n_prompt, n_req, n_lat)
    scale = 1.0 / math.sqrt(HEAD_DIM)

    cc, sc = _dft_mats(FGROUP_DIM)
    cs_chan = jnp.concatenate([cc, sc], axis=1).astype(BF16)
    mats_p = mats_s = None
    if depth > 1:
        cp, sp = _dft_mats(seq)
        mats_p = jnp.stack([cp, -sp]).astype(BF16)
        cl, sl = _dft_mats(n_lat)
        mats_s = jnp.stack([cl, -sl]).astype(BF16)

    x = jnp.concatenate([x_prompt.reshape(n_prompt, d), x_sample.reshape(n_req * n_lat, d)], axis=0)
    assert n_prompt % LANES == 0 and (n_req * n_lat) % LANES == 0
    cap_p = (CAPACITY_FACTOR * n_prompt) // n_exp
    cap_s = (CAPACITY_FACTOR * n_req * n_lat) // n_exp
    groups = ((0, n_prompt // LANES, cap_p, 0), (n_prompt // LANES, n_req * n_lat // LANES, cap_s, cap_p))
    slots = cap_p + cap_s
    assert cap_p % SUBLANES == 0 and cap_s % SUBLANES == 0
    new_k, new_v = [], []
    for i in range(depth):
        mod_l = mod[i].reshape(COND_ROWS * N_MOD, 1, d)
        j = i // 2
        if i % 2 == 0:
            h = ada_norm(x, norm_g[i, 0], mod_l, 0, 1, cond_row)
            q = project(h, w_q, j, out_dtypes=(BF16,),
                        head=(q_norm_g[j].reshape(1, HEAD_DIM) * scale, cos, sin))
            kf, kb = project(h, w_k, j, out_dtypes=(F32, BF16),
                             head=(k_norm_g[j].reshape(1, HEAD_DIM), cos, sin))
            vf, vb = project(h, w_v, j, out_dtypes=(F32, BF16))
            kvw = n_kv * HEAD_DIM
            new_k.append(kf[:n_prompt].reshape(batch, seq, n_kv, HEAD_DIM))
            new_v.append(vf[:n_prompt].reshape(batch, seq, n_kv, HEAD_DIM))
            att_p = attention(q, [(kb, vb, (seq, HEAD_DIM), lambda r, h: (r, h))],
                              n_kv=n_kv, n_req=batch, q_row0=0, lq=seq, tq=256)
            past = cache_k.shape[2]
            lat0 = n_prompt // n_lat
            att_s = attention(
                q,
                [(cache_k.reshape(n_req, -1, past, kvw), cache_v.reshape(n_req, -1, past, kvw),
                  (None, None, past, HEAD_DIM), lambda r, h, j=j: (r, j, 0, h)),
                 (kb, vb, (n_lat, HEAD_DIM), lambda r, h: (lat0 + r, h))],
                n_kv=n_kv, n_req=n_req, q_row0=n_prompt, lq=n_lat, tq=128)
            mix = jnp.concatenate([att_p, att_s], axis=0)
            x = project(mix, w_o, j, out_dtypes=(F32,), residual=(x, mod_l, 2, cond_row))
        else:
            u = ada_norm(x, norm_g[i, 0], mod_l, 0, 1, cond_row, mode="dft", cs=cs_chan)
            f_p = seq_dft(mats_p, u, n_req=batch, row0=0, tn=d)
            f_s = seq_dft(mats_s, u, n_req=n_req, row0=n_prompt)
            mix = jnp.concatenate([f_p, f_s], axis=0)
            x = project(mix, w_fourier, j, out_dtypes=(F32,), residual=(x, mod_l, 2, cond_row))

        h, aff = ada_norm(x, norm_g[i, 1], mod_l, 3, 4, cond_row, mode="router",
                          w_router_t=w_router[i].T)
        pick, tot, offs = route_select(aff, groups)
        xe = dispatch(h, pick, aff, tot, offs, slots)
        hid = expert_swiglu(xe, w_gate, w_up, i)
        ye = expert_down(hid, w_down, xe, i)
        x = combine(x, ye, pick, tot, offs, mod_l, 5, cond_row)

    y = final_norm(x, final_norm_g)
    y_prompt = y[:n_prompt].reshape(batch, seq, d)
    y_sample = y[n_prompt:].reshape(n_req, n_lat, d)
    return (y_prompt, y_sample, jnp.stack(new_k, axis=1), jnp.stack(new_v, axis=1))
```

```python
import functools
import math

import jax
import jax.numpy as jnp
from jax import lax
from jax.experimental import pallas as pl
from jax.experimental.pallas import tpu as pltpu

F32 = jnp.float32
BF16 = jnp.bfloat16

GRID_W = 64
HEAD_DIM = 128
FGROUP_DIM = 128
ROPE_THETA = 10000.0
CAPACITY_FACTOR = 2
N_MOD = 6
EPS = 1e-6

LANES = 128
SUBLANES = 8
MXU_COLS = 256
COND_ROWS = 8
VMEM_LIMIT_BYTES = 56 * 1024 * 1024


def _params(*sem):
    return pltpu.CompilerParams(dimension_semantics=sem, vmem_limit_bytes=VMEM_LIMIT_BYTES)


def _tile(n, want):
    return math.gcd(n, want)


def _mod_kernel(c_ref, w_ref, b_ref, o_ref):
    c = c_ref[...]
    a = (c * jax.nn.sigmoid(c)).astype(BF16)
    o_ref[...] = jnp.dot(a, w_ref[...].astype(BF16), preferred_element_type=F32) + b_ref[...]


def modulation_all(cond, w_mod, b_mod):
    depth, d, n = w_mod.shape
    tn = _tile(n, 512)
    return pl.pallas_call(
        _mod_kernel,
        out_shape=jax.ShapeDtypeStruct((depth, COND_ROWS, n), F32),
        grid=(depth, n // tn),
        in_specs=[
            pl.BlockSpec((COND_ROWS, d), lambda l, j: (0, 0)),
            pl.BlockSpec((None, d, tn), lambda l, j: (l, 0, j)),
            pl.BlockSpec((None, 1, tn), lambda l, j: (l, 0, j)),
        ],
        out_specs=pl.BlockSpec((None, COND_ROWS, tn), lambda l, j: (l, 0, j)),
        compiler_params=_params("parallel", "parallel"),
        name="modulation",
    )(cond, w_mod, b_mod.reshape(depth, 1, n))


def _rms(x, g):
    return x * lax.rsqrt(jnp.mean(x * x, axis=-1, keepdims=True) + EPS) * g


def _ada_kernel(x_ref, g_ref, sh_ref, sc_ref, *rest, mode):
    h = _rms(x_ref[...], g_ref[...]) * (1.0 + sc_ref[...]) + sh_ref[...]
    hb = h.astype(BF16)
    if mode == "plain":
        (h_ref,) = rest
        h_ref[...] = hb
    elif mode == "router":
        wr_ref, h_ref, aff_ref = rest
        h_ref[...] = hb
        logits = lax.dot_general(wr_ref[...].astype(BF16), hb, (((1,), (1,)), ((), ())),
                                 preferred_element_type=F32)
        m = jnp.max(logits, axis=0, keepdims=True)
        p = jnp.exp(logits - m)
        aff = p / jnp.sum(p, axis=0, keepdims=True)
        for blk in range(aff_ref.shape[0]):
            aff_ref[blk] = aff[:, blk * LANES:(blk + 1) * LANES]
    else:
        cs_ref, u_ref = rest
        n_groups = hb.shape[1] // FGROUP_DIM
        for grp in range(n_groups):
            lo = grp * FGROUP_DIM
            r = jnp.dot(hb[:, lo:lo + FGROUP_DIM], cs_ref[...], preferred_element_type=F32)
            u_ref[0, :, lo:lo + FGROUP_DIM] = r[:, :FGROUP_DIM].astype(BF16)
            u_ref[1, :, lo:lo + FGROUP_DIM] = r[:, FGROUP_DIM:].astype(BF16)


def ada_norm(x, g, mod_l, shift_idx, scale_idx, cond_row, *, mode="plain", w_router_t=None, cs=None):
    nt, d = x.shape
    tm = _tile(nt, 256)
    grid = (nt // tm,)
    row = lambda i: (i, 0)
    in_specs = [
        pl.BlockSpec((tm, d), row),
        pl.BlockSpec((1, d), lambda i: (0, 0)),
        pl.BlockSpec((None, 1, d), lambda i: (cond_row(i * tm) * N_MOD + shift_idx, 0, 0)),
        pl.BlockSpec((None, 1, d), lambda i: (cond_row(i * tm) * N_MOD + scale_idx, 0, 0)),
    ]
    args = [x, g.reshape(1, d), mod_l, mod_l]
    if mode == "plain":
        out_shape = jax.ShapeDtypeStruct((nt, d), BF16)
        out_specs = pl.BlockSpec((tm, d), row)
    elif mode == "router":
        e = w_router_t.shape[0]
        in_specs.append(pl.BlockSpec((e, d), lambda i: (0, 0)))
        args.append(w_router_t)
        out_shape = (jax.ShapeDtypeStruct((nt, d), BF16), jax.ShapeDtypeStruct((nt // LANES, e, LANES), F32))
        out_specs = (pl.BlockSpec((tm, d), row), pl.BlockSpec((tm // LANES, e, LANES), lambda i: (i, 0, 0)))
    else:
        in_specs.append(pl.BlockSpec(cs.shape, lambda i: (0, 0)))
        args.append(cs)
        out_shape = jax.ShapeDtypeStruct((2, nt, d), BF16)
        out_specs = pl.BlockSpec((2, tm, d), lambda i: (0, i, 0))
    return pl.pallas_call(
        functools.partial(_ada_kernel, mode=mode),
        out_shape=out_shape, grid=grid, in_specs=in_specs, out_specs=out_specs,
        compiler_params=_params("parallel"), name="ada_norm_" + mode,
    )(*args)


def _final_norm_kernel(x_ref, g_ref, o_ref):
    o_ref[...] = _rms(x_ref[...], g_ref[...])


def final_norm(x, g):
    nt, d = x.shape
    tm = _tile(nt, 256)
    return pl.pallas_call(
        _final_norm_kernel,
        out_shape=jax.ShapeDtypeStruct((nt, d), F32),
        grid=(nt // tm,),
        in_specs=[pl.BlockSpec((tm, d), lambda i: (i, 0)), pl.BlockSpec((1, d), lambda i: (0, 0))],
        out_specs=pl.BlockSpec((tm, d), lambda i: (i, 0)),
        compiler_params=_params("parallel"), name="final_norm",
    )(x, g.reshape(1, d))


def _stationary_weights(w_hbm, w_f32, w_bf16, sems, block, n_blocks, slicer, inner_axis):
    def copy(b, which):
        return pltpu.make_async_copy(slicer(w_hbm[which], b), w_f32.at[which], sems.at[which])

    @pl.when(pl.program_id(inner_axis) == 0)
    def _():
        @pl.when(block == 0)
        def _():
            for which in range(len(w_hbm)):
                copy(block, which).start()

        for which in range(len(w_hbm)):
            copy(block, which).wait()
            w_bf16[which] = w_f32[which].astype(BF16)

        @pl.when(block + 1 < n_blocks)
        def _():
            for which in range(len(w_hbm)):
                copy(block + 1, which).start()


def _weight_scratch(n_w, k, tn):
    return [pltpu.VMEM((n_w, k, tn), F32), pltpu.VMEM((n_w, k, tn), BF16), pltpu.SemaphoreType.DMA((n_w,))]


def _project_weights(w_ref, w_f32, w_bf16, sems, layer):
    tn = w_f32.shape[2]
    _stationary_weights(
        [w_ref], w_f32, w_bf16, sems, pl.program_id(0), pl.num_programs(0),
        lambda ref, b: ref.at[layer, :, pl.ds(pl.multiple_of(b * tn, tn), tn)], 1)


def _mm_plain_kernel(a_ref, w_ref, *rest, n_out, layer):
    outs, (w_f32, w_bf16, sems) = rest[:n_out], rest[n_out:]
    _project_weights(w_ref, w_f32, w_bf16, sems, layer)
    acc = jnp.dot(a_ref[...], w_bf16[0], preferred_element_type=F32)
    for o_ref in outs:
        o_ref[...] = acc.astype(o_ref.dtype)


def _swap32(y):
    lane = lax.broadcasted_iota(jnp.int32, y.shape, 1)
    first = (lane % 64) < 32
    return jnp.where(first, pltpu.roll(y, 96, 1), pltpu.roll(y, 32, 1))


def _mm_headnorm_kernel(a_ref, w_ref, g_ref, cos_ref, sin_ref, *rest, n_out, layer):
    outs, (w_f32, w_bf16, sems) = rest[:n_out], rest[n_out:]
    _project_weights(w_ref, w_f32, w_bf16, sems, layer)
    a = a_ref[...]
    cos, sin = cos_ref[...], sin_ref[...]
    tn = w_bf16.shape[2]
    step = math.gcd(tn, MXU_COLS)
    for c0 in range(0, tn, step):
        acc = jnp.dot(a, w_bf16[0, :, c0:c0 + step], preferred_element_type=F32)
        for lo in range(0, step, HEAD_DIM):
            y = _rms(acc[:, lo:lo + HEAD_DIM], g_ref[...])
            y = y * cos + _swap32(y) * sin
            for o_ref in outs:
                o_ref[:, c0 + lo:c0 + lo + HEAD_DIM] = y.astype(o_ref.dtype)


def _mm_residual_kernel(a_ref, w_ref, res_ref, gate_ref, o_ref, w_f32, w_bf16, sems, *, layer):
    _project_weights(w_ref, w_f32, w_bf16, sems, layer)
    acc = jnp.dot(a_ref[...], w_bf16[0], preferred_element_type=F32)
    o_ref[...] = res_ref[...] + gate_ref[...] * acc


def project(a, w_stack, layer, *, out_dtypes, head=None, residual=None, tm=512, tn=1024):
    m, k = a.shape
    n = w_stack.shape[-1]
    tm, tn = _tile(m, tm), _tile(n, tn)
    grid = (n // tn, m // tm)
    in_specs = [pl.BlockSpec((tm, k), lambda j, i: (i, 0)), pl.BlockSpec(memory_space=pl.ANY)]
    args = [a, w_stack]
    out_block = pl.BlockSpec((tm, tn), lambda j, i: (i, j))
    if head is not None:
        g, cos, sin = head
        in_specs += [pl.BlockSpec((1, HEAD_DIM), lambda j, i: (0, 0)),
                     pl.BlockSpec((tm, HEAD_DIM), lambda j, i: (i, 0)),
                     pl.BlockSpec((tm, HEAD_DIM), lambda j, i: (i, 0))]
        args += [g, cos, sin]
        body = functools.partial(_mm_headnorm_kernel, n_out=len(out_dtypes), layer=layer)
    elif residual is not None:
        res, mod_l, gate_idx, cond_row = residual
        in_specs += [pl.BlockSpec((tm, tn), lambda j, i: (i, j)),
                     pl.BlockSpec((None, 1, tn), lambda j, i: (cond_row(i * tm) * N_MOD + gate_idx, 0, j))]
        args += [res, mod_l]
        body = functools.partial(_mm_residual_kernel, layer=layer)
    else:
        body = functools.partial(_mm_plain_kernel, n_out=len(out_dtypes), layer=layer)
    outs = pl.pallas_call(
        body,
        out_shape=tuple(jax.ShapeDtypeStruct((m, n), dt) for dt in out_dtypes),
        grid=grid, in_specs=in_specs, out_specs=tuple(out_block for _ in out_dtypes),
        scratch_shapes=_weight_scratch(1, k, tn),
        compiler_params=_params("arbitrary", "arbitrary"), name="project",
    )(*args)
    return outs if len(outs) > 1 else outs[0]


PICK_ROWS = 512


def _ones_where(mask, dtype):
    return jnp.where(mask, 1.0, 0.0).astype(dtype)


def _select_kernel(aff_ref, pick_ref, tot_ref, offs_ref, w_scr, o_scr, *, groups):
    n_exp = aff_ref.shape[1]
    li = lax.broadcasted_iota(jnp.int32, (LANES, LANES), 0)
    lj = lax.broadcasted_iota(jnp.int32, (LANES, LANES), 1)
    tri = _ones_where(li <= lj, BF16)

    def chunk_cumsum(mask, nb):
        w = jnp.dot(_ones_where(mask, BF16).reshape(nb * n_exp, LANES), tri,
                    preferred_element_type=F32).reshape(nb, n_exp, LANES)
        w_scr[0:nb] = w

        def body(j, run):
            o_scr[j] = jnp.broadcast_to(run, (n_exp, LANES))
            return run + w_scr[j][:, LANES - 1:LANES]

        lax.fori_loop(0, nb, body, jnp.zeros((n_exp, 1), F32))
        return w, o_scr[0:nb]

    for j0, nb, cap, slot0 in groups:
        keys = pltpu.bitcast(aff_ref[j0:j0 + nb], jnp.int32)

        def count(mask):
            return jnp.sum(jnp.sum(_ones_where(mask, F32), axis=0), axis=1, keepdims=True)

        def search(it, prefix):
            cand = prefix | lax.shift_left(jnp.int32(1), 30 - it)
            return jnp.where(count(keys >= cand[None]) >= cap, cand, prefix)

        thr = lax.fori_loop(0, 31, search, jnp.zeros((n_exp, 1), jnp.int32))[None]
        above, ties = keys > thr, keys == thr
        need = (cap - count(above))[None]
        w_eq, o_eq = chunk_cumsum(ties, nb)
        sel = above | (ties & (w_eq + o_eq <= need))
        w, offs = chunk_cumsum(sel, nb)
        tot = jnp.broadcast_to(w[:, :, LANES - 1:LANES], w.shape)
        lead = offs - SUBLANES * jnp.floor(offs / SUBLANES)
        seg = SUBLANES * jnp.floor((lead + tot + (SUBLANES - 1)) / SUBLANES)
        rows = nb * n_exp
        blk = math.gcd(rows, 256)
        bi = lax.broadcasted_iota(jnp.int32, (blk, blk), 0)
        bj = lax.broadcasted_iota(jnp.int32, (blk, blk), 1)
        lower = _ones_where((bi // n_exp == bj // n_exp) & (bj % n_exp < bi % n_exp), BF16)
        seg2 = seg.reshape(rows, LANES).astype(BF16)
        base = jnp.concatenate(
            [jnp.dot(lower, seg2[r0:r0 + blk], preferred_element_type=F32) for r0 in range(0, rows, blk)],
            axis=0).reshape(nb, n_exp, LANES)
        pick_ref[j0:j0 + nb] = jnp.where(sel, base + lead + w - 1.0, -1.0).astype(jnp.int32)
        tot_ref[j0:j0 + nb] = tot.astype(jnp.int32)
        offs_ref[j0:j0 + nb] = offs.astype(jnp.int32) + slot0


def route_select(aff, groups):
    nbt, e, _ = aff.shape
    shape = jax.ShapeDtypeStruct(aff.shape, jnp.int32)
    max_nb = max(g[1] for g in groups)
    pick, tot, offs = pl.pallas_call(
        functools.partial(_select_kernel, groups=groups),
        out_shape=(shape, shape, shape),
        scratch_shapes=[pltpu.VMEM((max_nb, e, LANES), F32), pltpu.VMEM((max_nb, e, LANES), F32)],
        compiler_params=pltpu.CompilerParams(vmem_limit_bytes=VMEM_LIMIT_BYTES), name="route_select",
    )(aff)
    return pick, tot[:, :, 0].reshape(nbt * e), offs[:, :, 0].reshape(nbt * e)


def _pick_onehot(pick, row0, rows):
    ids = lax.broadcasted_iota(jnp.int32, (rows, LANES), 0) + row0
    hit = ids == pick[0:1, :]
    for e in range(1, pick.shape[0]):
        hit = hit | (ids == pick[e:e + 1, :])
    return hit


def _segment(tot_ref, offs_ref, j, n_exp, e):
    t = tot_ref[j * n_exp + e]
    o = offs_ref[j * n_exp + e]
    r = o & (SUBLANES - 1)
    return o, r, ((r + t + SUBLANES - 1) // SUBLANES) * SUBLANES


def _for_each_piece(tot_ref, offs_ref, j, n_exp, row0, rows, fn):
    max_groups = (SUBLANES - 1 + LANES + SUBLANES - 1) // SUBLANES

    def per_expert(e, base):
        o, r, seg = _segment(tot_ref, offs_ref, j, n_exp, e)
        lo = jnp.maximum(base, row0)
        hi = jnp.minimum(base + seg, row0 + rows)
        n = jnp.maximum(hi - lo, 0)
        slot = (o - r) + (lo - base)
        done = jnp.int32(0)
        for bit in range(max_groups.bit_length() - 1, -1, -1):
            size = SUBLANES << bit

            @pl.when((n & size) != 0)
            def _():
                fn(e, pl.multiple_of(slot + done, SUBLANES), pl.multiple_of(lo - row0 + done, SUBLANES), size)

            done = done + (n & size)
        return base + seg

    lax.fori_loop(0, n_exp, per_expert, jnp.int32(0))


def _staging_rows(tot_ref, offs_ref, j, n_exp):
    return lax.fori_loop(0, n_exp, lambda e, s: s + _segment(tot_ref, offs_ref, j, n_exp, e)[2], jnp.int32(0))


def _rounds(tot_ref, offs_ref, j, n_exp):
    return (_staging_rows(tot_ref, offs_ref, j, n_exp) + PICK_ROWS - 1) // PICK_ROWS


def _dispatch_kernel(tot_ref, offs_ref, h_ref, pick_ref, aff_ref, xe_ref, stage2, tail, sems):
    j = pl.program_id(0)
    buf = j % 2
    stage = stage2.at[buf]
    n_exp = pick_ref.shape[0]
    half = h_ref.shape[1] // 2
    pick, aff, h = pick_ref[...], aff_ref[...], h_ref[...]
    n_rounds = _rounds(tot_ref, offs_ref, j, n_exp)

    @pl.when(j == 0)
    def _():
        tail[...] = jnp.zeros_like(tail)

    def copy(b, e, slot, row, size):
        return pltpu.make_async_copy(stage2.at[b, pl.ds(row, size)], xe_ref.at[e, pl.ds(slot, size)], sems.at[b])

    def start_round(jj, r, b):
        _for_each_piece(tot_ref, offs_ref, jj, n_exp, r * PICK_ROWS, PICK_ROWS,
                        lambda e, slot, row, size: copy(b, e, slot, row, size).start())

    def wait_round(jj, r, b):
        _for_each_piece(tot_ref, offs_ref, jj, n_exp, r * PICK_ROWS, PICK_ROWS,
                        lambda e, slot, row, size: copy(b, e, slot, row, size).wait())

    def splice_tails(row0):
        def per_expert(e, base):
            _, r, seg = _segment(tot_ref, offs_ref, j, n_exp, e)
            last = base + seg - SUBLANES

            @pl.when((r > 0) & (base >= row0) & (base < row0 + PICK_ROWS))
            def _():
                row = pl.multiple_of(base - row0, SUBLANES)
                keep = lax.broadcasted_iota(jnp.int32, (SUBLANES, 1), 0) < r
                stage[pl.ds(row, SUBLANES), :] = jnp.where(keep, tail[e], stage[pl.ds(row, SUBLANES), :])

            @pl.when((seg > 0) & (last >= row0) & (last < row0 + PICK_ROWS))
            def _():
                tail[e] = stage[pl.ds(pl.multiple_of(last - row0, SUBLANES), SUBLANES), :]

            return base + seg

        lax.fori_loop(0, n_exp, per_expert, jnp.int32(0))

    def build(r):
        row0 = r * PICK_ROWS
        ids = lax.broadcasted_iota(jnp.int32, (PICK_ROWS, LANES), 0) + row0
        hit = jnp.zeros((PICK_ROWS, LANES), jnp.bool_)
        gate = jnp.zeros((PICK_ROWS, LANES), F32)
        for e in range(n_exp):
            he = ids == pick[e:e + 1, :]
            hit = hit | he
            gate = jnp.where(he, aff[e:e + 1, :], gate)
        rows = jnp.dot(_ones_where(hit, BF16), h, preferred_element_type=F32)
        bits = pltpu.bitcast(rows, jnp.uint32)
        stage[:, 0:half] = (bits[:, half:] & jnp.uint32(0xFFFF0000)) | (bits[:, :half] >> 16)
        g = jnp.sum(gate, axis=1, keepdims=True)
        stage[:, half:half + LANES] = pltpu.bitcast(jnp.broadcast_to(g, (PICK_ROWS, LANES)), jnp.uint32)
        splice_tails(row0)

    @pl.when(n_rounds > 0)
    def _():
        build(0)

    @pl.when(j > 0)
    def _():
        prev_rounds = _rounds(tot_ref, offs_ref, j - 1, n_exp)

        @pl.when(prev_rounds > 0)
        def _():
            wait_round(j - 1, prev_rounds - 1, 1 - buf)

    @pl.when(n_rounds > 0)
    def _():
        start_round(j, 0, buf)

    def later_round(r, carry):
        wait_round(j, r - 1, buf)
        build(r)
        start_round(j, r, buf)
        return carry

    lax.fori_loop(1, n_rounds, later_round, 0)

    @pl.when((j == pl.num_programs(0) - 1) & (n_rounds > 0))
    def _():
        wait_round(j, n_rounds - 1, buf)


def dispatch(h, pick, aff, tot, offs, slots):
    nt, d = h.shape
    nbt, e, _ = pick.shape
    width = d // 2 + LANES
    chunk = pl.BlockSpec((None, e, LANES), lambda j, *_: (j, 0, 0))
    return pl.pallas_call(
        _dispatch_kernel,
        out_shape=jax.ShapeDtypeStruct((e, slots, width), jnp.uint32),
        grid_spec=pltpu.PrefetchScalarGridSpec(
            num_scalar_prefetch=2, grid=(nbt,),
            in_specs=[pl.BlockSpec((LANES, d), lambda j, *_: (j, 0)), chunk, chunk],
            out_specs=pl.BlockSpec(memory_space=pl.ANY),
            scratch_shapes=[pltpu.VMEM((2, PICK_ROWS, width), jnp.uint32),
                            pltpu.VMEM((e, SUBLANES, width), jnp.uint32), pltpu.SemaphoreType.DMA((2,))]),
        compiler_params=_params("arbitrary"), name="dispatch",
    )(tot, offs, h, pick, aff)


def _combine_kernel(tot_ref, offs_ref, x_ref, pick_ref, gate_ref, ye_ref, o_ref, stage2, sems, *, slots):
    j = pl.program_id(0)
    buf = j % 2
    n_exp = pick_ref.shape[0]
    pick = pick_ref[...]
    total = _staging_rows(tot_ref, offs_ref, j, n_exp)
    n_rounds = (total + PICK_ROWS - 1) // PICK_ROWS

    def copy(b, e, slot, row, size):
        return pltpu.make_async_copy(ye_ref.at[pl.ds(e * slots + slot, size)],
                                     stage2.at[b, pl.ds(row, size)], sems.at[b])

    def start_round(jj, r, b):
        _for_each_piece(tot_ref, offs_ref, jj, n_exp, r * PICK_ROWS, PICK_ROWS,
                        lambda e, slot, row, size: copy(b, e, slot, row, size).start())

    def wait_round(jj, r, b):
        _for_each_piece(tot_ref, offs_ref, jj, n_exp, r * PICK_ROWS, PICK_ROWS,
                        lambda e, slot, row, size: copy(b, e, slot, row, size).wait())

    @pl.when((j == 0) & (n_rounds > 0))
    def _():
        start_round(0, 0, 0)

    @pl.when(j + 1 < pl.num_programs(0))
    def _():
        @pl.when(_rounds(tot_ref, offs_ref, j + 1, n_exp) > 0)
        def _():
            start_round(j + 1, 0, 1 - buf)

    o_ref[...] = jnp.zeros_like(o_ref)

    def one_round(r, carry):
        row0 = r * PICK_ROWS

        @pl.when(r > 0)
        def _():
            start_round(j, r, buf)

        wait_round(j, r, buf)
        live = lax.broadcasted_iota(jnp.int32, (PICK_ROWS, 1), 0) < total - row0
        y = jnp.where(live, stage2[buf], 0.0)
        onehot = _ones_where(_pick_onehot(pick, row0, PICK_ROWS), F32).T.astype(BF16)
        acc = o_ref[...]
        for _ in range(2):
            part = y.astype(BF16)
            acc = acc + jnp.dot(onehot, part, preferred_element_type=F32)
            y = y - part.astype(F32)
        o_ref[...] = acc
        return carry

    lax.fori_loop(0, n_rounds, one_round, 0)
    o_ref[...] = x_ref[...] + gate_ref[...] * o_ref[...]


def combine(x, ye, pick, tot, offs, mod_l, gate_idx, cond_row):
    nt, d = x.shape
    nbt, e, _ = pick.shape
    slots = ye.shape[1]
    return pl.pallas_call(
        functools.partial(_combine_kernel, slots=slots),
        out_shape=jax.ShapeDtypeStruct((nt, d), F32),
        grid_spec=pltpu.PrefetchScalarGridSpec(
            num_scalar_prefetch=2, grid=(nbt,),
            in_specs=[pl.BlockSpec((LANES, d), lambda j, *_: (j, 0)),
                      pl.BlockSpec((None, e, LANES), lambda j, *_: (j, 0, 0)),
                      pl.BlockSpec((None, 1, d), lambda j, *_: (cond_row(j * LANES) * N_MOD + gate_idx, 0, 0)),
                      pl.BlockSpec(memory_space=pl.ANY)],
            out_specs=pl.BlockSpec((LANES, d), lambda j, *_: (j, 0)),
            scratch_shapes=[pltpu.VMEM((2, PICK_ROWS, d), F32), pltpu.SemaphoreType.DMA((2,))]),
        compiler_params=_params("arbitrary"), name="combine",
    )(tot, offs, x, pick, mod_l, ye.reshape(e * slots, d))


def _unpack_rows(words):
    lo = pltpu.bitcast(words << 16, F32)
    hi = pltpu.bitcast(words & jnp.uint32(0xFFFF0000), F32)
    return jnp.concatenate([lo, hi], axis=1).astype(BF16)


def _expert_weights(w_refs, w_f32, w_bf16, sems, layer):
    tn = w_f32.shape[2]
    n_col = pl.num_programs(1)
    _stationary_weights(
        w_refs, w_f32, w_bf16, sems, pl.program_id(0) * n_col + pl.program_id(1), pl.num_programs(0) * n_col,
        lambda ref, b: ref.at[layer, b // n_col, :, pl.ds(pl.multiple_of((b % n_col) * tn, tn), tn)], 2)


def _swiglu_kernel(a_ref, wg_ref, wu_ref, o_ref, w_f32, w_bf16, sems, *, layer):
    _expert_weights([wg_ref, wu_ref], w_f32, w_bf16, sems, layer)
    a = _unpack_rows(a_ref[...])
    gte = jnp.dot(a, w_bf16[0], preferred_element_type=F32)
    up = jnp.dot(a, w_bf16[1], preferred_element_type=F32)
    o_ref[...] = (gte * jax.nn.sigmoid(gte) * up).astype(o_ref.dtype)


def expert_swiglu(xe, w_gate, w_up, layer, *, tm=512, tn=512):
    e, c, _ = xe.shape
    k, f = w_gate.shape[-2:]
    tm, tn = _tile(c, tm), _tile(f, tn)
    w_spec = pl.BlockSpec(memory_space=pl.ANY)
    return pl.pallas_call(
        functools.partial(_swiglu_kernel, layer=layer),
        out_shape=jax.ShapeDtypeStruct((e, c, f), BF16),
        grid=(e, f // tn, c // tm),
        in_specs=[pl.BlockSpec((None, tm, k // 2), lambda x, j, i: (x, i, 0)), w_spec, w_spec],
        out_specs=pl.BlockSpec((None, tm, tn), lambda x, j, i: (x, i, j)),
        scratch_shapes=_weight_scratch(2, k, tn),
        compiler_params=_params("arbitrary", "arbitrary", "arbitrary"), name="expert_swiglu",
    )(xe, w_gate, w_up)


def _down_kernel(a_ref, w_ref, gate_ref, o_ref, w_f32, w_bf16, sems, *, layer):
    _expert_weights([w_ref], w_f32, w_bf16, sems, layer)
    acc = jnp.dot(a_ref[...], w_bf16[0], preferred_element_type=F32)
    o_ref[...] = acc * pltpu.bitcast(gate_ref[...], F32)[:, 0:1]


def expert_down(hid, w_down, xe, layer, *, tm=512, tn=2048):
    e, c, f = hid.shape
    d = w_down.shape[-1]
    tm, tn = _tile(c, tm), _tile(d, tn)
    gate_blk = (xe.shape[-1] - LANES) // LANES
    return pl.pallas_call(
        functools.partial(_down_kernel, layer=layer),
        out_shape=jax.ShapeDtypeStruct((e, c, d), F32),
        grid=(e, d // tn, c // tm),
        in_specs=[pl.BlockSpec((None, tm, f), lambda x, j, i: (x, i, 0)),
                  pl.BlockSpec(memory_space=pl.ANY),
                  pl.BlockSpec((None, tm, LANES), lambda x, j, i: (x, i, gate_blk))],
        out_specs=pl.BlockSpec((None, tm, tn), lambda x, j, i: (x, i, j)),
        scratch_shapes=_weight_scratch(1, f, tn),
        compiler_params=_params("arbitrary", "arbitrary", "arbitrary"), name="expert_down",
    )(hid, w_down, xe)


def _attn_kernel(q_ref, *rest, group, n_seg, chunk):
    kv_refs, o_ref = rest[:2 * n_seg], rest[2 * n_seg]
    tq = q_ref.shape[0]
    q = jnp.concatenate([q_ref[:, g * HEAD_DIM:(g + 1) * HEAD_DIM] for g in range(group)], axis=0)
    m = jnp.full((group * tq, 1), -jnp.inf, F32)
    acc = jnp.zeros((group * tq, 2 * HEAD_DIM), F32)
    for seg in range(n_seg):
        k_ref, v_ref = kv_refs[2 * seg], kv_refs[2 * seg + 1]
        length = k_ref.shape[0]
        step = math.gcd(length, chunk)
        for lo in range(0, length, step):
            k = k_ref[lo:lo + step, :].astype(BF16)
            v = v_ref[lo:lo + step, :].astype(BF16)
            v1 = jnp.concatenate([v, jnp.ones_like(v)], axis=1)
            s = lax.dot_general(q, k, (((1,), (1,)), ((), ())), preferred_element_type=F32)
            m_new = jnp.maximum(m, jnp.max(s, axis=-1, keepdims=True))
            p = jnp.exp(s - m_new).astype(BF16)
            acc = jnp.exp(m - m_new) * acc + jnp.dot(p, v1, preferred_element_type=F32)
            m = m_new
    o = acc[:, :HEAD_DIM] / acc[:, HEAD_DIM:HEAD_DIM + 1]
    for g in range(group):
        o_ref[:, g * HEAD_DIM:(g + 1) * HEAD_DIM] = o[g * tq:(g + 1) * tq].astype(o_ref.dtype)


def attention(q, segments, *, n_kv, n_req, q_row0, lq, tq, chunk=512):
    group = q.shape[-1] // HEAD_DIM // n_kv
    tq = _tile(lq, tq)
    qb0, nqb = q_row0 // tq, lq // tq
    qw = group * HEAD_DIM
    in_specs = [pl.BlockSpec((tq, qw), lambda r, h, t: (qb0 + r * nqb + t, h))]
    args = [q]
    for k_arr, v_arr, block, imap in segments:
        spec = pl.BlockSpec(block, lambda r, h, t, imap=imap: imap(r, h))
        in_specs += [spec, spec]
        args += [k_arr, v_arr]
    return pl.pallas_call(
        functools.partial(_attn_kernel, group=group, n_seg=len(segments), chunk=chunk),
        out_shape=jax.ShapeDtypeStruct((n_req * lq, q.shape[-1]), BF16),
        grid=(n_req, n_kv, nqb),
        in_specs=in_specs,
        out_specs=pl.BlockSpec((tq, qw), lambda r, h, t: (r * nqb + t, h)),
        compiler_params=_params("parallel", "parallel", "arbitrary"), name="attention",
    )(*args)


def _seq_dft_kernel(m_ref, u_ref, o_ref):
    acc = jnp.dot(m_ref[0], u_ref[0], preferred_element_type=F32)
    acc += jnp.dot(m_ref[1], u_ref[1], preferred_element_type=F32)
    o_ref[...] = acc.astype(o_ref.dtype)


def seq_dft(mats, u, *, n_req, row0, tm=512, tn=512):
    length = mats.shape[1]
    d = u.shape[-1]
    tm, tn = _tile(length, tm), _tile(d, tn)
    rb0 = row0 // length
    return pl.pallas_call(
        _seq_dft_kernel,
        out_shape=jax.ShapeDtypeStruct((n_req * length, d), BF16),
        grid=(n_req, d // tn, length // tm),
        in_specs=[pl.BlockSpec((2, tm, length), lambda r, j, i: (0, i, 0)),
                  pl.BlockSpec((2, length, tn), lambda r, j, i: (0, rb0 + r, j))],
        out_specs=pl.BlockSpec((tm, tn), lambda r, j, i: (r * (length // tm) + i, j)),
        compiler_params=_params("parallel", "parallel", "arbitrary"), name="seq_dft",
    )(mats, u)


def _dft_mats(length):
    idx = jnp.arange(length, dtype=jnp.int32)
    ang = ((idx[:, None] * idx[None, :]) % length).astype(F32) * (2.0 * math.pi / length)
    norm = 1.0 / math.sqrt(length)
    return jnp.cos(ang) * norm, jnp.sin(ang) * norm


def _rope_tables(n_prompt, n_req, n_lat):
    half = HEAD_DIM // 2
    rows = n_lat // GRID_W
    row_pos = jnp.repeat(jnp.arange(rows, dtype=F32), GRID_W)
    col_pos = jnp.tile(jnp.arange(GRID_W, dtype=F32), rows)
    inv_freq = ROPE_THETA ** (-jnp.arange(0, half, 2, dtype=F32) / half)

    def cs(pos):
        ang = pos[:, None] * inv_freq[None, :]
        return jnp.cos(ang), jnp.sin(ang)

    cr, sr = cs(row_pos)
    cc, sc = cs(col_pos)
    cos = jnp.concatenate([cr, cr, cc, cc], axis=-1)
    sin = jnp.concatenate([-sr, sr, -sc, sc], axis=-1)
    cos = jnp.concatenate([jnp.ones((n_prompt, HEAD_DIM), F32), jnp.tile(cos, (n_req, 1))], axis=0)
    sin = jnp.concatenate([jnp.zeros((n_prompt, HEAD_DIM), F32), jnp.tile(sin, (n_req, 1))], axis=0)
    return cos, sin


def kernel(x_prompt, x_sample, cache_k, cache_v, c, c_ctx, norm_g, w_mod, b_mod, w_q, w_k, w_v, w_o,
           q_norm_g, k_norm_g, w_fourier, w_router, w_gate, w_up, w_down, final_norm_g):
    batch, seq, d = x_prompt.shape
    n_req, n_lat, _ = x_sample.shape
    depth = w_mod.shape[0]
    n_kv = cache_k.shape[3]
    n_exp = w_router.shape[-1]
    n_prompt = batch * seq
    nt = n_prompt + n_req * n_lat
    assert n_req + 1 <= COND_ROWS
    assert n_prompt % n_lat == 0

    def cond_row(r):
        return jnp.where(r < n_prompt, 0, 1 + (r - n_prompt) // n_lat)

    cond = jnp.zeros((COND_ROWS, d), F32).at[0].set(c_ctx).at[1:1 + n_req].set(c)
    mod = modulation_all(cond, w_mod, b_mod)
    cos, sin = _rope_tables(n_prompt, n_req, n_lat)
    scale = 1.0 / math.sqrt(HEAD_DIM)

    cc, sc = _dft_mats(FGROUP_DIM)
    cs_chan = jnp.concatenate([cc, sc], axis=1).astype(BF16)
    mats_p = mats_s = None
    if depth > 1:
        cp, sp = _dft_mats(seq)
        mats_p = jnp.stack([cp, -sp]).astype(BF16)
        cl, sl = _dft_mats(n_lat)
        mats_s = jnp.stack([cl, -sl]).astype(BF16)

    x = jnp.concatenate([x_prompt.reshape(n_prompt, d), x_sample.reshape(n_req * n_lat, d)], axis=0)
    assert n_prompt % LANES == 0 and (n_req * n_lat) % LANES == 0
    cap_p = (CAPACITY_FACTOR * n_prompt) // n_exp
    cap_s = (CAPACITY_FACTOR * n_req * n_lat) // n_exp
    groups = ((0, n_prompt // LANES, cap_p, 0), (n_prompt // LANES, n_req * n_lat // LANES, cap_s, cap_p))
    slots = cap_p + cap_s
    assert cap_p % SUBLANES == 0 and cap_s % SUBLANES == 0
    new_k, new_v = [], []
    for i in range(depth):
        mod_l = mod[i].reshape(COND_ROWS * N_MOD, 1, d)
        j = i // 2
        if i % 2 == 0:
            h = ada_norm(x, norm_g[i, 0], mod_l, 0, 1, cond_row)
            q = project(h, w_q, j, out_dtypes=(BF16,),
                        head=(q_norm_g[j].reshape(1, HEAD_DIM) * scale, cos, sin))
            kf, kb = project(h, w_k, j, out_dtypes=(F32, BF16),
                             head=(k_norm_g[j].reshape(1, HEAD_DIM), cos, sin))
            vf, vb = project(h, w_v, j, out_dtypes=(F32, BF16))
            kvw = n_kv * HEAD_DIM
            new_k.append(kf[:n_prompt].reshape(batch, seq, n_kv, HEAD_DIM))
            new_v.append(vf[:n_prompt].reshape(batch, seq, n_kv, HEAD_DIM))
            att_p = attention(q, [(kb, vb, (seq, HEAD_DIM), lambda r, h: (r, h))],
                              n_kv=n_kv, n_req=batch, q_row0=0, lq=seq, tq=256)
            past = cache_k.shape[2]
            lat0 = n_prompt // n_lat
            att_s = attention(
                q,
                [(cache_k.reshape(n_req, -1, past, kvw), cache_v.reshape(n_req, -1, past, kvw),
                  (None, None, past, HEAD_DIM), lambda r, h, j=j: (r, j, 0, h)),
                 (kb, vb, (n_lat, HEAD_DIM), lambda r, h: (lat0 + r, h))],
                n_kv=n_kv, n_req=n_req, q_row0=n_prompt, lq=n_lat, tq=128)
            mix = jnp.concatenate([att_p, att_s], axis=0)
            x = project(mix, w_o, j, out_dtypes=(F32,), residual=(x, mod_l, 2, cond_row))
        else:
            u = ada_norm(x, norm_g[i, 0], mod_l, 0, 1, cond_row, mode="dft", cs=cs_chan)
            f_p = seq_dft(mats_p, u, n_req=batch, row0=0, tn=d)
            f_s = seq_dft(mats_s, u, n_req=n_req, row0=n_prompt)
            mix = jnp.concatenate([f_p, f_s], axis=0)
            x = project(mix, w_fourier, j, out_dtypes=(F32,), residual=(x, mod_l, 2, cond_row))

        h, aff = ada_norm(x, norm_g[i, 1], mod_l, 3, 4, cond_row, mode="router",
                          w_router_t=w_router[i].T)
        pick, tot, offs = route_select(aff, groups)
        xe = dispatch(h, pick, aff, tot, offs, slots)
        hid = expert_swiglu(xe, w_gate, w_up, i)
        ye = expert_down(hid, w_down, xe, i)
        x = combine(x, ye, pick, tot, offs, mod_l, 5, cond_row)

    y = final_norm(x, final_norm_g)
    y_prompt = y[:n_prompt].reshape(batch, seq, d)
    y_sample = y[n_prompt:].reshape(n_req, n_lat, d)
    return (y_prompt, y_sample, jnp.stack(new_k, axis=1), jnp.stack(new_v, axis=1))
```

```python
import functools
import math

import jax
import jax.numpy as jnp
from jax import lax
from jax.experimental import pallas as pl
from jax.experimental.pallas import tpu as pltpu

F32 = jnp.float32
BF16 = jnp.bfloat16

GRID_W = 64
HEAD_DIM = 128
FGROUP_DIM = 128
ROPE_THETA = 10000.0
CAPACITY_FACTOR = 2
N_MOD = 6
EPS = 1e-6

LANES = 128
SUBLANES = 8
MXU_COLS = 256
COND_ROWS = 8
VMEM_LIMIT_BYTES = 56 * 1024 * 1024


def _params(*sem):
    return pltpu.CompilerParams(dimension_semantics=sem, vmem_limit_bytes=VMEM_LIMIT_BYTES)


def _tile(n, want):
    return math.gcd(n, want)


def _mod_kernel(c_ref, w_ref, b_ref, o_ref):
    c = c_ref[...]
    a = (c * jax.nn.sigmoid(c)).astype(BF16)
    o_ref[...] = jnp.dot(a, w_ref[...].astype(BF16), preferred_element_type=F32) + b_ref[...]


def modulation_all(cond, w_mod, b_mod):
    depth, d, n = w_mod.shape
    tn = _tile(n, 512)
    return pl.pallas_call(
        _mod_kernel,
        out_shape=jax.ShapeDtypeStruct((depth, COND_ROWS, n), F32),
        grid=(depth, n // tn),
        in_specs=[
            pl.BlockSpec((COND_ROWS, d), lambda l, j: (0, 0)),
            pl.BlockSpec((None, d, tn), lambda l, j: (l, 0, j)),
            pl.BlockSpec((None, 1, tn), lambda l, j: (l, 0, j)),
        ],
        out_specs=pl.BlockSpec((None, COND_ROWS, tn), lambda l, j: (l, 0, j)),
        compiler_params=_params("parallel", "parallel"),
        name="modulation",
    )(cond, w_mod, b_mod.reshape(depth, 1, n))


def _rms(x, g):
    return x * lax.rsqrt(jnp.mean(x * x, axis=-1, keepdims=True) + EPS) * g


def _ada_kernel(x_ref, g_ref, sh_ref, sc_ref, *rest, mode):
    _ada_apply(x_ref[...], g_ref, sh_ref, sc_ref, rest, mode)


def _ada_apply(x, g_ref, sh_ref, sc_ref, rest, mode):
    h = _rms(x, g_ref[...]) * (1.0 + sc_ref[...]) + sh_ref[...]
    hb = h.astype(BF16)
    if mode == "plain":
        (h_ref,) = rest
        h_ref[...] = hb
    elif mode == "router":
        wr_ref, h_ref, aff_ref = rest
        h_ref[...] = hb
        logits = lax.dot_general(wr_ref[...].astype(BF16), hb, (((1,), (1,)), ((), ())),
                                 preferred_element_type=F32)
        m = jnp.max(logits, axis=0, keepdims=True)
        p = jnp.exp(logits - m)
        aff = p / jnp.sum(p, axis=0, keepdims=True)
        for blk in range(aff_ref.shape[0]):
            aff_ref[blk] = aff[:, blk * LANES:(blk + 1) * LANES]
    else:
        cs_ref, u_ref = rest
        n_groups = hb.shape[1] // FGROUP_DIM
        for grp in range(n_groups):
            lo = grp * FGROUP_DIM
            r = jnp.dot(hb[:, lo:lo + FGROUP_DIM], cs_ref[...], preferred_element_type=F32)
            u_ref[0, :, lo:lo + FGROUP_DIM] = r[:, :FGROUP_DIM].astype(BF16)
            u_ref[1, :, lo:lo + FGROUP_DIM] = r[:, FGROUP_DIM:].astype(BF16)


def ada_norm(x, g, mod_l, shift_idx, scale_idx, cond_row, *, mode="plain", w_router_t=None, cs=None):
    nt, d = x.shape
    tm = _tile(nt, 256)
    grid = (nt // tm,)
    row = lambda i: (i, 0)
    in_specs = [
        pl.BlockSpec((tm, d), row),
        pl.BlockSpec((1, d), lambda i: (0, 0)),
        pl.BlockSpec((None, 1, d), lambda i: (cond_row(i * tm) * N_MOD + shift_idx, 0, 0)),
        pl.BlockSpec((None, 1, d), lambda i: (cond_row(i * tm) * N_MOD + scale_idx, 0, 0)),
    ]
    args = [x, g.reshape(1, d), mod_l, mod_l]
    if mode == "plain":
        out_shape = jax.ShapeDtypeStruct((nt, d), BF16)
        out_specs = pl.BlockSpec((tm, d), row)
    elif mode == "router":
        e = w_router_t.shape[0]
        in_specs.append(pl.BlockSpec((e, d), lambda i: (0, 0)))
        args.append(w_router_t)
        out_shape = (jax.ShapeDtypeStruct((nt, d), BF16), jax.ShapeDtypeStruct((nt // LANES, e, LANES), F32))
        out_specs = (pl.BlockSpec((tm, d), row), pl.BlockSpec((tm // LANES, e, LANES), lambda i: (i, 0, 0)))
    else:
        in_specs.append(pl.BlockSpec(cs.shape, lambda i: (0, 0)))
        args.append(cs)
        out_shape = jax.ShapeDtypeStruct((2, nt, d), BF16)
        out_specs = pl.BlockSpec((2, tm, d), lambda i: (0, i, 0))
    return pl.pallas_call(
        functools.partial(_ada_kernel, mode=mode),
        out_shape=out_shape, grid=grid, in_specs=in_specs, out_specs=out_specs,
        compiler_params=_params("parallel"), name="ada_norm_" + mode,
    )(*args)


def _stationary_weights(w_hbm, w_f32, w_bf16, sems, block, n_blocks, slicer, inner_axis):
    def copy(b, which):
        return pltpu.make_async_copy(slicer(w_hbm[which], b), w_f32.at[which], sems.at[which])

    @pl.when(pl.program_id(inner_axis) == 0)
    def _():
        @pl.when(block == 0)
        def _():
            for which in range(len(w_hbm)):
                copy(block, which).start()

        for which in range(len(w_hbm)):
            copy(block, which).wait()
            w_bf16[which] = w_f32[which].astype(BF16)

        @pl.when(block + 1 < n_blocks)
        def _():
            for which in range(len(w_hbm)):
                copy(block + 1, which).start()


def _weight_scratch(n_w, k, tn):
    return [pltpu.VMEM((n_w, k, tn), F32), pltpu.VMEM((n_w, k, tn), BF16), pltpu.SemaphoreType.DMA((n_w,))]


def _project_weights(w_ref, w_f32, w_bf16, sems, layer):
    tn = w_f32.shape[2]
    _stationary_weights(
        [w_ref], w_f32, w_bf16, sems, pl.program_id(0), pl.num_programs(0),
        lambda ref, b: ref.at[layer, :, pl.ds(pl.multiple_of(b * tn, tn), tn)], 1)


def _mm_plain_kernel(a_ref, w_ref, *rest, n_out, layer):
    outs, (w_f32, w_bf16, sems) = rest[:n_out], rest[n_out:]
    _project_weights(w_ref, w_f32, w_bf16, sems, layer)
    acc = jnp.dot(a_ref[...], w_bf16[0], preferred_element_type=F32)
    for o_ref in outs:
        o_ref[...] = acc.astype(o_ref.dtype)


def _swap32(y):
    lane = lax.broadcasted_iota(jnp.int32, y.shape, 1)
    first = (lane % 64) < 32
    return jnp.where(first, pltpu.roll(y, 96, 1), pltpu.roll(y, 32, 1))


def _mm_headnorm_kernel(a_ref, w_ref, g_ref, cos_ref, sin_ref, *rest, n_out, layer):
    outs, (w_f32, w_bf16, sems) = rest[:n_out], rest[n_out:]
    _project_weights(w_ref, w_f32, w_bf16, sems, layer)
    a = a_ref[...]
    cos, sin = cos_ref[...], sin_ref[...]
    tn = w_bf16.shape[2]
    step = math.gcd(tn, MXU_COLS)
    for c0 in range(0, tn, step):
        acc = jnp.dot(a, w_bf16[0, :, c0:c0 + step], preferred_element_type=F32)
        for lo in range(0, step, HEAD_DIM):
            y = _rms(acc[:, lo:lo + HEAD_DIM], g_ref[...])
            y = y * cos + _swap32(y) * sin
            for o_ref in outs:
                o_ref[:, c0 + lo:c0 + lo + HEAD_DIM] = y.astype(o_ref.dtype)


def _mm_residual_kernel(a0_ref, a1_ref, w_ref, res_ref, gate_ref, o_ref, w_f32, w_bf16, sems, *, layer, split):
    _project_weights(w_ref, w_f32, w_bf16, sems, layer)

    def emit(a_ref):
        acc = jnp.dot(a_ref[...], w_bf16[0], preferred_element_type=F32)
        o_ref[...] = res_ref[...] + gate_ref[...] * acc

    @pl.when(pl.program_id(1) < split)
    def _():
        emit(a0_ref)

    @pl.when(pl.program_id(1) >= split)
    def _():
        emit(a1_ref)


def project(a, w_stack, layer, *, out_dtypes, head=None, residual=None, tm=512, tn=1024):
    n = w_stack.shape[-1]
    if residual is not None:
        a0, a1 = a
        k = a0.shape[1]
        m = a0.shape[0] + a1.shape[0]
        tm, tn = _tile(math.gcd(a0.shape[0], a1.shape[0]), tm), _tile(n, tn)
        split = a0.shape[0] // tm
        in_specs = [pl.BlockSpec((tm, k), lambda j, i: (jnp.minimum(i, split - 1), 0)),
                    pl.BlockSpec((tm, k), lambda j, i: (jnp.maximum(i - split, 0), 0)),
                    pl.BlockSpec(memory_space=pl.ANY)]
        args = [a0, a1, w_stack]
    else:
        m, k = a.shape
        tm, tn = _tile(m, tm), _tile(n, tn)
        in_specs = [pl.BlockSpec((tm, k), lambda j, i: (i, 0)), pl.BlockSpec(memory_space=pl.ANY)]
        args = [a, w_stack]
    grid = (n // tn, m // tm)
    out_block = pl.BlockSpec((tm, tn), lambda j, i: (i, j))
    if head is not None:
        g, cos, sin = head
        in_specs += [pl.BlockSpec((1, HEAD_DIM), lambda j, i: (0, 0)),
                     pl.BlockSpec((tm, HEAD_DIM), lambda j, i: (i, 0)),
                     pl.BlockSpec((tm, HEAD_DIM), lambda j, i: (i, 0))]
        args += [g, cos, sin]
        body = functools.partial(_mm_headnorm_kernel, n_out=len(out_dtypes), layer=layer)
    elif residual is not None:
        res, mod_l, gate_idx, cond_row = residual
        in_specs += [pl.BlockSpec((tm, tn), lambda j, i: (i, j)),
                     pl.BlockSpec((None, 1, tn), lambda j, i: (cond_row(i * tm) * N_MOD + gate_idx, 0, j))]
        args += [res, mod_l]
        body = functools.partial(_mm_residual_kernel, layer=layer, split=split)
    else:
        body = functools.partial(_mm_plain_kernel, n_out=len(out_dtypes), layer=layer)
    outs = pl.pallas_call(
        body,
        out_shape=tuple(jax.ShapeDtypeStruct((m, n), dt) for dt in out_dtypes),
        grid=grid, in_specs=in_specs, out_specs=tuple(out_block for _ in out_dtypes),
        scratch_shapes=_weight_scratch(1, k, tn),
        compiler_params=_params("arbitrary", "arbitrary"), name="project",
    )(*args)
    return outs if len(outs) > 1 else outs[0]


PICK_ROWS = 512


def _ones_where(mask, dtype):
    return jnp.where(mask, 1.0, 0.0).astype(dtype)


def _select_kernel(aff_ref, pick_ref, tot_ref, offs_ref, w_scr, o_scr, *, groups):
    n_exp = aff_ref.shape[1]
    li = lax.broadcasted_iota(jnp.int32, (LANES, LANES), 0)
    lj = lax.broadcasted_iota(jnp.int32, (LANES, LANES), 1)
    tri = _ones_where(li <= lj, BF16)

    def chunk_cumsum(mask, nb):
        w = jnp.dot(_ones_where(mask, BF16).reshape(nb * n_exp, LANES), tri,
                    preferred_element_type=F32).reshape(nb, n_exp, LANES)
        w_scr[0:nb] = w

        def body(j, run):
            o_scr[j] = jnp.broadcast_to(run, (n_exp, LANES))
            return run + w_scr[j][:, LANES - 1:LANES]

        lax.fori_loop(0, nb, body, jnp.zeros((n_exp, 1), F32))
        return w, o_scr[0:nb]

    for j0, nb, cap, slot0 in groups:
        keys = pltpu.bitcast(aff_ref[j0:j0 + nb], jnp.int32)

        def count(mask):
            return jnp.sum(jnp.sum(_ones_where(mask, F32), axis=0), axis=1, keepdims=True)

        def search(it, prefix):
            cand = prefix | lax.shift_left(jnp.int32(1), 30 - it)
            return jnp.where(count(keys >= cand[None]) >= cap, cand, prefix)

        thr = lax.fori_loop(0, 31, search, jnp.zeros((n_exp, 1), jnp.int32))[None]
        above, ties = keys > thr, keys == thr
        need = (cap - count(above))[None]
        w_eq, o_eq = chunk_cumsum(ties, nb)
        sel = above | (ties & (w_eq + o_eq <= need))
        w, offs = chunk_cumsum(sel, nb)
        tot = jnp.broadcast_to(w[:, :, LANES - 1:LANES], w.shape)
        lead = offs - SUBLANES * jnp.floor(offs / SUBLANES)
        seg = SUBLANES * jnp.floor((lead + tot + (SUBLANES - 1)) / SUBLANES)
        rows = nb * n_exp
        blk = math.gcd(rows, 256)
        bi = lax.broadcasted_iota(jnp.int32, (blk, blk), 0)
        bj = lax.broadcasted_iota(jnp.int32, (blk, blk), 1)
        lower = _ones_where((bi // n_exp == bj // n_exp) & (bj % n_exp < bi % n_exp), BF16)
        seg2 = seg.reshape(rows, LANES).astype(BF16)
        base = jnp.concatenate(
            [jnp.dot(lower, seg2[r0:r0 + blk], preferred_element_type=F32) for r0 in range(0, rows, blk)],
            axis=0).reshape(nb, n_exp, LANES)
        pick_ref[j0:j0 + nb] = jnp.where(sel, base + lead + w - 1.0, -1.0).astype(jnp.int32)
        tot_ref[j0:j0 + nb] = tot.astype(jnp.int32)
        offs_ref[j0:j0 + nb] = offs.astype(jnp.int32) + slot0


def route_select(aff, groups):
    nbt, e, _ = aff.shape
    shape = jax.ShapeDtypeStruct(aff.shape, jnp.int32)
    max_nb = max(g[1] for g in groups)
    pick, tot, offs = pl.pallas_call(
        functools.partial(_select_kernel, groups=groups),
        out_shape=(shape, shape, shape),
        scratch_shapes=[pltpu.VMEM((max_nb, e, LANES), F32), pltpu.VMEM((max_nb, e, LANES), F32)],
        compiler_params=pltpu.CompilerParams(vmem_limit_bytes=VMEM_LIMIT_BYTES), name="route_select",
    )(aff)
    return pick, tot[:, :, 0].reshape(nbt * e), offs[:, :, 0].reshape(nbt * e)


def _pick_onehot(pick, row0, rows):
    ids = lax.broadcasted_iota(jnp.int32, (rows, LANES), 0) + row0
    hit = ids == pick[0:1, :]
    for e in range(1, pick.shape[0]):
        hit = hit | (ids == pick[e:e + 1, :])
    return hit


def _segment(tot_ref, offs_ref, j, n_exp, e):
    t = tot_ref[j * n_exp + e]
    o = offs_ref[j * n_exp + e]
    r = o & (SUBLANES - 1)
    return o, r, ((r + t + SUBLANES - 1) // SUBLANES) * SUBLANES


def _for_each_piece(tot_ref, offs_ref, j, n_exp, row0, rows, fn):
    max_groups = (SUBLANES - 1 + LANES + SUBLANES - 1) // SUBLANES

    def per_expert(e, base):
        o, r, seg = _segment(tot_ref, offs_ref, j, n_exp, e)
        lo = jnp.maximum(base, row0)
        hi = jnp.minimum(base + seg, row0 + rows)
        n = jnp.maximum(hi - lo, 0)
        slot = (o - r) + (lo - base)
        done = jnp.int32(0)
        for bit in range(max_groups.bit_length() - 1, -1, -1):
            size = SUBLANES << bit

            @pl.when((n & size) != 0)
            def _():
                fn(e, pl.multiple_of(slot + done, SUBLANES), pl.multiple_of(lo - row0 + done, SUBLANES), size)

            done = done + (n & size)
        return base + seg

    lax.fori_loop(0, n_exp, per_expert, jnp.int32(0))


def _staging_rows(tot_ref, offs_ref, j, n_exp):
    return lax.fori_loop(0, n_exp, lambda e, s: s + _segment(tot_ref, offs_ref, j, n_exp, e)[2], jnp.int32(0))


def _rounds(tot_ref, offs_ref, j, n_exp):
    return (_staging_rows(tot_ref, offs_ref, j, n_exp) + PICK_ROWS - 1) // PICK_ROWS


def _dispatch_kernel(tot_ref, offs_ref, h_ref, pick_ref, aff_ref, xe_ref, stage2, tail, sems):
    j = pl.program_id(0)
    buf = j % 2
    stage = stage2.at[buf]
    n_exp = pick_ref.shape[0]
    half = h_ref.shape[1] // 2
    pick, aff, h = pick_ref[...], aff_ref[...], h_ref[...]
    n_rounds = _rounds(tot_ref, offs_ref, j, n_exp)

    @pl.when(j == 0)
    def _():
        tail[...] = jnp.zeros_like(tail)

    def copy(b, e, slot, row, size):
        return pltpu.make_async_copy(stage2.at[b, pl.ds(row, size)], xe_ref.at[e, pl.ds(slot, size)], sems.at[b])

    def start_round(jj, r, b):
        _for_each_piece(tot_ref, offs_ref, jj, n_exp, r * PICK_ROWS, PICK_ROWS,
                        lambda e, slot, row, size: copy(b, e, slot, row, size).start())

    def wait_round(jj, r, b):
        _for_each_piece(tot_ref, offs_ref, jj, n_exp, r * PICK_ROWS, PICK_ROWS,
                        lambda e, slot, row, size: copy(b, e, slot, row, size).wait())

    def splice_tails(row0):
        def per_expert(e, base):
            _, r, seg = _segment(tot_ref, offs_ref, j, n_exp, e)
            last = base + seg - SUBLANES

            @pl.when((r > 0) & (base >= row0) & (base < row0 + PICK_ROWS))
            def _():
                row = pl.multiple_of(base - row0, SUBLANES)
                keep = lax.broadcasted_iota(jnp.int32, (SUBLANES, 1), 0) < r
                stage[pl.ds(row, SUBLANES), :] = jnp.where(keep, tail[e], stage[pl.ds(row, SUBLANES), :])

            @pl.when((seg > 0) & (last >= row0) & (last < row0 + PICK_ROWS))
            def _():
                tail[e] = stage[pl.ds(pl.multiple_of(last - row0, SUBLANES), SUBLANES), :]

            return base + seg

        lax.fori_loop(0, n_exp, per_expert, jnp.int32(0))

    def build(r):
        row0 = r * PICK_ROWS
        ids = lax.broadcasted_iota(jnp.int32, (PICK_ROWS, LANES), 0) + row0
        hit = jnp.zeros((PICK_ROWS, LANES), jnp.bool_)
        gate = jnp.zeros((PICK_ROWS, LANES), F32)
        for e in range(n_exp):
            he = ids == pick[e:e + 1, :]
            hit = hit | he
            gate = jnp.where(he, aff[e:e + 1, :], gate)
        rows = jnp.dot(_ones_where(hit, BF16), h, preferred_element_type=F32)
        bits = pltpu.bitcast(rows, jnp.uint32)
        stage[:, 0:half] = (bits[:, half:] & jnp.uint32(0xFFFF0000)) | (bits[:, :half] >> 16)
        g = jnp.sum(gate, axis=1, keepdims=True)
        stage[:, half:half + LANES] = pltpu.bitcast(jnp.broadcast_to(g, (PICK_ROWS, LANES)), jnp.uint32)
        splice_tails(row0)

    @pl.when(n_rounds > 0)
    def _():
        build(0)

    @pl.when(j > 0)
    def _():
        prev_rounds = _rounds(tot_ref, offs_ref, j - 1, n_exp)

        @pl.when(prev_rounds > 0)
        def _():
            wait_round(j - 1, prev_rounds - 1, 1 - buf)

    @pl.when(n_rounds > 0)
    def _():
        start_round(j, 0, buf)

    def later_round(r, carry):
        wait_round(j, r - 1, buf)
        build(r)
        start_round(j, r, buf)
        return carry

    lax.fori_loop(1, n_rounds, later_round, 0)

    @pl.when((j == pl.num_programs(0) - 1) & (n_rounds > 0))
    def _():
        wait_round(j, n_rounds - 1, buf)


def dispatch(h, pick, aff, tot, offs, slots):
    nt, d = h.shape
    nbt, e, _ = pick.shape
    width = d // 2 + LANES
    chunk = pl.BlockSpec((None, e, LANES), lambda j, *_: (j, 0, 0))
    return pl.pallas_call(
        _dispatch_kernel,
        out_shape=jax.ShapeDtypeStruct((e, slots, width), jnp.uint32),
        grid_spec=pltpu.PrefetchScalarGridSpec(
            num_scalar_prefetch=2, grid=(nbt,),
            in_specs=[pl.BlockSpec((LANES, d), lambda j, *_: (j, 0)), chunk, chunk],
            out_specs=pl.BlockSpec(memory_space=pl.ANY),
            scratch_shapes=[pltpu.VMEM((2, PICK_ROWS, width), jnp.uint32),
                            pltpu.VMEM((e, SUBLANES, width), jnp.uint32), pltpu.SemaphoreType.DMA((2,))]),
        compiler_params=_params("arbitrary"), name="dispatch",
    )(tot, offs, h, pick, aff)


def _combine_kernel(tot_ref, offs_ref, x_ref, pick_ref, gate_ref, ye_ref, *rest, slots, follow):
    n_in = {None: 0, "final": 1, "plain": 3, "dft": 4}[follow]
    follow_in, (o_ref, *follow_out), (stage2, sems) = rest[:n_in], rest[n_in:-2], rest[-2:]
    j = pl.program_id(0)
    buf = j % 2
    n_exp = pick_ref.shape[0]
    pick = pick_ref[...]
    total = _staging_rows(tot_ref, offs_ref, j, n_exp)
    n_rounds = (total + PICK_ROWS - 1) // PICK_ROWS

    def copy(b, e, slot, row, size):
        return pltpu.make_async_copy(ye_ref.at[pl.ds(e * slots + slot, size)],
                                     stage2.at[b, pl.ds(row, size)], sems.at[b])

    def start_round(jj, r, b):
        _for_each_piece(tot_ref, offs_ref, jj, n_exp, r * PICK_ROWS, PICK_ROWS,
                        lambda e, slot, row, size: copy(b, e, slot, row, size).start())

    def wait_round(jj, r, b):
        _for_each_piece(tot_ref, offs_ref, jj, n_exp, r * PICK_ROWS, PICK_ROWS,
                        lambda e, slot, row, size: copy(b, e, slot, row, size).wait())

    @pl.when((j == 0) & (n_rounds > 0))
    def _():
        start_round(0, 0, 0)

    @pl.when(j + 1 < pl.num_programs(0))
    def _():
        @pl.when(_rounds(tot_ref, offs_ref, j + 1, n_exp) > 0)
        def _():
            start_round(j + 1, 0, 1 - buf)

    o_ref[...] = jnp.zeros_like(o_ref)

    def one_round(r, carry):
        row0 = r * PICK_ROWS

        @pl.when(r > 0)
        def _():
            start_round(j, r, buf)

        wait_round(j, r, buf)
        live = lax.broadcasted_iota(jnp.int32, (PICK_ROWS, 1), 0) < total - row0
        y = jnp.where(live, stage2[buf], 0.0)
        onehot = _ones_where(_pick_onehot(pick, row0, PICK_ROWS), F32).T.astype(BF16)
        acc = o_ref[...]
        for _ in range(2):
            part = y.astype(BF16)
            acc = acc + jnp.dot(onehot, part, preferred_element_type=F32)
            y = y - part.astype(F32)
        o_ref[...] = acc
        return carry

    lax.fori_loop(0, n_rounds, one_round, 0)
    x_new = x_ref[...] + gate_ref[...] * o_ref[...]
    if follow == "final":
        o_ref[...] = _rms(x_new, follow_in[0][...])
    else:
        o_ref[...] = x_new
        if follow is not None:
            g_ref, sh_ref, sc_ref = follow_in[:3]
            _ada_apply(x_new, g_ref, sh_ref, sc_ref, tuple(follow_in[3:]) + tuple(follow_out), follow)


def combine(x, ye, pick, tot, offs, mod_l, gate_idx, cond_row, *, follow=None, g=None, mod_next=None, cs=None):
    nt, d = x.shape
    nbt, e, _ = pick.shape
    slots = ye.shape[1]
    rows = pl.BlockSpec((LANES, d), lambda j, *_: (j, 0))
    in_specs = [rows,
                pl.BlockSpec((None, e, LANES), lambda j, *_: (j, 0, 0)),
                pl.BlockSpec((None, 1, d), lambda j, *_: (cond_row(j * LANES) * N_MOD + gate_idx, 0, 0)),
                pl.BlockSpec(memory_space=pl.ANY)]
    args = [tot, offs, x, pick, mod_l, ye.reshape(e * slots, d)]
    out_shape, out_specs = [jax.ShapeDtypeStruct((nt, d), F32)], [rows]
    if follow is not None:
        in_specs.append(pl.BlockSpec((1, d), lambda j, *_: (0, 0)))
        args.append(g.reshape(1, d))
    if follow in ("plain", "dft"):
        in_specs += [pl.BlockSpec((None, 1, d), lambda j, *_: (cond_row(j * LANES) * N_MOD + 0, 0, 0)),
                     pl.BlockSpec((None, 1, d), lambda j, *_: (cond_row(j * LANES) * N_MOD + 1, 0, 0))]
        args += [mod_next, mod_next]
    if follow == "plain":
        out_shape.append(jax.ShapeDtypeStruct((nt, d), BF16))
        out_specs.append(rows)
    if follow == "dft":
        in_specs.append(pl.BlockSpec(cs.shape, lambda j, *_: (0, 0)))
        args.append(cs)
        out_shape.append(jax.ShapeDtypeStruct((2, nt, d), BF16))
        out_specs.append(pl.BlockSpec((2, LANES, d), lambda j, *_: (0, j, 0)))
    outs = pl.pallas_call(
        functools.partial(_combine_kernel, slots=slots, follow=follow),
        out_shape=tuple(out_shape),
        grid_spec=pltpu.PrefetchScalarGridSpec(
            num_scalar_prefetch=2, grid=(nbt,), in_specs=in_specs, out_specs=tuple(out_specs),
            scratch_shapes=[pltpu.VMEM((2, PICK_ROWS, d), F32), pltpu.SemaphoreType.DMA((2,))]),
        compiler_params=_params("arbitrary"), name="combine",
    )(*args)
    return outs if len(outs) > 1 else outs[0]


def _unpack_rows(words):
    lo = pltpu.bitcast(words << 16, F32)
    hi = pltpu.bitcast(words & jnp.uint32(0xFFFF0000), F32)
    return jnp.concatenate([lo, hi], axis=1).astype(BF16)


def _expert_weights(w_refs, w_f32, w_bf16, sems, layer):
    tn = w_f32.shape[2]
    n_col = pl.num_programs(1)
    _stationary_weights(
        w_refs, w_f32, w_bf16, sems, pl.program_id(0) * n_col + pl.program_id(1), pl.num_programs(0) * n_col,
        lambda ref, b: ref.at[layer, b // n_col, :, pl.ds(pl.multiple_of((b % n_col) * tn, tn), tn)], 2)


def _swiglu_kernel(a_ref, wg_ref, wu_ref, o_ref, w_f32, w_bf16, sems, *, layer):
    _expert_weights([wg_ref, wu_ref], w_f32, w_bf16, sems, layer)
    a = _unpack_rows(a_ref[...])
    gte = jnp.dot(a, w_bf16[0], preferred_element_type=F32)
    up = jnp.dot(a, w_bf16[1], preferred_element_type=F32)
    o_ref[...] = (gte * jax.nn.sigmoid(gte) * up).astype(o_ref.dtype)


def expert_swiglu(xe, w_gate, w_up, layer, *, tm=512, tn=512):
    e, c, _ = xe.shape
    k, f = w_gate.shape[-2:]
    tm, tn = _tile(c, tm), _tile(f, tn)
    w_spec = pl.BlockSpec(memory_space=pl.ANY)
    return pl.pallas_call(
        functools.partial(_swiglu_kernel, layer=layer),
        out_shape=jax.ShapeDtypeStruct((e, c, f), BF16),
        grid=(e, f // tn, c // tm),
        in_specs=[pl.BlockSpec((None, tm, k // 2), lambda x, j, i: (x, i, 0)), w_spec, w_spec],
        out_specs=pl.BlockSpec((None, tm, tn), lambda x, j, i: (x, i, j)),
        scratch_shapes=_weight_scratch(2, k, tn),
        compiler_params=_params("arbitrary", "arbitrary", "arbitrary"), name="expert_swiglu",
    )(xe, w_gate, w_up)


def _down_kernel(a_ref, w_ref, gate_ref, o_ref, w_f32, w_bf16, sems, *, layer):
    _expert_weights([w_ref], w_f32, w_bf16, sems, layer)
    acc = jnp.dot(a_ref[...], w_bf16[0], preferred_element_type=F32)
    o_ref[...] = acc * pltpu.bitcast(gate_ref[...], F32)[:, 0:1]


def expert_down(hid, w_down, xe, layer, *, tm=512, tn=2048):
    e, c, f = hid.shape
    d = w_down.shape[-1]
    tm, tn = _tile(c, tm), _tile(d, tn)
    gate_blk = (xe.shape[-1] - LANES) // LANES
    return pl.pallas_call(
        functools.partial(_down_kernel, layer=layer),
        out_shape=jax.ShapeDtypeStruct((e, c, d), F32),
        grid=(e, d // tn, c // tm),
        in_specs=[pl.BlockSpec((None, tm, f), lambda x, j, i: (x, i, 0)),
                  pl.BlockSpec(memory_space=pl.ANY),
                  pl.BlockSpec((None, tm, LANES), lambda x, j, i: (x, i, gate_blk))],
        out_specs=pl.BlockSpec((None, tm, tn), lambda x, j, i: (x, i, j)),
        scratch_shapes=_weight_scratch(1, f, tn),
        compiler_params=_params("arbitrary", "arbitrary", "arbitrary"), name="expert_down",
    )(hid, w_down, xe)


def _attn_kernel(q_ref, *rest, group, n_seg, chunk):
    kv_refs, o_ref = rest[:2 * n_seg], rest[2 * n_seg]
    tq = q_ref.shape[0]
    q = jnp.concatenate([q_ref[:, g * HEAD_DIM:(g + 1) * HEAD_DIM] for g in range(group)], axis=0)
    m = jnp.full((group * tq, 1), -jnp.inf, F32)
    acc = jnp.zeros((group * tq, 2 * HEAD_DIM), F32)
    for seg in range(n_seg):
        k_ref, v_ref = kv_refs[2 * seg], kv_refs[2 * seg + 1]
        length = k_ref.shape[0]
        step = math.gcd(length, chunk)
        for lo in range(0, length, step):
            k = k_ref[lo:lo + step, :].astype(BF16)
            v = v_ref[lo:lo + step, :].astype(BF16)
            v1 = jnp.concatenate([v, jnp.ones_like(v)], axis=1)
            s = lax.dot_general(q, k, (((1,), (1,)), ((), ())), preferred_element_type=F32)
            m_new = jnp.maximum(m, jnp.max(s, axis=-1, keepdims=True))
            p = jnp.exp2(s - m_new).astype(BF16)
            acc = jnp.exp2(m - m_new) * acc + jnp.dot(p, v1, preferred_element_type=F32)
            m = m_new
    o = acc[:, :HEAD_DIM] / acc[:, HEAD_DIM:HEAD_DIM + 1]
    for g in range(group):
        o_ref[:, g * HEAD_DIM:(g + 1) * HEAD_DIM] = o[g * tq:(g + 1) * tq].astype(o_ref.dtype)


def attention(q, segments, *, n_kv, n_req, q_row0, lq, tq, chunk=512):
    group = q.shape[-1] // HEAD_DIM // n_kv
    tq = _tile(lq, tq)
    qb0, nqb = q_row0 // tq, lq // tq
    qw = group * HEAD_DIM
    in_specs = [pl.BlockSpec((tq, qw), lambda r, h, t: (qb0 + r * nqb + t, h))]
    args = [q]
    for k_arr, v_arr, block, imap in segments:
        spec = pl.BlockSpec(block, lambda r, h, t, imap=imap: imap(r, h))
        in_specs += [spec, spec]
        args += [k_arr, v_arr]
    return pl.pallas_call(
        functools.partial(_attn_kernel, group=group, n_seg=len(segments), chunk=chunk),
        out_shape=jax.ShapeDtypeStruct((n_req * lq, q.shape[-1]), BF16),
        grid=(n_req, n_kv, nqb),
        in_specs=in_specs,
        out_specs=pl.BlockSpec((tq, qw), lambda r, h, t: (r * nqb + t, h)),
        compiler_params=_params("parallel", "parallel", "arbitrary"), name="attention",
    )(*args)


def _seq_dft_kernel(m_ref, u_ref, o_ref):
    acc = jnp.dot(m_ref[0], u_ref[0], preferred_element_type=F32)
    acc += jnp.dot(m_ref[1], u_ref[1], preferred_element_type=F32)
    o_ref[...] = acc.astype(o_ref.dtype)


def seq_dft(mats, u, *, n_req, row0, tm=512, tn=512):
    length = mats.shape[1]
    d = u.shape[-1]
    tm, tn = _tile(length, tm), _tile(d, tn)
    rb0 = row0 // length
    return pl.pallas_call(
        _seq_dft_kernel,
        out_shape=jax.ShapeDtypeStruct((n_req * length, d), BF16),
        grid=(n_req, d // tn, length // tm),
        in_specs=[pl.BlockSpec((2, tm, length), lambda r, j, i: (0, i, 0)),
                  pl.BlockSpec((2, length, tn), lambda r, j, i: (0, rb0 + r, j))],
        out_specs=pl.BlockSpec((tm, tn), lambda r, j, i: (r * (length // tm) + i, j)),
        compiler_params=_params("parallel", "parallel", "arbitrary"), name="seq_dft",
    )(mats, u)


def _dft_mats(length):
    n2 = 1 << ((length.bit_length() - 1) // 2)
    n1 = length // n2
    assert n1 * n2 == length
    j = jnp.arange(length, dtype=jnp.int32)[:, None]
    ang_a = ((j * jnp.arange(n1, dtype=jnp.int32)[None, :]) % n1).astype(F32) * (2.0 * math.pi / n1)
    ang_b = ((j * jnp.arange(n2, dtype=jnp.int32)[None, :]) % length).astype(F32) * (2.0 * math.pi / length)
    ca, sa = jnp.cos(ang_a)[:, :, None], jnp.sin(ang_a)[:, :, None]
    cb, sb = jnp.cos(ang_b)[:, None, :], jnp.sin(ang_b)[:, None, :]
    norm = 1.0 / math.sqrt(length)
    cos = (ca * cb - sa * sb).reshape(length, length) * norm
    sin = (sa * cb + ca * sb).reshape(length, length) * norm
    return cos, sin


def _rope_tables(n_prompt, n_req, n_lat):
    half = HEAD_DIM // 2
    rows = n_lat // GRID_W
    row_pos = jnp.repeat(jnp.arange(rows, dtype=F32), GRID_W)
    col_pos = jnp.tile(jnp.arange(GRID_W, dtype=F32), rows)
    inv_freq = ROPE_THETA ** (-jnp.arange(0, half, 2, dtype=F32) / half)

    def cs(pos):
        ang = pos[:, None] * inv_freq[None, :]
        return jnp.cos(ang), jnp.sin(ang)

    cr, sr = cs(row_pos)
    cc, sc = cs(col_pos)
    cos = jnp.concatenate([cr, cr, cc, cc], axis=-1)
    sin = jnp.concatenate([-sr, sr, -sc, sc], axis=-1)
    cos = jnp.concatenate([jnp.ones((n_prompt, HEAD_DIM), F32), jnp.tile(cos, (n_req, 1))], axis=0)
    sin = jnp.concatenate([jnp.zeros((n_prompt, HEAD_DIM), F32), jnp.tile(sin, (n_req, 1))], axis=0)
    return cos, sin


def kernel(x_prompt, x_sample, cache_k, cache_v, c, c_ctx, norm_g, w_mod, b_mod, w_q, w_k, w_v, w_o,
           q_norm_g, k_norm_g, w_fourier, w_router, w_gate, w_up, w_down, final_norm_g):
    batch, seq, d = x_prompt.shape
    n_req, n_lat, _ = x_sample.shape
    depth = w_mod.shape[0]
    n_kv = cache_k.shape[3]
    n_exp = w_router.shape[-1]
    n_prompt = batch * seq
    nt = n_prompt + n_req * n_lat
    assert n_req + 1 <= COND_ROWS
    assert n_prompt % n_lat == 0

    def cond_row(r):
        return jnp.where(r < n_prompt, 0, 1 + (r - n_prompt) // n_lat)

    cond = jnp.zeros((COND_ROWS, d), F32).at[0].set(c_ctx).at[1:1 + n_req].set(c)
    mod = modulation_all(cond, w_mod, b_mod)
    cos, sin = _rope_tables(n_prompt, n_req, n_lat)
    scale = math.log2(math.e) / math.sqrt(HEAD_DIM)

    cc, sc = _dft_mats(FGROUP_DIM)
    cs_chan = jnp.concatenate([cc, sc], axis=1).astype(BF16)
    mats_p = mats_s = None
    if depth > 1:
        cp, sp = _dft_mats(seq)
        mats_p = jnp.stack([cp, -sp]).astype(BF16)
        cl, sl = _dft_mats(n_lat)
        mats_s = jnp.stack([cl, -sl]).astype(BF16)

    x = jnp.concatenate([x_prompt.reshape(n_prompt, d), x_sample.reshape(n_req * n_lat, d)], axis=0)
    assert n_prompt % LANES == 0 and (n_req * n_lat) % LANES == 0
    cap_p = (CAPACITY_FACTOR * n_prompt) // n_exp
    cap_s = (CAPACITY_FACTOR * n_req * n_lat) // n_exp
    groups = ((0, n_prompt // LANES, cap_p, 0), (n_prompt // LANES, n_req * n_lat // LANES, cap_s, cap_p))
    slots = cap_p + cap_s
    assert cap_p % SUBLANES == 0 and cap_s % SUBLANES == 0
    new_k, new_v = [], []
    mixer_in = None
    for i in range(depth):
        mod_l = mod[i].reshape(COND_ROWS * N_MOD, 1, d)
        j = i // 2
        if i % 2 == 0:
            h = mixer_in if i > 0 else ada_norm(x, norm_g[i, 0], mod_l, 0, 1, cond_row)
            q = project(h, w_q, j, out_dtypes=(BF16,),
                        head=(q_norm_g[j].reshape(1, HEAD_DIM) * scale, cos, sin))
            kf, kb = project(h, w_k, j, out_dtypes=(F32, BF16),
                             head=(k_norm_g[j].reshape(1, HEAD_DIM), cos, sin))
            vf, vb = project(h, w_v, j, out_dtypes=(F32, BF16))
            kvw = n_kv * HEAD_DIM
            new_k.append(kf[:n_prompt].reshape(batch, seq, n_kv, HEAD_DIM))
            new_v.append(vf[:n_prompt].reshape(batch, seq, n_kv, HEAD_DIM))
            att_p = attention(q, [(kb, vb, (seq, HEAD_DIM), lambda r, h: (r, h))],
                              n_kv=n_kv, n_req=batch, q_row0=0, lq=seq, tq=256)
            past = cache_k.shape[2]
            lat0 = n_prompt // n_lat
            att_s = attention(
                q,
                [(cache_k.reshape(n_req, -1, past, kvw), cache_v.reshape(n_req, -1, past, kvw),
                  (None, None, past, HEAD_DIM), lambda r, h, j=j: (r, j, 0, h)),
                 (kb, vb, (n_lat, HEAD_DIM), lambda r, h: (lat0 + r, h))],
                n_kv=n_kv, n_req=n_req, q_row0=n_prompt, lq=n_lat, tq=256)
            x = project((att_p, att_s), w_o, j, out_dtypes=(F32,), residual=(x, mod_l, 2, cond_row))
        else:
            u = mixer_in if i > 0 else ada_norm(x, norm_g[i, 0], mod_l, 0, 1, cond_row, mode="dft", cs=cs_chan)
            f_p = seq_dft(mats_p, u, n_req=batch, row0=0, tn=d)
            f_s = seq_dft(mats_s, u, n_req=n_req, row0=n_prompt)
            x = project((f_p, f_s), w_fourier, j, out_dtypes=(F32,), residual=(x, mod_l, 2, cond_row))

        h, aff = ada_norm(x, norm_g[i, 1], mod_l, 3, 4, cond_row, mode="router",
                          w_router_t=w_router[i].T)
        pick, tot, offs = route_select(aff, groups)
        xe = dispatch(h, pick, aff, tot, offs, slots)
        hid = expert_swiglu(xe, w_gate, w_up, i)
        ye = expert_down(hid, w_down, xe, i)
        if i + 1 < depth:
            follow = "plain" if (i + 1) % 2 == 0 else "dft"
            x, mixer_in = combine(x, ye, pick, tot, offs, mod_l, 5, cond_row, follow=follow, g=norm_g[i + 1, 0],
                                  mod_next=mod[i + 1].reshape(COND_ROWS * N_MOD, 1, d),
                                  cs=cs_chan if follow == "dft" else None)
        else:
            y = combine(x, ye, pick, tot, offs, mod_l, 5, cond_row, follow="final", g=final_norm_g)

    y_prompt = y[:n_prompt].reshape(batch, seq, d)
    y_sample = y[n_prompt:].reshape(n_req, n_lat, d)
    return (y_prompt, y_sample, jnp.stack(new_k, axis=1), jnp.stack(new_v, axis=1))
```

```python
import functools
import math

import jax
import jax.numpy as jnp
from jax import lax
from jax.experimental import pallas as pl
from jax.experimental.pallas import tpu as pltpu

F32 = jnp.float32
BF16 = jnp.bfloat16

GRID_W = 64
HEAD_DIM = 128
FGROUP_DIM = 128
ROPE_THETA = 10000.0
CAPACITY_FACTOR = 2
N_MOD = 6
EPS = 1e-6

LANES = 128
SUBLANES = 8
MXU_COLS = 256
COND_ROWS = 8
VMEM_LIMIT_BYTES = 56 * 1024 * 1024


def _params(*sem):
    return pltpu.CompilerParams(dimension_semantics=sem, vmem_limit_bytes=VMEM_LIMIT_BYTES)


def _tile(n, want):
    return math.gcd(n, want)


def _mod_kernel(c_ref, w_ref, b_ref, o_ref):
    c = c_ref[...]
    a = (c * jax.nn.sigmoid(c)).astype(BF16)
    o_ref[...] = jnp.dot(a, w_ref[...].astype(BF16), preferred_element_type=F32) + b_ref[...]


def modulation_all(cond, w_mod, b_mod):
    depth, d, n = w_mod.shape
    tn = _tile(n, 512)
    return pl.pallas_call(
        _mod_kernel,
        out_shape=jax.ShapeDtypeStruct((depth, COND_ROWS, n), F32),
        grid=(depth, n // tn),
        in_specs=[
            pl.BlockSpec((COND_ROWS, d), lambda l, j: (0, 0)),
            pl.BlockSpec((None, d, tn), lambda l, j: (l, 0, j)),
            pl.BlockSpec((None, 1, tn), lambda l, j: (l, 0, j)),
        ],
        out_specs=pl.BlockSpec((None, COND_ROWS, tn), lambda l, j: (l, 0, j)),
        compiler_params=_params("parallel", "parallel"),
        name="modulation",
    )(cond, w_mod, b_mod.reshape(depth, 1, n))


def _rms(x, g):
    return x * lax.rsqrt(jnp.mean(x * x, axis=-1, keepdims=True) + EPS) * g


def _split_rows(specs_of, first, rest_rows, tm):
    split = first.shape[0] // tm
    assert split * tm == first.shape[0] and rest_rows.shape[0] % tm == 0
    return (specs_of(lambda i: jnp.minimum(i, split - 1)), specs_of(lambda i: jnp.maximum(i - split, 0))), split


def _ada_kernel(*refs, mode, split):
    if split is None:
        x_ref, g_ref, sh_ref, sc_ref, *rest = refs
        _ada_apply(x_ref[...], g_ref, sh_ref, sc_ref, rest, mode)
        return
    x0_ref, x1_ref, g_ref, sh_ref, sc_ref, *rest = refs

    @pl.when(pl.program_id(0) < split)
    def _():
        _ada_apply(x0_ref[...], g_ref, sh_ref, sc_ref, rest, mode)

    @pl.when(pl.program_id(0) >= split)
    def _():
        _ada_apply(x1_ref[...], g_ref, sh_ref, sc_ref, rest, mode)


def _ada_apply(x, g_ref, sh_ref, sc_ref, rest, mode):
    h = _rms(x, g_ref[...]) * (1.0 + sc_ref[...]) + sh_ref[...]
    hb = h.astype(BF16)
    if mode == "plain":
        (h_ref,) = rest
        h_ref[...] = hb
    elif mode == "router":
        wr_ref, h_ref, aff_ref = rest
        h_ref[...] = hb
        logits = lax.dot_general(wr_ref[...].astype(BF16), hb, (((1,), (1,)), ((), ())),
                                 preferred_element_type=F32)
        m = jnp.max(logits, axis=0, keepdims=True)
        p = jnp.exp(logits - m)
        aff = p / jnp.sum(p, axis=0, keepdims=True)
        for blk in range(aff_ref.shape[0]):
            aff_ref[blk] = aff[:, blk * LANES:(blk + 1) * LANES]
    else:
        cs_ref, u_ref = rest
        n_groups = hb.shape[1] // FGROUP_DIM
        for grp in range(n_groups):
            lo = grp * FGROUP_DIM
            r = jnp.dot(hb[:, lo:lo + FGROUP_DIM], cs_ref[...], preferred_element_type=F32)
            u_ref[0, :, lo:lo + FGROUP_DIM] = r[:, :FGROUP_DIM].astype(BF16)
            u_ref[1, :, lo:lo + FGROUP_DIM] = r[:, FGROUP_DIM:].astype(BF16)


def ada_norm(x, g, mod_l, shift_idx, scale_idx, cond_row, *, mode="plain", w_router_t=None, cs=None):
    row = lambda i: (i, 0)
    if isinstance(x, tuple):
        d = x[0].shape[1]
        nt = x[0].shape[0] + x[1].shape[0]
        tm = _tile(math.gcd(x[0].shape[0], x[1].shape[0]), 256)
        x_specs, split = _split_rows(lambda blk: pl.BlockSpec((tm, d), lambda i: (blk(i), 0)), x[0], x[1], tm)
        x_specs, x_args = list(x_specs), list(x)
    else:
        nt, d = x.shape
        tm = _tile(nt, 256)
        x_specs, x_args, split = [pl.BlockSpec((tm, d), row)], [x], None
    grid = (nt // tm,)
    in_specs = x_specs + [
        pl.BlockSpec((1, d), lambda i: (0, 0)),
        pl.BlockSpec((None, 1, d), lambda i: (cond_row(i * tm) * N_MOD + shift_idx, 0, 0)),
        pl.BlockSpec((None, 1, d), lambda i: (cond_row(i * tm) * N_MOD + scale_idx, 0, 0)),
    ]
    args = x_args + [g.reshape(1, d), mod_l, mod_l]
    if mode == "plain":
        out_shape = jax.ShapeDtypeStruct((nt, d), BF16)
        out_specs = pl.BlockSpec((tm, d), row)
    elif mode == "router":
        e = w_router_t.shape[0]
        in_specs.append(pl.BlockSpec((e, d), lambda i: (0, 0)))
        args.append(w_router_t)
        out_shape = (jax.ShapeDtypeStruct((nt, d), BF16), jax.ShapeDtypeStruct((nt // LANES, e, LANES), F32))
        out_specs = (pl.BlockSpec((tm, d), row), pl.BlockSpec((tm // LANES, e, LANES), lambda i: (i, 0, 0)))
    else:
        in_specs.append(pl.BlockSpec(cs.shape, lambda i: (0, 0)))
        args.append(cs)
        out_shape = jax.ShapeDtypeStruct((2, nt, d), BF16)
        out_specs = pl.BlockSpec((2, tm, d), lambda i: (0, i, 0))
    return pl.pallas_call(
        functools.partial(_ada_kernel, mode=mode, split=split),
        out_shape=out_shape, grid=grid, in_specs=in_specs, out_specs=out_specs,
        compiler_params=_params("parallel"), name="ada_norm_" + mode,
    )(*args)


def _stationary_weights(w_hbm, w_f32, w_bf16, sems, block, n_blocks, slicer, inner_axis):
    def copy(b, which):
        return pltpu.make_async_copy(slicer(w_hbm[which], b), w_f32.at[which], sems.at[which])

    @pl.when(pl.program_id(inner_axis) == 0)
    def _():
        @pl.when(block == 0)
        def _():
            for which in range(len(w_hbm)):
                copy(block, which).start()

        for which in range(len(w_hbm)):
            copy(block, which).wait()
            w_bf16[which] = w_f32[which].astype(BF16)

        @pl.when(block + 1 < n_blocks)
        def _():
            for which in range(len(w_hbm)):
                copy(block + 1, which).start()


def _weight_scratch(n_w, k, tn):
    return [pltpu.VMEM((n_w, k, tn), F32), pltpu.VMEM((n_w, k, tn), BF16), pltpu.SemaphoreType.DMA((n_w,))]


def _project_weights(w_ref, w_f32, w_bf16, sems, layer):
    tn = w_f32.shape[2]
    _stationary_weights(
        [w_ref], w_f32, w_bf16, sems, pl.program_id(0), pl.num_programs(0),
        lambda ref, b: ref.at[layer, :, pl.ds(pl.multiple_of(b * tn, tn), tn)], 1)


def _mm_plain_kernel(a_ref, w_ref, *rest, n_out, layer):
    outs, (w_f32, w_bf16, sems) = rest[:n_out], rest[n_out:]
    _project_weights(w_ref, w_f32, w_bf16, sems, layer)
    acc = jnp.dot(a_ref[...], w_bf16[0], preferred_element_type=F32)
    for o_ref in outs:
        o_ref[...] = acc.astype(o_ref.dtype)


def _swap32(y):
    lane = lax.broadcasted_iota(jnp.int32, y.shape, 1)
    first = (lane % 64) < 32
    return jnp.where(first, pltpu.roll(y, 96, 1), pltpu.roll(y, 32, 1))


def _mm_headnorm_kernel(a_ref, w_ref, g_ref, cos_ref, sin_ref, *rest, n_out, layer):
    outs, (w_f32, w_bf16, sems) = rest[:n_out], rest[n_out:]
    _project_weights(w_ref, w_f32, w_bf16, sems, layer)
    a = a_ref[...]
    cos, sin = cos_ref[...], sin_ref[...]
    tn = w_bf16.shape[2]
    step = math.gcd(tn, MXU_COLS)
    for c0 in range(0, tn, step):
        acc = jnp.dot(a, w_bf16[0, :, c0:c0 + step], preferred_element_type=F32)
        for lo in range(0, step, HEAD_DIM):
            y = _rms(acc[:, lo:lo + HEAD_DIM], g_ref[...])
            y = y * cos + _swap32(y) * sin
            for o_ref in outs:
                o_ref[:, c0 + lo:c0 + lo + HEAD_DIM] = y.astype(o_ref.dtype)


def _mm_residual_kernel(a0_ref, a1_ref, w_ref, *rest, layer, split, res_split):
    if res_split:
        res0_ref, res1_ref, gate_ref, o_ref, w_f32, w_bf16, sems = rest
    else:
        res0_ref, gate_ref, o_ref, w_f32, w_bf16, sems = rest
        res1_ref = res0_ref
    _project_weights(w_ref, w_f32, w_bf16, sems, layer)

    def emit(a_ref, res_ref):
        acc = jnp.dot(a_ref[...], w_bf16[0], preferred_element_type=F32)
        o_ref[...] = res_ref[...] + gate_ref[...] * acc

    @pl.when(pl.program_id(1) < split)
    def _():
        emit(a0_ref, res0_ref)

    @pl.when(pl.program_id(1) >= split)
    def _():
        emit(a1_ref, res1_ref)


def project(a, w_stack, layer, *, out_dtypes, head=None, residual=None, tm=512, tn=1024):
    n = w_stack.shape[-1]
    if residual is not None:
        a0, a1 = a
        k = a0.shape[1]
        m = a0.shape[0] + a1.shape[0]
        tm, tn = _tile(math.gcd(a0.shape[0], a1.shape[0]), tm), _tile(n, tn)
        split = a0.shape[0] // tm
        in_specs = [pl.BlockSpec((tm, k), lambda j, i: (jnp.minimum(i, split - 1), 0)),
                    pl.BlockSpec((tm, k), lambda j, i: (jnp.maximum(i - split, 0), 0)),
                    pl.BlockSpec(memory_space=pl.ANY)]
        args = [a0, a1, w_stack]
    else:
        m, k = a.shape
        tm, tn = _tile(m, tm), _tile(n, tn)
        in_specs = [pl.BlockSpec((tm, k), lambda j, i: (i, 0)), pl.BlockSpec(memory_space=pl.ANY)]
        args = [a, w_stack]
    grid = (n // tn, m // tm)
    out_block = pl.BlockSpec((tm, tn), lambda j, i: (i, j))
    if head is not None:
        g, cos, sin = head
        in_specs += [pl.BlockSpec((1, HEAD_DIM), lambda j, i: (0, 0)),
                     pl.BlockSpec((tm, HEAD_DIM), lambda j, i: (i, 0)),
                     pl.BlockSpec((tm, HEAD_DIM), lambda j, i: (i, 0))]
        args += [g, cos, sin]
        body = functools.partial(_mm_headnorm_kernel, n_out=len(out_dtypes), layer=layer)
    elif residual is not None:
        res, mod_l, gate_idx, cond_row = residual
        if isinstance(res, tuple):
            assert res[0].shape[0] == a0.shape[0]
            in_specs += [pl.BlockSpec((tm, tn), lambda j, i: (jnp.minimum(i, split - 1), j)),
                         pl.BlockSpec((tm, tn), lambda j, i: (jnp.maximum(i - split, 0), j))]
            args += list(res)
        else:
            in_specs.append(pl.BlockSpec((tm, tn), lambda j, i: (i, j)))
            args.append(res)
        in_specs.append(pl.BlockSpec((None, 1, tn), lambda j, i: (cond_row(i * tm) * N_MOD + gate_idx, 0, j)))
        args.append(mod_l)
        body = functools.partial(_mm_residual_kernel, layer=layer, split=split, res_split=isinstance(res, tuple))
    else:
        body = functools.partial(_mm_plain_kernel, n_out=len(out_dtypes), layer=layer)
    outs = pl.pallas_call(
        body,
        out_shape=tuple(jax.ShapeDtypeStruct((m, n), dt) for dt in out_dtypes),
        grid=grid, in_specs=in_specs, out_specs=tuple(out_block for _ in out_dtypes),
        scratch_shapes=_weight_scratch(1, k, tn),
        compiler_params=_params("arbitrary", "arbitrary"), name="project",
    )(*args)
    return outs if len(outs) > 1 else outs[0]


PICK_ROWS = 512


def _ones_where(mask, dtype):
    return jnp.where(mask, 1.0, 0.0).astype(dtype)


def _select_kernel(aff_ref, pick_ref, tot_ref, offs_ref, w_scr, o_scr, *, groups):
    n_exp = aff_ref.shape[1]
    li = lax.broadcasted_iota(jnp.int32, (LANES, LANES), 0)
    lj = lax.broadcasted_iota(jnp.int32, (LANES, LANES), 1)
    tri = _ones_where(li <= lj, BF16)

    def chunk_cumsum(mask, nb):
        w = jnp.dot(_ones_where(mask, BF16).reshape(nb * n_exp, LANES), tri,
                    preferred_element_type=F32).reshape(nb, n_exp, LANES)
        w_scr[0:nb] = w

        def body(j, run):
            o_scr[j] = jnp.broadcast_to(run, (n_exp, LANES))
            return run + w_scr[j][:, LANES - 1:LANES]

        lax.fori_loop(0, nb, body, jnp.zeros((n_exp, 1), F32))
        return w, o_scr[0:nb]

    for j0, nb, cap, slot0 in groups:
        keys = pltpu.bitcast(aff_ref[j0:j0 + nb], jnp.int32)

        def count(mask):
            return jnp.sum(jnp.sum(_ones_where(mask, F32), axis=0), axis=1, keepdims=True)

        def search(it, prefix):
            cand = prefix | lax.shift_left(jnp.int32(1), 30 - it)
            return jnp.where(count(keys >= cand[None]) >= cap, cand, prefix)

        thr = lax.fori_loop(0, 31, search, jnp.zeros((n_exp, 1), jnp.int32))[None]
        above, ties = keys > thr, keys == thr
        need = (cap - count(above))[None]
        w_eq, o_eq = chunk_cumsum(ties, nb)
        sel = above | (ties & (w_eq + o_eq <= need))
        w, offs = chunk_cumsum(sel, nb)
        tot = jnp.broadcast_to(w[:, :, LANES - 1:LANES], w.shape)
        lead = offs - SUBLANES * jnp.floor(offs / SUBLANES)
        seg = SUBLANES * jnp.floor((lead + tot + (SUBLANES - 1)) / SUBLANES)
        rows = nb * n_exp
        blk = math.gcd(rows, 256)
        bi = lax.broadcasted_iota(jnp.int32, (blk, blk), 0)
        bj = lax.broadcasted_iota(jnp.int32, (blk, blk), 1)
        lower = _ones_where((bi // n_exp == bj // n_exp) & (bj % n_exp < bi % n_exp), BF16)
        seg2 = seg.reshape(rows, LANES).astype(BF16)
        base = jnp.concatenate(
            [jnp.dot(lower, seg2[r0:r0 + blk], preferred_element_type=F32) for r0 in range(0, rows, blk)],
            axis=0).reshape(nb, n_exp, LANES)
        pick_ref[j0:j0 + nb] = jnp.where(sel, base + lead + w - 1.0, -1.0).astype(jnp.int32)
        tot_ref[j0:j0 + nb] = tot.astype(jnp.int32)
        offs_ref[j0:j0 + nb] = offs.astype(jnp.int32) + slot0


def route_select(aff, groups):
    nbt, e, _ = aff.shape
    shape = jax.ShapeDtypeStruct(aff.shape, jnp.int32)
    max_nb = max(g[1] for g in groups)
    pick, tot, offs = pl.pallas_call(
        functools.partial(_select_kernel, groups=groups),
        out_shape=(shape, shape, shape),
        scratch_shapes=[pltpu.VMEM((max_nb, e, LANES), F32), pltpu.VMEM((max_nb, e, LANES), F32)],
        compiler_params=pltpu.CompilerParams(vmem_limit_bytes=VMEM_LIMIT_BYTES), name="route_select",
    )(aff)
    return pick, tot[:, :, 0].reshape(nbt * e), offs[:, :, 0].reshape(nbt * e)


def _pick_onehot(pick, row0, rows):
    ids = lax.broadcasted_iota(jnp.int32, (rows, LANES), 0) + row0
    hit = ids == pick[0:1, :]
    for e in range(1, pick.shape[0]):
        hit = hit | (ids == pick[e:e + 1, :])
    return hit


def _segment(tot_ref, offs_ref, j, n_exp, e):
    t = tot_ref[j * n_exp + e]
    o = offs_ref[j * n_exp + e]
    r = o & (SUBLANES - 1)
    return o, r, ((r + t + SUBLANES - 1) // SUBLANES) * SUBLANES


def _for_each_piece(tot_ref, offs_ref, j, n_exp, row0, rows, fn):
    max_groups = (SUBLANES - 1 + LANES + SUBLANES - 1) // SUBLANES

    def per_expert(e, base):
        o, r, seg = _segment(tot_ref, offs_ref, j, n_exp, e)
        lo = jnp.maximum(base, row0)
        hi = jnp.minimum(base + seg, row0 + rows)
        n = jnp.maximum(hi - lo, 0)
        slot = (o - r) + (lo - base)
        done = jnp.int32(0)
        for bit in range(max_groups.bit_length() - 1, -1, -1):
            size = SUBLANES << bit

            @pl.when((n & size) != 0)
            def _():
                fn(e, pl.multiple_of(slot + done, SUBLANES), pl.multiple_of(lo - row0 + done, SUBLANES), size)

            done = done + (n & size)
        return base + seg

    lax.fori_loop(0, n_exp, per_expert, jnp.int32(0))


def _staging_rows(tot_ref, offs_ref, j, n_exp):
    return lax.fori_loop(0, n_exp, lambda e, s: s + _segment(tot_ref, offs_ref, j, n_exp, e)[2], jnp.int32(0))


def _rounds(tot_ref, offs_ref, j, n_exp):
    return (_staging_rows(tot_ref, offs_ref, j, n_exp) + PICK_ROWS - 1) // PICK_ROWS


def _dispatch_kernel(tot_ref, offs_ref, h_ref, pick_ref, aff_ref, xe_ref, stage2, tail, sems):
    j = pl.program_id(0)
    buf = j % 2
    stage = stage2.at[buf]
    n_exp = pick_ref.shape[0]
    half = h_ref.shape[1] // 2
    pick, aff, h = pick_ref[...], aff_ref[...], h_ref[...]
    n_rounds = _rounds(tot_ref, offs_ref, j, n_exp)

    @pl.when(j == 0)
    def _():
        tail[...] = jnp.zeros_like(tail)

    def copy(b, e, slot, row, size):
        return pltpu.make_async_copy(stage2.at[b, pl.ds(row, size)], xe_ref.at[e, pl.ds(slot, size)], sems.at[b])

    def start_round(jj, r, b):
        _for_each_piece(tot_ref, offs_ref, jj, n_exp, r * PICK_ROWS, PICK_ROWS,
                        lambda e, slot, row, size: copy(b, e, slot, row, size).start())

    def wait_round(jj, r, b):
        _for_each_piece(tot_ref, offs_ref, jj, n_exp, r * PICK_ROWS, PICK_ROWS,
                        lambda e, slot, row, size: copy(b, e, slot, row, size).wait())

    def splice_tails(row0):
        def per_expert(e, base):
            _, r, seg = _segment(tot_ref, offs_ref, j, n_exp, e)
            last = base + seg - SUBLANES

            @pl.when((r > 0) & (base >= row0) & (base < row0 + PICK_ROWS))
            def _():
                row = pl.multiple_of(base - row0, SUBLANES)
                keep = lax.broadcasted_iota(jnp.int32, (SUBLANES, 1), 0) < r
                stage[pl.ds(row, SUBLANES), :] = jnp.where(keep, tail[e], stage[pl.ds(row, SUBLANES), :])

            @pl.when((seg > 0) & (last >= row0) & (last < row0 + PICK_ROWS))
            def _():
                tail[e] = stage[pl.ds(pl.multiple_of(last - row0, SUBLANES), SUBLANES), :]

            return base + seg

        lax.fori_loop(0, n_exp, per_expert, jnp.int32(0))

    def build(r):
        row0 = r * PICK_ROWS
        ids = lax.broadcasted_iota(jnp.int32, (PICK_ROWS, LANES), 0) + row0
        hit = jnp.zeros((PICK_ROWS, LANES), jnp.bool_)
        gate = jnp.zeros((PICK_ROWS, LANES), F32)
        for e in range(n_exp):
            he = ids == pick[e:e + 1, :]
            hit = hit | he
            gate = jnp.where(he, aff[e:e + 1, :], gate)
        rows = jnp.dot(_ones_where(hit, BF16), h, preferred_element_type=F32)
        bits = pltpu.bitcast(rows, jnp.uint32)
        stage[:, 0:half] = bits[:, half:] | (bits[:, :half] >> 16)
        g = jnp.sum(gate, axis=1, keepdims=True)
        stage[:, half:half + LANES] = pltpu.bitcast(jnp.broadcast_to(g, (PICK_ROWS, LANES)), jnp.uint32)
        splice_tails(row0)

    @pl.when(n_rounds > 0)
    def _():
        build(0)

    @pl.when(j > 0)
    def _():
        prev_rounds = _rounds(tot_ref, offs_ref, j - 1, n_exp)

        @pl.when(prev_rounds > 0)
        def _():
            wait_round(j - 1, prev_rounds - 1, 1 - buf)

    @pl.when(n_rounds > 0)
    def _():
        start_round(j, 0, buf)

    def later_round(r, carry):
        wait_round(j, r - 1, buf)
        build(r)
        start_round(j, r, buf)
        return carry

    lax.fori_loop(1, n_rounds, later_round, 0)

    @pl.when((j == pl.num_programs(0) - 1) & (n_rounds > 0))
    def _():
        wait_round(j, n_rounds - 1, buf)


def dispatch(h, pick, aff, tot, offs, slots):
    nt, d = h.shape
    nbt, e, _ = pick.shape
    width = d // 2 + LANES
    chunk = pl.BlockSpec((None, e, LANES), lambda j, *_: (j, 0, 0))
    return pl.pallas_call(
        _dispatch_kernel,
        out_shape=jax.ShapeDtypeStruct((e, slots, width), jnp.uint32),
        grid_spec=pltpu.PrefetchScalarGridSpec(
            num_scalar_prefetch=2, grid=(nbt,),
            in_specs=[pl.BlockSpec((LANES, d), lambda j, *_: (j, 0)), chunk, chunk],
            out_specs=pl.BlockSpec(memory_space=pl.ANY),
            scratch_shapes=[pltpu.VMEM((2, PICK_ROWS, width), jnp.uint32),
                            pltpu.VMEM((e, SUBLANES, width), jnp.uint32), pltpu.SemaphoreType.DMA((2,))]),
        compiler_params=_params("arbitrary"), name="dispatch",
    )(tot, offs, h, pick, aff)


def _combine_kernel(tot_ref, offs_ref, x_ref, pick_ref, gate_ref, ye_ref, *rest, slots, follow, split):
    n_in = {None: 0, "final": 1, "plain": 3, "dft": 4}[follow]
    follow_in, (o_ref, *follow_out), (stage2, acc_ref, sems) = rest[:n_in], rest[n_in:-3], rest[-3:]
    j = pl.program_id(0)
    buf = j % 2
    n_exp = pick_ref.shape[0]
    pick = pick_ref[...]
    total = _staging_rows(tot_ref, offs_ref, j, n_exp)
    n_rounds = (total + PICK_ROWS - 1) // PICK_ROWS

    def copy(b, e, slot, row, size):
        return pltpu.make_async_copy(ye_ref.at[pl.ds(e * slots + slot, size)],
                                     stage2.at[b, pl.ds(row, size)], sems.at[b])

    def start_round(jj, r, b):
        _for_each_piece(tot_ref, offs_ref, jj, n_exp, r * PICK_ROWS, PICK_ROWS,
                        lambda e, slot, row, size: copy(b, e, slot, row, size).start())

    def wait_round(jj, r, b):
        _for_each_piece(tot_ref, offs_ref, jj, n_exp, r * PICK_ROWS, PICK_ROWS,
                        lambda e, slot, row, size: copy(b, e, slot, row, size).wait())

    @pl.when((j == 0) & (n_rounds > 0))
    def _():
        start_round(0, 0, 0)

    @pl.when(j + 1 < pl.num_programs(0))
    def _():
        @pl.when(_rounds(tot_ref, offs_ref, j + 1, n_exp) > 0)
        def _():
            start_round(j + 1, 0, 1 - buf)

    acc_ref[...] = jnp.zeros_like(acc_ref)

    def one_round(r, carry):
        row0 = r * PICK_ROWS

        @pl.when(r > 0)
        def _():
            start_round(j, r, buf)

        wait_round(j, r, buf)
        live = lax.broadcasted_iota(jnp.int32, (PICK_ROWS, 1), 0) < total - row0
        y = jnp.where(live, stage2[buf], 0.0)
        onehot = _ones_where(_pick_onehot(pick, row0, PICK_ROWS), F32).T.astype(BF16)
        acc = acc_ref[...]
        for _ in range(2):
            part = y.astype(BF16)
            acc = acc + jnp.dot(onehot, part, preferred_element_type=F32)
            y = y - part.astype(F32)
        acc_ref[...] = acc
        return carry

    lax.fori_loop(0, n_rounds, one_round, 0)
    x_new = x_ref[...] + gate_ref[...] * acc_ref[...]
    if follow == "final":
        y_out = _rms(x_new, follow_in[0][...])

        @pl.when(j < split)
        def _():
            o_ref[...] = y_out

        @pl.when(j >= split)
        def _():
            follow_out[0][...] = y_out
    else:
        o_ref[...] = x_new
        if follow is not None:
            g_ref, sh_ref, sc_ref = follow_in[:3]
            _ada_apply(x_new, g_ref, sh_ref, sc_ref, tuple(follow_in[3:]) + tuple(follow_out), follow)


def combine(x, ye, pick, tot, offs, mod_l, gate_idx, cond_row, *, follow=None, g=None, mod_next=None, cs=None,
            first_rows=None):
    nt, d = x.shape
    nbt, e, _ = pick.shape
    slots = ye.shape[1]
    rows = pl.BlockSpec((LANES, d), lambda j, *_: (j, 0))
    in_specs = [rows,
                pl.BlockSpec((None, e, LANES), lambda j, *_: (j, 0, 0)),
                pl.BlockSpec((None, 1, d), lambda j, *_: (cond_row(j * LANES) * N_MOD + gate_idx, 0, 0)),
                pl.BlockSpec(memory_space=pl.ANY)]
    args = [tot, offs, x, pick, mod_l, ye.reshape(e * slots, d)]
    out_shape, out_specs, split = [jax.ShapeDtypeStruct((nt, d), F32)], [rows], None
    if follow is not None:
        in_specs.append(pl.BlockSpec((1, d), lambda j, *_: (0, 0)))
        args.append(g.reshape(1, d))
    if follow == "final":
        split = first_rows // LANES
        assert split * LANES == first_rows
        out_shape = [jax.ShapeDtypeStruct((first_rows, d), F32), jax.ShapeDtypeStruct((nt - first_rows, d), F32)]
        out_specs = [pl.BlockSpec((LANES, d), lambda j, *_: (jnp.minimum(j, split - 1), 0)),
                     pl.BlockSpec((LANES, d), lambda j, *_: (jnp.maximum(j - split, 0), 0))]
    if follow in ("plain", "dft"):
        in_specs += [pl.BlockSpec((None, 1, d), lambda j, *_: (cond_row(j * LANES) * N_MOD + 0, 0, 0)),
                     pl.BlockSpec((None, 1, d), lambda j, *_: (cond_row(j * LANES) * N_MOD + 1, 0, 0))]
        args += [mod_next, mod_next]
    if follow == "plain":
        out_shape.append(jax.ShapeDtypeStruct((nt, d), BF16))
        out_specs.append(rows)
    if follow == "dft":
        in_specs.append(pl.BlockSpec(cs.shape, lambda j, *_: (0, 0)))
        args.append(cs)
        out_shape.append(jax.ShapeDtypeStruct((2, nt, d), BF16))
        out_specs.append(pl.BlockSpec((2, LANES, d), lambda j, *_: (0, j, 0)))
    outs = pl.pallas_call(
        functools.partial(_combine_kernel, slots=slots, follow=follow, split=split),
        out_shape=tuple(out_shape),
        grid_spec=pltpu.PrefetchScalarGridSpec(
            num_scalar_prefetch=2, grid=(nbt,), in_specs=in_specs, out_specs=tuple(out_specs),
            scratch_shapes=[pltpu.VMEM((2, PICK_ROWS, d), F32), pltpu.VMEM((LANES, d), F32),
                            pltpu.SemaphoreType.DMA((2,))]),
        compiler_params=_params("arbitrary"), name="combine",
    )(*args)
    return outs if len(outs) > 1 else outs[0]


def _unpack_rows(words):
    lo = pltpu.bitcast(words << 16, F32)
    hi = pltpu.bitcast(words & jnp.uint32(0xFFFF0000), F32)
    return jnp.concatenate([lo, hi], axis=1).astype(BF16)


def _expert_weights(w_refs, w_f32, w_bf16, sems, layer):
    tn = w_f32.shape[2]
    n_col = pl.num_programs(1)
    _stationary_weights(
        w_refs, w_f32, w_bf16, sems, pl.program_id(0) * n_col + pl.program_id(1), pl.num_programs(0) * n_col,
        lambda ref, b: ref.at[layer, b // n_col, :, pl.ds(pl.multiple_of((b % n_col) * tn, tn), tn)], 2)


def _swiglu_kernel(a_ref, wg_ref, wu_ref, o_ref, w_f32, w_bf16, sems, *, layer):
    _expert_weights([wg_ref, wu_ref], w_f32, w_bf16, sems, layer)
    a = _unpack_rows(a_ref[...])
    gte = jnp.dot(a, w_bf16[0], preferred_element_type=F32)
    up = jnp.dot(a, w_bf16[1], preferred_element_type=F32)
    o_ref[...] = (gte * jax.nn.sigmoid(gte) * up).astype(o_ref.dtype)


def expert_swiglu(xe, w_gate, w_up, layer, *, tm=512, tn=512):
    e, c, _ = xe.shape
    k, f = w_gate.shape[-2:]
    tm, tn = _tile(c, tm), _tile(f, tn)
    w_spec = pl.BlockSpec(memory_space=pl.ANY)
    return pl.pallas_call(
        functools.partial(_swiglu_kernel, layer=layer),
        out_shape=jax.ShapeDtypeStruct((e, c, f), BF16),
        grid=(e, f // tn, c // tm),
        in_specs=[pl.BlockSpec((None, tm, k // 2), lambda x, j, i: (x, i, 0)), w_spec, w_spec],
        out_specs=pl.BlockSpec((None, tm, tn), lambda x, j, i: (x, i, j)),
        scratch_shapes=_weight_scratch(2, k, tn),
        compiler_params=_params("arbitrary", "arbitrary", "arbitrary"), name="expert_swiglu",
    )(xe, w_gate, w_up)


def _down_kernel(a_ref, w_ref, gate_ref, o_ref, w_f32, w_bf16, sems, *, layer):
    _expert_weights([w_ref], w_f32, w_bf16, sems, layer)
    acc = jnp.dot(a_ref[...], w_bf16[0], preferred_element_type=F32)
    o_ref[...] = acc * pltpu.bitcast(gate_ref[...], F32)[:, 0:1]


def expert_down(hid, w_down, xe, layer, *, tm=512, tn=2048):
    e, c, f = hid.shape
    d = w_down.shape[-1]
    tm, tn = _tile(c, tm), _tile(d, tn)
    gate_blk = (xe.shape[-1] - LANES) // LANES
    return pl.pallas_call(
        functools.partial(_down_kernel, layer=layer),
        out_shape=jax.ShapeDtypeStruct((e, c, d), F32),
        grid=(e, d // tn, c // tm),
        in_specs=[pl.BlockSpec((None, tm, f), lambda x, j, i: (x, i, 0)),
                  pl.BlockSpec(memory_space=pl.ANY),
                  pl.BlockSpec((None, tm, LANES), lambda x, j, i: (x, i, gate_blk))],
        out_specs=pl.BlockSpec((None, tm, tn), lambda x, j, i: (x, i, j)),
        scratch_shapes=_weight_scratch(1, f, tn),
        compiler_params=_params("arbitrary", "arbitrary", "arbitrary"), name="expert_down",
    )(hid, w_down, xe)


def _attn_kernel(q_ref, *rest, group, n_seg, chunk):
    kv_refs, o_ref = rest[:2 * n_seg], rest[2 * n_seg]
    tq = q_ref.shape[0]
    q = jnp.concatenate([q_ref[:, g * HEAD_DIM:(g + 1) * HEAD_DIM] for g in range(group)], axis=0)
    m = jnp.full((group * tq, 1), -jnp.inf, F32)
    acc = jnp.zeros((group * tq, 2 * HEAD_DIM), F32)
    for seg in range(n_seg):
        k_ref, v_ref = kv_refs[2 * seg], kv_refs[2 * seg + 1]
        length = k_ref.shape[0]
        step = math.gcd(length, chunk)
        for lo in range(0, length, step):
            k = k_ref[lo:lo + step, :].astype(BF16)
            v = v_ref[lo:lo + step, :].astype(BF16)
            v1 = jnp.concatenate([v, jnp.ones_like(v)], axis=1)
            s = lax.dot_general(q, k, (((1,), (1,)), ((), ())), preferred_element_type=F32)
            m_new = jnp.maximum(m, jnp.max(s, axis=-1, keepdims=True))
            p = jnp.exp2(s - m_new).astype(BF16)
            acc = jnp.exp2(m - m_new) * acc + jnp.dot(p, v1, preferred_element_type=F32)
            m = m_new
    o = acc[:, :HEAD_DIM] / acc[:, HEAD_DIM:HEAD_DIM + 1]
    for g in range(group):
        o_ref[:, g * HEAD_DIM:(g + 1) * HEAD_DIM] = o[g * tq:(g + 1) * tq].astype(o_ref.dtype)


def attention(q, segments, *, n_kv, n_req, q_row0, lq, tq, chunk=512):
    group = q.shape[-1] // HEAD_DIM // n_kv
    tq = _tile(lq, tq)
    qb0, nqb = q_row0 // tq, lq // tq
    qw = group * HEAD_DIM
    in_specs = [pl.BlockSpec((tq, qw), lambda r, h, t: (qb0 + r * nqb + t, h))]
    args = [q]
    for k_arr, v_arr, block, imap in segments:
        spec = pl.BlockSpec(block, lambda r, h, t, imap=imap: imap(r, h))
        in_specs += [spec, spec]
        args += [k_arr, v_arr]
    return pl.pallas_call(
        functools.partial(_attn_kernel, group=group, n_seg=len(segments), chunk=chunk),
        out_shape=jax.ShapeDtypeStruct((n_req * lq, q.shape[-1]), BF16),
        grid=(n_req, n_kv, nqb),
        in_specs=in_specs,
        out_specs=pl.BlockSpec((tq, qw), lambda r, h, t: (r * nqb + t, h)),
        compiler_params=_params("parallel", "parallel", "arbitrary"), name="attention",
    )(*args)


def _seq_dft_kernel(m_ref, u_ref, o_ref, f_scr):
    length = u_ref.shape[1]
    half = length // 2
    blk = math.gcd(half, MXU_COLS)
    n_blk = length // blk

    @pl.when(pl.program_id(2) == 0)
    def _():
        r = lax.broadcasted_iota(jnp.int32, (blk, blk), 0)
        c = lax.broadcasted_iota(jnp.int32, (blk, blk), 1)
        mirror = _ones_where((r >= 1) & (c == blk - r), BF16)
        first = _ones_where((r == 0) & (c == 0), BF16)
        row_id = lax.broadcasted_iota(jnp.int32, (blk, 1), 0)
        for b in range(half // blk):
            lo = b * blk
            for plane, sign in ((0, 1.0), (1, -1.0)):
                src = (n_blk - 1 - b) * blk
                mir = jnp.dot(mirror, u_ref[plane, src:src + blk, :], preferred_element_type=F32)
                if b >= 1:
                    mir += jnp.dot(first, u_ref[plane, src + blk:src + 2 * blk, :], preferred_element_type=F32)
                f = u_ref[plane, lo:lo + blk, :].astype(F32) + sign * mir
                if b == 0 and plane == 1:
                    f = jnp.where(row_id == 0, u_ref[0, half:half + 1, :].astype(F32), f)
                f_scr[plane, lo:lo + blk, :] = f.astype(BF16)

    acc = jnp.dot(m_ref[0], f_scr[0], preferred_element_type=F32)
    acc += jnp.dot(m_ref[1], f_scr[1], preferred_element_type=F32)
    o_ref[...] = acc.astype(o_ref.dtype)


def seq_dft(mats, u, *, n_req, row0, tm=512, tn=512):
    length, half = mats.shape[1:]
    d = u.shape[-1]
    tm, tn = _tile(length, tm), _tile(d, tn)
    rb0 = row0 // length
    return pl.pallas_call(
        _seq_dft_kernel,
        out_shape=jax.ShapeDtypeStruct((n_req * length, d), BF16),
        grid=(n_req, d // tn, length // tm),
        in_specs=[pl.BlockSpec((2, tm, half), lambda r, j, i: (0, i, 0)),
                  pl.BlockSpec((2, length, tn), lambda r, j, i: (0, rb0 + r, j))],
        out_specs=pl.BlockSpec((tm, tn), lambda r, j, i: (r * (length // tm) + i, j)),
        scratch_shapes=[pltpu.VMEM((2, half, tn), BF16)],
        compiler_params=_params("parallel", "parallel", "arbitrary"), name="seq_dft",
    )(mats, u)


def _seq_dft_mats(length):
    cos, sin = _dft_mats(length)
    half = length // 2
    return jnp.stack([cos[:, :half], (-sin[:, :half]).at[:, 0].set(cos[:, half])]).astype(BF16)


def _dft_mats(length):
    n2 = 1 << ((length.bit_length() - 1) // 2)
    n1 = length // n2
    assert n1 * n2 == length
    j = jnp.arange(length, dtype=jnp.int32)[:, None]
    ang_a = ((j * jnp.arange(n1, dtype=jnp.int32)[None, :]) % n1).astype(F32) * (2.0 * math.pi / n1)
    ang_b = ((j * jnp.arange(n2, dtype=jnp.int32)[None, :]) % length).astype(F32) * (2.0 * math.pi / length)
    ca, sa = jnp.cos(ang_a)[:, :, None], jnp.sin(ang_a)[:, :, None]
    cb, sb = jnp.cos(ang_b)[:, None, :], jnp.sin(ang_b)[:, None, :]
    norm = 1.0 / math.sqrt(length)
    cos = (ca * cb - sa * sb).reshape(length, length) * norm
    sin = (sa * cb + ca * sb).reshape(length, length) * norm
    return cos, sin


def _rope_tables(n_prompt, n_req, n_lat):
    half = HEAD_DIM // 2
    rows = n_lat // GRID_W
    row_pos = jnp.repeat(jnp.arange(rows, dtype=F32), GRID_W)
    col_pos = jnp.tile(jnp.arange(GRID_W, dtype=F32), rows)
    inv_freq = ROPE_THETA ** (-jnp.arange(0, half, 2, dtype=F32) / half)

    def cs(pos):
        ang = pos[:, None] * inv_freq[None, :]
        return jnp.cos(ang), jnp.sin(ang)

    cr, sr = cs(row_pos)
    cc, sc = cs(col_pos)
    cos = jnp.concatenate([cr, cr, cc, cc], axis=-1)
    sin = jnp.concatenate([-sr, sr, -sc, sc], axis=-1)
    cos = jnp.concatenate([jnp.ones((n_prompt, HEAD_DIM), F32), jnp.tile(cos, (n_req, 1))], axis=0)
    sin = jnp.concatenate([jnp.zeros((n_prompt, HEAD_DIM), F32), jnp.tile(sin, (n_req, 1))], axis=0)
    return cos, sin


def kernel(x_prompt, x_sample, cache_k, cache_v, c, c_ctx, norm_g, w_mod, b_mod, w_q, w_k, w_v, w_o,
           q_norm_g, k_norm_g, w_fourier, w_router, w_gate, w_up, w_down, final_norm_g):
    batch, seq, d = x_prompt.shape
    n_req, n_lat, _ = x_sample.shape
    depth = w_mod.shape[0]
    n_kv = cache_k.shape[3]
    n_exp = w_router.shape[-1]
    n_prompt = batch * seq
    nt = n_prompt + n_req * n_lat
    assert n_req + 1 <= COND_ROWS
    assert n_prompt % n_lat == 0

    def cond_row(r):
        return jnp.where(r < n_prompt, 0, 1 + (r - n_prompt) // n_lat)

    cond = jnp.zeros((COND_ROWS, d), F32).at[0].set(c_ctx).at[1:1 + n_req].set(c)
    mod = modulation_all(cond, w_mod, b_mod)
    cos, sin = _rope_tables(n_prompt, n_req, n_lat)
    scale = math.log2(math.e) / math.sqrt(HEAD_DIM)

    cc, sc = _dft_mats(FGROUP_DIM)
    cs_chan = jnp.concatenate([cc, sc], axis=1).astype(BF16)
    mats_p = mats_s = None
    if depth > 1:
        mats_p, mats_s = _seq_dft_mats(seq), _seq_dft_mats(n_lat)

    x = (x_prompt.reshape(n_prompt, d), x_sample.reshape(n_req * n_lat, d))
    assert n_prompt % LANES == 0 and (n_req * n_lat) % LANES == 0
    cap_p = (CAPACITY_FACTOR * n_prompt) // n_exp
    cap_s = (CAPACITY_FACTOR * n_req * n_lat) // n_exp
    groups = ((0, n_prompt // LANES, cap_p, 0), (n_prompt // LANES, n_req * n_lat // LANES, cap_s, cap_p))
    slots = cap_p + cap_s
    assert cap_p % SUBLANES == 0 and cap_s % SUBLANES == 0
    new_k, new_v = [], []
    mixer_in = None
    for i in range(depth):
        mod_l = mod[i].reshape(COND_ROWS * N_MOD, 1, d)
        j = i // 2
        if i % 2 == 0:
            h = mixer_in if i > 0 else ada_norm(x, norm_g[i, 0], mod_l, 0, 1, cond_row)
            q = project(h, w_q, j, out_dtypes=(BF16,),
                        head=(q_norm_g[j].reshape(1, HEAD_DIM) * scale, cos, sin))
            kf, kb = project(h, w_k, j, out_dtypes=(F32, BF16),
                             head=(k_norm_g[j].reshape(1, HEAD_DIM), cos, sin))
            vf, vb = project(h, w_v, j, out_dtypes=(F32, BF16))
            kvw = n_kv * HEAD_DIM
            new_k.append(kf[:n_prompt].reshape(batch, seq, n_kv, HEAD_DIM))
            new_v.append(vf[:n_prompt].reshape(batch, seq, n_kv, HEAD_DIM))
            att_p = attention(q, [(kb, vb, (seq, HEAD_DIM), lambda r, h: (r, h))],
                              n_kv=n_kv, n_req=batch, q_row0=0, lq=seq, tq=256)
            past = cache_k.shape[2]
            lat0 = n_prompt // n_lat
            att_s = attention(
                q,
                [(cache_k.reshape(n_req, -1, past, kvw), cache_v.reshape(n_req, -1, past, kvw),
                  (None, None, past, HEAD_DIM), lambda r, h, j=j: (r, j, 0, h)),
                 (kb, vb, (n_lat, HEAD_DIM), lambda r, h: (lat0 + r, h))],
                n_kv=n_kv, n_req=n_req, q_row0=n_prompt, lq=n_lat, tq=256)
            x = project((att_p, att_s), w_o, j, out_dtypes=(F32,), residual=(x, mod_l, 2, cond_row))
        else:
            u = mixer_in if i > 0 else ada_norm(x, norm_g[i, 0], mod_l, 0, 1, cond_row, mode="dft", cs=cs_chan)
            f_p = seq_dft(mats_p, u, n_req=batch, row0=0, tn=d)
            f_s = seq_dft(mats_s, u, n_req=n_req, row0=n_prompt)
            x = project((f_p, f_s), w_fourier, j, out_dtypes=(F32,), residual=(x, mod_l, 2, cond_row))

        h, aff = ada_norm(x, norm_g[i, 1], mod_l, 3, 4, cond_row, mode="router",
                          w_router_t=w_router[i].T)
        pick, tot, offs = route_select(aff, groups)
        xe = dispatch(h, pick, aff, tot, offs, slots)
        hid = expert_swiglu(xe, w_gate, w_up, i)
        ye = expert_down(hid, w_down, xe, i)
        if i + 1 < depth:
            follow = "plain" if (i + 1) % 2 == 0 else "dft"
            x, mixer_in = combine(x, ye, pick, tot, offs, mod_l, 5, cond_row, follow=follow, g=norm_g[i + 1, 0],
                                  mod_next=mod[i + 1].reshape(COND_ROWS * N_MOD, 1, d),
                                  cs=cs_chan if follow == "dft" else None)
        else:
            y_p, y_s = combine(x, ye, pick, tot, offs, mod_l, 5, cond_row, follow="final", g=final_norm_g,
                               first_rows=n_prompt)

    return (y_p.reshape(batch, seq, d), y_s.reshape(n_req, n_lat, d),
            jnp.stack(new_k, axis=1), jnp.stack(new_v, axis=1))
```

```python
import functools
import math

import jax
import jax.numpy as jnp
from jax import lax
from jax.experimental import pallas as pl
from jax.experimental.pallas import tpu as pltpu

F32 = jnp.float32
BF16 = jnp.bfloat16

GRID_W = 64
HEAD_DIM = 128
FGROUP_DIM = 128
ROPE_THETA = 10000.0
CAPACITY_FACTOR = 2
N_MOD = 6
EPS = 1e-6

LANES = 128
SUBLANES = 8
MXU_COLS = 256
COND_ROWS = 8
VMEM_LIMIT_BYTES = 56 * 1024 * 1024


def _params(*sem):
    return pltpu.CompilerParams(dimension_semantics=sem, vmem_limit_bytes=VMEM_LIMIT_BYTES)


def _tile(n, want):
    return math.gcd(n, want)


def _mod_kernel(c_ref, w_ref, b_ref, o_ref):
    c = c_ref[...]
    a = (c * jax.nn.sigmoid(c)).astype(BF16)
    o_ref[...] = jnp.dot(a, w_ref[...].astype(BF16), preferred_element_type=F32) + b_ref[...]


def modulation_all(cond, w_mod, b_mod):
    depth, d, n = w_mod.shape
    tn = _tile(n, 512)
    return pl.pallas_call(
        _mod_kernel,
        out_shape=jax.ShapeDtypeStruct((depth, COND_ROWS, n), F32),
        grid=(depth, n // tn),
        in_specs=[
            pl.BlockSpec((COND_ROWS, d), lambda l, j: (0, 0)),
            pl.BlockSpec((None, d, tn), lambda l, j: (l, 0, j)),
            pl.BlockSpec((None, 1, tn), lambda l, j: (l, 0, j)),
        ],
        out_specs=pl.BlockSpec((None, COND_ROWS, tn), lambda l, j: (l, 0, j)),
        compiler_params=_params("parallel", "parallel"),
        name="modulation",
    )(cond, w_mod, b_mod.reshape(depth, 1, n))


def _rms(x, g):
    return x * lax.rsqrt(jnp.mean(x * x, axis=-1, keepdims=True) + EPS) * g


def _split_rows(specs_of, first, rest_rows, tm):
    split = first.shape[0] // tm
    assert split * tm == first.shape[0] and rest_rows.shape[0] % tm == 0
    return (specs_of(lambda i: jnp.minimum(i, split - 1)), specs_of(lambda i: jnp.maximum(i - split, 0))), split


def _ada_kernel(*refs, mode, split):
    if split is None:
        x_ref, g_ref, sh_ref, sc_ref, *rest = refs
        _ada_apply(x_ref[...], g_ref, sh_ref, sc_ref, rest, mode)
        return
    x0_ref, x1_ref, g_ref, sh_ref, sc_ref, *rest = refs

    @pl.when(pl.program_id(0) < split)
    def _():
        _ada_apply(x0_ref[...], g_ref, sh_ref, sc_ref, rest, mode)

    @pl.when(pl.program_id(0) >= split)
    def _():
        _ada_apply(x1_ref[...], g_ref, sh_ref, sc_ref, rest, mode)


def _ada_apply(x, g_ref, sh_ref, sc_ref, rest, mode):
    h = _rms(x, g_ref[...]) * (1.0 + sc_ref[...]) + sh_ref[...]
    hb = h.astype(BF16)
    if mode == "plain":
        (h_ref,) = rest
        h_ref[...] = hb
    elif mode == "router":
        wr_ref, h_ref, aff_ref = rest
        h_ref[...] = hb
        logits = lax.dot_general(wr_ref[...].astype(BF16), hb, (((1,), (1,)), ((), ())),
                                 preferred_element_type=F32)
        m = jnp.max(logits, axis=0, keepdims=True)
        p = jnp.exp(logits - m)
        aff = p / jnp.sum(p, axis=0, keepdims=True)
        for blk in range(aff_ref.shape[0]):
            aff_ref[blk] = aff[:, blk * LANES:(blk + 1) * LANES]
    else:
        cs_ref, u_ref = rest
        n_groups = hb.shape[1] // FGROUP_DIM
        for grp in range(n_groups):
            lo = grp * FGROUP_DIM
            r = jnp.dot(hb[:, lo:lo + FGROUP_DIM], cs_ref[...], preferred_element_type=F32)
            u_ref[0, :, lo:lo + FGROUP_DIM] = r[:, :FGROUP_DIM].astype(BF16)
            u_ref[1, :, lo:lo + FGROUP_DIM] = r[:, FGROUP_DIM:].astype(BF16)


def ada_norm(x, g, mod_l, shift_idx, scale_idx, cond_row, *, mode="plain", w_router_t=None, cs=None):
    row = lambda i: (i, 0)
    if isinstance(x, tuple):
        d = x[0].shape[1]
        nt = x[0].shape[0] + x[1].shape[0]
        tm = _tile(math.gcd(x[0].shape[0], x[1].shape[0]), 256)
        x_specs, split = _split_rows(lambda blk: pl.BlockSpec((tm, d), lambda i: (blk(i), 0)), x[0], x[1], tm)
        x_specs, x_args = list(x_specs), list(x)
    else:
        nt, d = x.shape
        tm = _tile(nt, 256)
        x_specs, x_args, split = [pl.BlockSpec((tm, d), row)], [x], None
    grid = (nt // tm,)
    in_specs = x_specs + [
        pl.BlockSpec((1, d), lambda i: (0, 0)),
        pl.BlockSpec((None, 1, d), lambda i: (cond_row(i * tm) * N_MOD + shift_idx, 0, 0)),
        pl.BlockSpec((None, 1, d), lambda i: (cond_row(i * tm) * N_MOD + scale_idx, 0, 0)),
    ]
    args = x_args + [g.reshape(1, d), mod_l, mod_l]
    if mode == "plain":
        out_shape = jax.ShapeDtypeStruct((nt, d), BF16)
        out_specs = pl.BlockSpec((tm, d), row)
    elif mode == "router":
        e = w_router_t.shape[0]
        in_specs.append(pl.BlockSpec((e, d), lambda i: (0, 0)))
        args.append(w_router_t)
        out_shape = (jax.ShapeDtypeStruct((nt, d), BF16), jax.ShapeDtypeStruct((nt // LANES, e, LANES), F32))
        out_specs = (pl.BlockSpec((tm, d), row), pl.BlockSpec((tm // LANES, e, LANES), lambda i: (i, 0, 0)))
    else:
        in_specs.append(pl.BlockSpec(cs.shape, lambda i: (0, 0)))
        args.append(cs)
        out_shape = jax.ShapeDtypeStruct((2, nt, d), BF16)
        out_specs = pl.BlockSpec((2, tm, d), lambda i: (0, i, 0))
    return pl.pallas_call(
        functools.partial(_ada_kernel, mode=mode, split=split),
        out_shape=out_shape, grid=grid, in_specs=in_specs, out_specs=out_specs,
        compiler_params=_params("parallel"), name="ada_norm_" + mode,
    )(*args)


def _stationary_weights(w_hbm, w_f32, w_bf16, sems, block, n_blocks, slicer, inner_axis):
    def copy(b, which):
        return pltpu.make_async_copy(slicer(w_hbm[which], b), w_f32.at[which], sems.at[which])

    @pl.when(pl.program_id(inner_axis) == 0)
    def _():
        @pl.when(block == 0)
        def _():
            for which in range(len(w_hbm)):
                copy(block, which).start()

        for which in range(len(w_hbm)):
            copy(block, which).wait()
            w_bf16[which] = w_f32[which].astype(BF16)

        @pl.when(block + 1 < n_blocks)
        def _():
            for which in range(len(w_hbm)):
                copy(block + 1, which).start()


def _weight_scratch(n_w, k, tn):
    return [pltpu.VMEM((n_w, k, tn), F32), pltpu.VMEM((n_w, k, tn), BF16), pltpu.SemaphoreType.DMA((n_w,))]


def _project_weights(w_ref, w_f32, w_bf16, sems, layer):
    tn = w_f32.shape[2]
    _stationary_weights(
        [w_ref], w_f32, w_bf16, sems, pl.program_id(0), pl.num_programs(0),
        lambda ref, b: ref.at[layer, :, pl.ds(pl.multiple_of(b * tn, tn), tn)], 1)


def _mm_plain_kernel(a_ref, w_ref, *rest, n_out, layer):
    outs, (w_f32, w_bf16, sems) = rest[:n_out], rest[n_out:]
    _project_weights(w_ref, w_f32, w_bf16, sems, layer)
    acc = jnp.dot(a_ref[...], w_bf16[0], preferred_element_type=F32)
    for o_ref in outs:
        o_ref[...] = acc.astype(o_ref.dtype)


def _swap32(y):
    lane = lax.broadcasted_iota(jnp.int32, y.shape, 1)
    first = (lane % 64) < 32
    return jnp.where(first, pltpu.roll(y, 96, 1), pltpu.roll(y, 32, 1))


def _mm_headnorm_kernel(a_ref, w_ref, g_ref, cos_ref, sin_ref, *rest, n_out, layer):
    outs, (w_f32, w_bf16, sems) = rest[:n_out], rest[n_out:]
    _project_weights(w_ref, w_f32, w_bf16, sems, layer)
    a = a_ref[...]
    cos, sin = cos_ref[...], sin_ref[...]
    tn = w_bf16.shape[2]
    step = math.gcd(tn, MXU_COLS)
    for c0 in range(0, tn, step):
        acc = jnp.dot(a, w_bf16[0, :, c0:c0 + step], preferred_element_type=F32)
        for lo in range(0, step, HEAD_DIM):
            y = _rms(acc[:, lo:lo + HEAD_DIM], g_ref[...])
            y = y * cos + _swap32(y) * sin
            for o_ref in outs:
                o_ref[:, c0 + lo:c0 + lo + HEAD_DIM] = y.astype(o_ref.dtype)


def _mm_residual_kernel(a0_ref, a1_ref, w_ref, *rest, layer, split, res_split):
    if res_split:
        res0_ref, res1_ref, gate_ref, o_ref, w_f32, w_bf16, sems = rest
    else:
        res0_ref, gate_ref, o_ref, w_f32, w_bf16, sems = rest
        res1_ref = res0_ref
    _project_weights(w_ref, w_f32, w_bf16, sems, layer)

    def emit(a_ref, res_ref):
        acc = jnp.dot(a_ref[...], w_bf16[0], preferred_element_type=F32)
        o_ref[...] = res_ref[...] + gate_ref[...] * acc

    @pl.when(pl.program_id(1) < split)
    def _():
        emit(a0_ref, res0_ref)

    @pl.when(pl.program_id(1) >= split)
    def _():
        emit(a1_ref, res1_ref)


def project(a, w_stack, layer, *, out_dtypes, head=None, residual=None, tm=512, tn=1024):
    n = w_stack.shape[-1]
    if residual is not None:
        a0, a1 = a
        k = a0.shape[1]
        m = a0.shape[0] + a1.shape[0]
        tm, tn = _tile(math.gcd(a0.shape[0], a1.shape[0]), tm), _tile(n, tn)
        split = a0.shape[0] // tm
        in_specs = [pl.BlockSpec((tm, k), lambda j, i: (jnp.minimum(i, split - 1), 0)),
                    pl.BlockSpec((tm, k), lambda j, i: (jnp.maximum(i - split, 0), 0)),
                    pl.BlockSpec(memory_space=pl.ANY)]
        args = [a0, a1, w_stack]
    else:
        m, k = a.shape
        tm, tn = _tile(m, tm), _tile(n, tn)
        in_specs = [pl.BlockSpec((tm, k), lambda j, i: (i, 0)), pl.BlockSpec(memory_space=pl.ANY)]
        args = [a, w_stack]
    grid = (n // tn, m // tm)
    out_block = pl.BlockSpec((tm, tn), lambda j, i: (i, j))
    if head is not None:
        g, cos, sin = head
        in_specs += [pl.BlockSpec((1, HEAD_DIM), lambda j, i: (0, 0)),
                     pl.BlockSpec((tm, HEAD_DIM), lambda j, i: (i, 0)),
                     pl.BlockSpec((tm, HEAD_DIM), lambda j, i: (i, 0))]
        args += [g, cos, sin]
        body = functools.partial(_mm_headnorm_kernel, n_out=len(out_dtypes), layer=layer)
    elif residual is not None:
        res, mod_l, gate_idx, cond_row = residual
        if isinstance(res, tuple):
            assert res[0].shape[0] == a0.shape[0]
            in_specs += [pl.BlockSpec((tm, tn), lambda j, i: (jnp.minimum(i, split - 1), j)),
                         pl.BlockSpec((tm, tn), lambda j, i: (jnp.maximum(i - split, 0), j))]
            args += list(res)
        else:
            in_specs.append(pl.BlockSpec((tm, tn), lambda j, i: (i, j)))
            args.append(res)
        in_specs.append(pl.BlockSpec((None, 1, tn), lambda j, i: (cond_row(i * tm) * N_MOD + gate_idx, 0, j)))
        args.append(mod_l)
        body = functools.partial(_mm_residual_kernel, layer=layer, split=split, res_split=isinstance(res, tuple))
    else:
        body = functools.partial(_mm_plain_kernel, n_out=len(out_dtypes), layer=layer)
    outs = pl.pallas_call(
        body,
        out_shape=tuple(jax.ShapeDtypeStruct((m, n), dt) for dt in out_dtypes),
        grid=grid, in_specs=in_specs, out_specs=tuple(out_block for _ in out_dtypes),
        scratch_shapes=_weight_scratch(1, k, tn),
        compiler_params=_params("arbitrary", "arbitrary"), name="project",
    )(*args)
    return outs if len(outs) > 1 else outs[0]


PICK_ROWS = 512


def _ones_where(mask, dtype):
    return jnp.where(mask, 1.0, 0.0).astype(dtype)


def _select_kernel(aff_ref, pick_ref, tot_ref, offs_ref, w_scr, o_scr, *, groups):
    n_exp = aff_ref.shape[1]
    li = lax.broadcasted_iota(jnp.int32, (LANES, LANES), 0)
    lj = lax.broadcasted_iota(jnp.int32, (LANES, LANES), 1)
    tri = _ones_where(li <= lj, BF16)

    def chunk_cumsum(mask, nb):
        w = jnp.dot(_ones_where(mask, BF16).reshape(nb * n_exp, LANES), tri,
                    preferred_element_type=F32).reshape(nb, n_exp, LANES)
        w_scr[0:nb] = w

        def body(j, run):
            o_scr[j] = jnp.broadcast_to(run, (n_exp, LANES))
            return run + w_scr[j][:, LANES - 1:LANES]

        lax.fori_loop(0, nb, body, jnp.zeros((n_exp, 1), F32))
        return w, o_scr[0:nb]

    for j0, nb, cap, slot0 in groups:
        keys = pltpu.bitcast(aff_ref[j0:j0 + nb], jnp.int32)

        def count(mask):
            return jnp.sum(jnp.sum(_ones_where(mask, F32), axis=0), axis=1, keepdims=True)

        def search(it, prefix):
            cand = prefix | lax.shift_left(jnp.int32(1), 30 - it)
            return jnp.where(count(keys >= cand[None]) >= cap, cand, prefix)

        thr = lax.fori_loop(0, 31, search, jnp.zeros((n_exp, 1), jnp.int32))[None]
        above, ties = keys > thr, keys == thr
        need = (cap - count(above))[None]
        w_eq, o_eq = chunk_cumsum(ties, nb)
        sel = above | (ties & (w_eq + o_eq <= need))
        w, offs = chunk_cumsum(sel, nb)
        tot = jnp.broadcast_to(w[:, :, LANES - 1:LANES], w.shape)
        lead = offs - SUBLANES * jnp.floor(offs / SUBLANES)
        seg = SUBLANES * jnp.floor((lead + tot + (SUBLANES - 1)) / SUBLANES)
        rows = nb * n_exp
        blk = math.gcd(rows, 256)
        bi = lax.broadcasted_iota(jnp.int32, (blk, blk), 0)
        bj = lax.broadcasted_iota(jnp.int32, (blk, blk), 1)
        lower = _ones_where((bi // n_exp == bj // n_exp) & (bj % n_exp < bi % n_exp), BF16)
        seg2 = seg.reshape(rows, LANES).astype(BF16)
        base = jnp.concatenate(
            [jnp.dot(lower, seg2[r0:r0 + blk], preferred_element_type=F32) for r0 in range(0, rows, blk)],
            axis=0).reshape(nb, n_exp, LANES)
        pick_ref[j0:j0 + nb] = jnp.where(sel, base + lead + w - 1.0, -1.0).astype(jnp.int32)
        tot_ref[j0:j0 + nb] = tot.astype(jnp.int32)
        offs_ref[j0:j0 + nb] = offs.astype(jnp.int32) + slot0


def route_select(aff, groups):
    nbt, e, _ = aff.shape
    shape = jax.ShapeDtypeStruct(aff.shape, jnp.int32)
    max_nb = max(g[1] for g in groups)
    pick, tot, offs = pl.pallas_call(
        functools.partial(_select_kernel, groups=groups),
        out_shape=(shape, shape, shape),
        scratch_shapes=[pltpu.VMEM((max_nb, e, LANES), F32), pltpu.VMEM((max_nb, e, LANES), F32)],
        compiler_params=pltpu.CompilerParams(vmem_limit_bytes=VMEM_LIMIT_BYTES), name="route_select",
    )(aff)
    return pick, tot[:, :, 0].reshape(nbt * e), offs[:, :, 0].reshape(nbt * e)


def _sub_blocks():
    first = (3 * PICK_ROWS // 4) // 16 * 16
    return ((0, first), (first, PICK_ROWS - first))


def _fill_owner_rows(tot_ref, offs_ref, j, n_exp, row0, pairs):
    def per_expert(e, base):
        _, _, seg = _segment(tot_ref, offs_ref, j, n_exp, e)
        lo = jnp.maximum(base, row0)
        hi = jnp.minimum(base + seg, row0 + PICK_ROWS)
        vals = [jnp.broadcast_to(src[pl.ds(e, 1), :], (SUBLANES, LANES)) for src, _ in pairs]

        def per_group(g, carry):
            row = pl.multiple_of(lo - row0 + g * SUBLANES, SUBLANES)
            for val, (_, dst) in zip(vals, pairs):
                dst[pl.ds(row, SUBLANES), :] = val
            return carry

        lax.fori_loop(0, jnp.maximum(hi - lo, 0) // SUBLANES, per_group, 0)
        return base + seg

    lax.fori_loop(0, n_exp, per_expert, jnp.int32(0))


def _segment(tot_ref, offs_ref, j, n_exp, e):
    t = tot_ref[j * n_exp + e]
    o = offs_ref[j * n_exp + e]
    r = o & (SUBLANES - 1)
    return o, r, ((r + t + SUBLANES - 1) // SUBLANES) * SUBLANES


def _for_each_piece(tot_ref, offs_ref, j, n_exp, row0, rows, fn):
    max_groups = (SUBLANES - 1 + LANES + SUBLANES - 1) // SUBLANES

    def per_expert(e, base):
        o, r, seg = _segment(tot_ref, offs_ref, j, n_exp, e)
        lo = jnp.maximum(base, row0)
        hi = jnp.minimum(base + seg, row0 + rows)
        n = jnp.maximum(hi - lo, 0)
        slot = (o - r) + (lo - base)
        done = jnp.int32(0)
        for bit in range(max_groups.bit_length() - 1, -1, -1):
            size = SUBLANES << bit

            @pl.when((n & size) != 0)
            def _():
                fn(e, pl.multiple_of(slot + done, SUBLANES), pl.multiple_of(lo - row0 + done, SUBLANES), size)

            done = done + (n & size)
        return base + seg

    lax.fori_loop(0, n_exp, per_expert, jnp.int32(0))


def _staging_rows(tot_ref, offs_ref, j, n_exp):
    return lax.fori_loop(0, n_exp, lambda e, s: s + _segment(tot_ref, offs_ref, j, n_exp, e)[2], jnp.int32(0))


def _rounds(tot_ref, offs_ref, j, n_exp):
    return (_staging_rows(tot_ref, offs_ref, j, n_exp) + PICK_ROWS - 1) // PICK_ROWS


def _dispatch_kernel(tot_ref, offs_ref, h_ref, pick_ref, aff_ref, xe_ref, stage2, tail, owner_pick, owner_aff, sems):
    j = pl.program_id(0)
    buf = j % 2
    stage = stage2.at[buf]
    n_exp = pick_ref.shape[0]
    half = h_ref.shape[1] // 2
    total = _staging_rows(tot_ref, offs_ref, j, n_exp)
    n_rounds = (total + PICK_ROWS - 1) // PICK_ROWS

    @pl.when(j == 0)
    def _():
        tail[...] = jnp.zeros_like(tail)
        owner_aff[...] = jnp.zeros_like(owner_aff)

    def copy(b, e, slot, row, size):
        return pltpu.make_async_copy(stage2.at[b, pl.ds(row, size)], xe_ref.at[e, pl.ds(slot, size)], sems.at[b])

    def start_round(jj, r, b):
        _for_each_piece(tot_ref, offs_ref, jj, n_exp, r * PICK_ROWS, PICK_ROWS,
                        lambda e, slot, row, size: copy(b, e, slot, row, size).start())

    def wait_round(jj, r, b):
        _for_each_piece(tot_ref, offs_ref, jj, n_exp, r * PICK_ROWS, PICK_ROWS,
                        lambda e, slot, row, size: copy(b, e, slot, row, size).wait())

    def splice_tails(row0):
        def per_expert(e, base):
            _, r, seg = _segment(tot_ref, offs_ref, j, n_exp, e)
            last = base + seg - SUBLANES

            @pl.when((r > 0) & (base >= row0) & (base < row0 + PICK_ROWS))
            def _():
                row = pl.multiple_of(base - row0, SUBLANES)
                keep = lax.broadcasted_iota(jnp.int32, (SUBLANES, 1), 0) < r
                stage[pl.ds(row, SUBLANES), :] = jnp.where(keep, tail[e], stage[pl.ds(row, SUBLANES), :])

            @pl.when((seg > 0) & (last >= row0) & (last < row0 + PICK_ROWS))
            def _():
                tail[e] = stage[pl.ds(pl.multiple_of(last - row0, SUBLANES), SUBLANES), :]

            return base + seg

        lax.fori_loop(0, n_exp, per_expert, jnp.int32(0))

    def build(r):
        row0 = r * PICK_ROWS
        owner_pick[...] = jnp.full(owner_pick.shape, -1, jnp.int32)
        _fill_owner_rows(tot_ref, offs_ref, j, n_exp, row0, ((pick_ref, owner_pick), (aff_ref, owner_aff)))
        for b0, n in _sub_blocks():
            @pl.when(total - row0 > b0)
            def _():
                ids = lax.broadcasted_iota(jnp.int32, (n, LANES), 0) + (row0 + b0)
                hit = ids == owner_pick[b0:b0 + n, :]
                rows = jnp.dot(_ones_where(hit, BF16), h_ref[...], preferred_element_type=F32)
                bits = pltpu.bitcast(rows, jnp.uint32)
                stage[b0:b0 + n, 0:half] = bits[:, half:] | (bits[:, :half] >> 16)
                g = jnp.sum(jnp.where(hit, owner_aff[b0:b0 + n, :], 0.0), axis=1, keepdims=True)
                stage[b0:b0 + n, half:half + LANES] = pltpu.bitcast(jnp.broadcast_to(g, (n, LANES)), jnp.uint32)
        splice_tails(row0)

    @pl.when(n_rounds > 0)
    def _():
        build(0)

    @pl.when(j > 0)
    def _():
        prev_rounds = _rounds(tot_ref, offs_ref, j - 1, n_exp)

        @pl.when(prev_rounds > 0)
        def _():
            wait_round(j - 1, prev_rounds - 1, 1 - buf)

    @pl.when(n_rounds > 0)
    def _():
        start_round(j, 0, buf)

    def later_round(r, carry):
        wait_round(j, r - 1, buf)
        build(r)
        start_round(j, r, buf)
        return carry

    lax.fori_loop(1, n_rounds, later_round, 0)

    @pl.when((j == pl.num_programs(0) - 1) & (n_rounds > 0))
    def _():
        wait_round(j, n_rounds - 1, buf)


def dispatch(h, pick, aff, tot, offs, slots):
    nt, d = h.shape
    nbt, e, _ = pick.shape
    width = d // 2 + LANES
    chunk = pl.BlockSpec((None, e, LANES), lambda j, *_: (j, 0, 0))
    return pl.pallas_call(
        _dispatch_kernel,
        out_shape=jax.ShapeDtypeStruct((e, slots, width), jnp.uint32),
        grid_spec=pltpu.PrefetchScalarGridSpec(
            num_scalar_prefetch=2, grid=(nbt,),
            in_specs=[pl.BlockSpec((LANES, d), lambda j, *_: (j, 0)), chunk, chunk],
            out_specs=pl.BlockSpec(memory_space=pl.ANY),
            scratch_shapes=[pltpu.VMEM((2, PICK_ROWS, width), jnp.uint32),
                            pltpu.VMEM((e, SUBLANES, width), jnp.uint32),
                            pltpu.VMEM((PICK_ROWS, LANES), jnp.int32), pltpu.VMEM((PICK_ROWS, LANES), F32),
                            pltpu.SemaphoreType.DMA((2,))]),
        compiler_params=_params("arbitrary"), name="dispatch",
    )(tot, offs, h, pick, aff)


def _combine_kernel(tot_ref, offs_ref, x_ref, pick_ref, gate_ref, ye_ref, *rest, slots, follow, split):
    n_in = {None: 0, "final": 1, "plain": 3, "dft": 4}[follow]
    follow_in, (o_ref, *follow_out), (stage2, acc_ref, owner_pick, sems) = rest[:n_in], rest[n_in:-4], rest[-4:]
    j = pl.program_id(0)
    buf = j % 2
    n_exp = pick_ref.shape[0]
    total = _staging_rows(tot_ref, offs_ref, j, n_exp)
    n_rounds = (total + PICK_ROWS - 1) // PICK_ROWS

    def copy(b, e, slot, row, size):
        return pltpu.make_async_copy(ye_ref.at[pl.ds(e * slots + slot, size)],
                                     stage2.at[b, pl.ds(row, size)], sems.at[b])

    def start_round(jj, r, b):
        _for_each_piece(tot_ref, offs_ref, jj, n_exp, r * PICK_ROWS, PICK_ROWS,
                        lambda e, slot, row, size: copy(b, e, slot, row, size).start())

    def wait_round(jj, r, b):
        _for_each_piece(tot_ref, offs_ref, jj, n_exp, r * PICK_ROWS, PICK_ROWS,
                        lambda e, slot, row, size: copy(b, e, slot, row, size).wait())

    @pl.when((j == 0) & (n_rounds > 0))
    def _():
        start_round(0, 0, 0)

    @pl.when(j + 1 < pl.num_programs(0))
    def _():
        @pl.when(_rounds(tot_ref, offs_ref, j + 1, n_exp) > 0)
        def _():
            start_round(j + 1, 0, 1 - buf)

    acc_ref[...] = jnp.zeros_like(acc_ref)

    def one_round(r, carry):
        row0 = r * PICK_ROWS

        @pl.when(r > 0)
        def _():
            start_round(j, r, buf)

        wait_round(j, r, buf)
        owner_pick[...] = jnp.full(owner_pick.shape, -1, jnp.int32)
        _fill_owner_rows(tot_ref, offs_ref, j, n_exp, row0, ((pick_ref, owner_pick),))
        for b0, n in _sub_blocks():
            @pl.when(total - row0 > b0)
            def _():
                ids = lax.broadcasted_iota(jnp.int32, (n, LANES), 0) + (row0 + b0)
                onehot = _ones_where(ids == owner_pick[b0:b0 + n, :], F32).T.astype(BF16)
                live = ids[:, 0:1] < total
                y = jnp.where(live, stage2[buf, b0:b0 + n, :], 0.0)
                acc = acc_ref[...]
                for _ in range(2):
                    part = y.astype(BF16)
                    acc = acc + jnp.dot(onehot, part, preferred_element_type=F32)
                    y = y - part.astype(F32)
                acc_ref[...] = acc
        return carry

    lax.fori_loop(0, n_rounds, one_round, 0)
    x_new = x_ref[...] + gate_ref[...] * acc_ref[...]
    if follow == "final":
        y_out = _rms(x_new, follow_in[0][...])

        @pl.when(j < split)
        def _():
            o_ref[...] = y_out

        @pl.when(j >= split)
        def _():
            follow_out[0][...] = y_out
    else:
        o_ref[...] = x_new
        if follow is not None:
            g_ref, sh_ref, sc_ref = follow_in[:3]
            _ada_apply(x_new, g_ref, sh_ref, sc_ref, tuple(follow_in[3:]) + tuple(follow_out), follow)


def combine(x, ye, pick, tot, offs, mod_l, gate_idx, cond_row, *, follow=None, g=None, mod_next=None, cs=None,
            first_rows=None):
    nt, d = x.shape
    nbt, e, _ = pick.shape
    slots = ye.shape[1]
    rows = pl.BlockSpec((LANES, d), lambda j, *_: (j, 0))
    in_specs = [rows,
                pl.BlockSpec((None, e, LANES), lambda j, *_: (j, 0, 0)),
                pl.BlockSpec((None, 1, d), lambda j, *_: (cond_row(j * LANES) * N_MOD + gate_idx, 0, 0)),
                pl.BlockSpec(memory_space=pl.ANY)]
    args = [tot, offs, x, pick, mod_l, ye.reshape(e * slots, d)]
    out_shape, out_specs, split = [jax.ShapeDtypeStruct((nt, d), F32)], [rows], None
    if follow is not None:
        in_specs.append(pl.BlockSpec((1, d), lambda j, *_: (0, 0)))
        args.append(g.reshape(1, d))
    if follow == "final":
        split = first_rows // LANES
        assert split * LANES == first_rows
        out_shape = [jax.ShapeDtypeStruct((first_rows, d), F32), jax.ShapeDtypeStruct((nt - first_rows, d), F32)]
        out_specs = [pl.BlockSpec((LANES, d), lambda j, *_: (jnp.minimum(j, split - 1), 0)),
                     pl.BlockSpec((LANES, d), lambda j, *_: (jnp.maximum(j - split, 0), 0))]
    if follow in ("plain", "dft"):
        in_specs += [pl.BlockSpec((None, 1, d), lambda j, *_: (cond_row(j * LANES) * N_MOD + 0, 0, 0)),
                     pl.BlockSpec((None, 1, d), lambda j, *_: (cond_row(j * LANES) * N_MOD + 1, 0, 0))]
        args += [mod_next, mod_next]
    if follow == "plain":
        out_shape.append(jax.ShapeDtypeStruct((nt, d), BF16))
        out_specs.append(rows)
    if follow == "dft":
        in_specs.append(pl.BlockSpec(cs.shape, lambda j, *_: (0, 0)))
        args.append(cs)
        out_shape.append(jax.ShapeDtypeStruct((2, nt, d), BF16))
        out_specs.append(pl.BlockSpec((2, LANES, d), lambda j, *_: (0, j, 0)))
    outs = pl.pallas_call(
        functools.partial(_combine_kernel, slots=slots, follow=follow, split=split),
        out_shape=tuple(out_shape),
        grid_spec=pltpu.PrefetchScalarGridSpec(
            num_scalar_prefetch=2, grid=(nbt,), in_specs=in_specs, out_specs=tuple(out_specs),
            scratch_shapes=[pltpu.VMEM((2, PICK_ROWS, d), F32), pltpu.VMEM((LANES, d), F32),
                            pltpu.VMEM((PICK_ROWS, LANES), jnp.int32), pltpu.SemaphoreType.DMA((2,))]),
        compiler_params=_params("arbitrary"), name="combine",
    )(*args)
    return outs if len(outs) > 1 else outs[0]


def _unpack_rows(words):
    lo = pltpu.bitcast(words << 16, F32)
    hi = pltpu.bitcast(words & jnp.uint32(0xFFFF0000), F32)
    return jnp.concatenate([lo, hi], axis=1).astype(BF16)


def _expert_weights(w_refs, w_f32, w_bf16, sems, layer):
    tn = w_f32.shape[2]
    n_col = pl.num_programs(1)
    _stationary_weights(
        w_refs, w_f32, w_bf16, sems, pl.program_id(0) * n_col + pl.program_id(1), pl.num_programs(0) * n_col,
        lambda ref, b: ref.at[layer, b // n_col, :, pl.ds(pl.multiple_of((b % n_col) * tn, tn), tn)], 2)


def _swiglu_kernel(a_ref, wg_ref, wu_ref, o_ref, w_f32, w_bf16, sems, *, layer):
    _expert_weights([wg_ref, wu_ref], w_f32, w_bf16, sems, layer)
    a = _unpack_rows(a_ref[...])
    gte = jnp.dot(a, w_bf16[0], preferred_element_type=F32)
    up = jnp.dot(a, w_bf16[1], preferred_element_type=F32)
    o_ref[...] = (gte * jax.nn.sigmoid(gte) * up).astype(o_ref.dtype)


def expert_swiglu(xe, w_gate, w_up, layer, *, tm=512, tn=512):
    e, c, _ = xe.shape
    k, f = w_gate.shape[-2:]
    tm, tn = _tile(c, tm), _tile(f, tn)
    w_spec = pl.BlockSpec(memory_space=pl.ANY)
    return pl.pallas_call(
        functools.partial(_swiglu_kernel, layer=layer),
        out_shape=jax.ShapeDtypeStruct((e, c, f), BF16),
        grid=(e, f // tn, c // tm),
        in_specs=[pl.BlockSpec((None, tm, k // 2), lambda x, j, i: (x, i, 0)), w_spec, w_spec],
        out_specs=pl.BlockSpec((None, tm, tn), lambda x, j, i: (x, i, j)),
        scratch_shapes=_weight_scratch(2, k, tn),
        compiler_params=_params("arbitrary", "arbitrary", "arbitrary"), name="expert_swiglu",
    )(xe, w_gate, w_up)


def _down_kernel(a_ref, w_ref, gate_ref, o_ref, w_f32, w_bf16, sems, *, layer):
    _expert_weights([w_ref], w_f32, w_bf16, sems, layer)
    acc = jnp.dot(a_ref[...], w_bf16[0], preferred_element_type=F32)
    o_ref[...] = acc * pltpu.bitcast(gate_ref[...], F32)[:, 0:1]


def expert_down(hid, w_down, xe, layer, *, tm=512, tn=2048):
    e, c, f = hid.shape
    d = w_down.shape[-1]
    tm, tn = _tile(c, tm), _tile(d, tn)
    gate_blk = (xe.shape[-1] - LANES) // LANES
    return pl.pallas_call(
        functools.partial(_down_kernel, layer=layer),
        out_shape=jax.ShapeDtypeStruct((e, c, d), F32),
        grid=(e, d // tn, c // tm),
        in_specs=[pl.BlockSpec((None, tm, f), lambda x, j, i: (x, i, 0)),
                  pl.BlockSpec(memory_space=pl.ANY),
                  pl.BlockSpec((None, tm, LANES), lambda x, j, i: (x, i, gate_blk))],
        out_specs=pl.BlockSpec((None, tm, tn), lambda x, j, i: (x, i, j)),
        scratch_shapes=_weight_scratch(1, f, tn),
        compiler_params=_params("arbitrary", "arbitrary", "arbitrary"), name="expert_down",
    )(hid, w_down, xe)


def _attn_kernel(q_ref, *rest, group, n_seg, chunk):
    kv_refs, o_ref = rest[:2 * n_seg], rest[2 * n_seg]
    tq = q_ref.shape[0]
    q = jnp.concatenate([q_ref[:, g * HEAD_DIM:(g + 1) * HEAD_DIM] for g in range(group)], axis=0)
    m = jnp.full((group * tq, 1), -jnp.inf, F32)
    acc = jnp.zeros((group * tq, 2 * HEAD_DIM), F32)
    for seg in range(n_seg):
        k_ref, v_ref = kv_refs[2 * seg], kv_refs[2 * seg + 1]
        length = k_ref.shape[0]
        step = math.gcd(length, chunk)
        for lo in range(0, length, step):
            k = k_ref[lo:lo + step, :].astype(BF16)
            v = v_ref[lo:lo + step, :].astype(BF16)
            v1 = jnp.concatenate([v, jnp.ones_like(v)], axis=1)
            s = lax.dot_general(q, k, (((1,), (1,)), ((), ())), preferred_element_type=F32)
            m_new = jnp.maximum(m, jnp.max(s, axis=-1, keepdims=True))
            p = jnp.exp2(s - m_new).astype(BF16)
            acc = jnp.exp2(m - m_new) * acc + jnp.dot(p, v1, preferred_element_type=F32)
            m = m_new
    o = acc[:, :HEAD_DIM] / acc[:, HEAD_DIM:HEAD_DIM + 1]
    for g in range(group):
        o_ref[:, g * HEAD_DIM:(g + 1) * HEAD_DIM] = o[g * tq:(g + 1) * tq].astype(o_ref.dtype)


def attention(q, segments, *, n_kv, n_req, q_row0, lq, tq, chunk=512):
    group = q.shape[-1] // HEAD_DIM // n_kv
    tq = _tile(lq, tq)
    qb0, nqb = q_row0 // tq, lq // tq
    qw = group * HEAD_DIM
    in_specs = [pl.BlockSpec((tq, qw), lambda r, h, t: (qb0 + r * nqb + t, h))]
    args = [q]
    for k_arr, v_arr, block, imap in segments:
        spec = pl.BlockSpec(block, lambda r, h, t, imap=imap: imap(r, h))
        in_specs += [spec, spec]
        args += [k_arr, v_arr]
    return pl.pallas_call(
        functools.partial(_attn_kernel, group=group, n_seg=len(segments), chunk=chunk),
        out_shape=jax.ShapeDtypeStruct((n_req * lq, q.shape[-1]), BF16),
        grid=(n_req, n_kv, nqb),
        in_specs=in_specs,
        out_specs=pl.BlockSpec((tq, qw), lambda r, h, t: (r * nqb + t, h)),
        compiler_params=_params("parallel", "parallel", "arbitrary"), name="attention",
    )(*args)


def _seq_dft_kernel(m_ref, u_ref, o_ref, f_scr):
    length = u_ref.shape[1]
    half = length // 2
    blk = math.gcd(half, MXU_COLS)
    n_blk = length // blk

    @pl.when(pl.program_id(2) == 0)
    def _():
        r = lax.broadcasted_iota(jnp.int32, (blk, blk), 0)
        c = lax.broadcasted_iota(jnp.int32, (blk, blk), 1)
        mirror = _ones_where((r >= 1) & (c == blk - r), BF16)
        first = _ones_where((r == 0) & (c == 0), BF16)
        row_id = lax.broadcasted_iota(jnp.int32, (blk, 1), 0)
        for b in range(half // blk):
            lo = b * blk
            for plane, sign in ((0, 1.0), (1, -1.0)):
                src = (n_blk - 1 - b) * blk
                mir = jnp.dot(mirror, u_ref[plane, src:src + blk, :], preferred_element_type=F32)
                if b >= 1:
                    mir += jnp.dot(first, u_ref[plane, src + blk:src + 2 * blk, :], preferred_element_type=F32)
                f = u_ref[plane, lo:lo + blk, :].astype(F32) + sign * mir
                if b == 0 and plane == 1:
                    f = jnp.where(row_id == 0, u_ref[0, half:half + 1, :].astype(F32), f)
                f_scr[plane, lo:lo + blk, :] = f.astype(BF16)

    acc = jnp.dot(m_ref[0], f_scr[0], preferred_element_type=F32)
    acc += jnp.dot(m_ref[1], f_scr[1], preferred_element_type=F32)
    o_ref[...] = acc.astype(o_ref.dtype)


def seq_dft(mats, u, *, n_req, row0, tm=512, tn=512):
    length, half = mats.shape[1:]
    d = u.shape[-1]
    tm, tn = _tile(length, tm), _tile(d, tn)
    rb0 = row0 // length
    return pl.pallas_call(
        _seq_dft_kernel,
        out_shape=jax.ShapeDtypeStruct((n_req * length, d), BF16),
        grid=(n_req, d // tn, length // tm),
        in_specs=[pl.BlockSpec((2, tm, half), lambda r, j, i: (0, i, 0)),
                  pl.BlockSpec((2, length, tn), lambda r, j, i: (0, rb0 + r, j))],
        out_specs=pl.BlockSpec((tm, tn), lambda r, j, i: (r * (length // tm) + i, j)),
        scratch_shapes=[pltpu.VMEM((2, half, tn), BF16)],
        compiler_params=_params("parallel", "parallel", "arbitrary"), name="seq_dft",
    )(mats, u)


def _seq_dft_mats(length):
    cos, sin = _dft_mats(length)
    half = length // 2
    return jnp.stack([cos[:, :half], (-sin[:, :half]).at[:, 0].set(cos[:, half])]).astype(BF16)


def _dft_mats(length):
    n2 = 1 << ((length.bit_length() - 1) // 2)
    n1 = length // n2
    assert n1 * n2 == length
    j = jnp.arange(length, dtype=jnp.int32)[:, None]
    ang_a = ((j * jnp.arange(n1, dtype=jnp.int32)[None, :]) % n1).astype(F32) * (2.0 * math.pi / n1)
    ang_b = ((j * jnp.arange(n2, dtype=jnp.int32)[None, :]) % length).astype(F32) * (2.0 * math.pi / length)
    ca, sa = jnp.cos(ang_a)[:, :, None], jnp.sin(ang_a)[:, :, None]
    cb, sb = jnp.cos(ang_b)[:, None, :], jnp.sin(ang_b)[:, None, :]
    norm = 1.0 / math.sqrt(length)
    cos = (ca * cb - sa * sb).reshape(length, length) * norm
    sin = (sa * cb + ca * sb).reshape(length, length) * norm
    return cos, sin


def _rope_tables(n_prompt, n_req, n_lat):
    half = HEAD_DIM // 2
    rows = n_lat // GRID_W
    row_pos = jnp.repeat(jnp.arange(rows, dtype=F32), GRID_W)
    col_pos = jnp.tile(jnp.arange(GRID_W, dtype=F32), rows)
    inv_freq = ROPE_THETA ** (-jnp.arange(0, half, 2, dtype=F32) / half)

    def cs(pos):
        ang = pos[:, None] * inv_freq[None, :]
        return jnp.cos(ang), jnp.sin(ang)

    cr, sr = cs(row_pos)
    cc, sc = cs(col_pos)
    cos = jnp.concatenate([cr, cr, cc, cc], axis=-1)
    sin = jnp.concatenate([-sr, sr, -sc, sc], axis=-1)
    cos = jnp.concatenate([jnp.ones((n_prompt, HEAD_DIM), F32), jnp.tile(cos, (n_req, 1))], axis=0)
    sin = jnp.concatenate([jnp.zeros((n_prompt, HEAD_DIM), F32), jnp.tile(sin, (n_req, 1))], axis=0)
    return cos, sin


def kernel(x_prompt, x_sample, cache_k, cache_v, c, c_ctx, norm_g, w_mod, b_mod, w_q, w_k, w_v, w_o,
           q_norm_g, k_norm_g, w_fourier, w_router, w_gate, w_up, w_down, final_norm_g):
    batch, seq, d = x_prompt.shape
    n_req, n_lat, _ = x_sample.shape
    depth = w_mod.shape[0]
    n_kv = cache_k.shape[3]
    n_exp = w_router.shape[-1]
    n_prompt = batch * seq
    nt = n_prompt + n_req * n_lat
    assert n_req + 1 <= COND_ROWS
    assert n_prompt % n_lat == 0

    def cond_row(r):
        return jnp.where(r < n_prompt, 0, 1 + (r - n_prompt) // n_lat)

    cond = jnp.zeros((COND_ROWS, d), F32).at[0].set(c_ctx).at[1:1 + n_req].set(c)
    mod = modulation_all(cond, w_mod, b_mod)
    cos, sin = _rope_tables(n_prompt, n_req, n_lat)
    scale = math.log2(math.e) / math.sqrt(HEAD_DIM)

    cc, sc = _dft_mats(FGROUP_DIM)
    cs_chan = jnp.concatenate([cc, sc], axis=1).astype(BF16)
    mats_p = mats_s = None
    if depth > 1:
        mats_p, mats_s = _seq_dft_mats(seq), _seq_dft_mats(n_lat)

    x = (x_prompt.reshape(n_prompt, d), x_sample.reshape(n_req * n_lat, d))
    assert n_prompt % LANES == 0 and (n_req * n_lat) % LANES == 0
    cap_p = (CAPACITY_FACTOR * n_prompt) // n_exp
    cap_s = (CAPACITY_FACTOR * n_req * n_lat) // n_exp
    groups = ((0, n_prompt // LANES, cap_p, 0), (n_prompt // LANES, n_req * n_lat // LANES, cap_s, cap_p))
    slots = cap_p + cap_s
    assert cap_p % SUBLANES == 0 and cap_s % SUBLANES == 0
    new_k, new_v = [], []
    mixer_in = None
    for i in range(depth):
        mod_l = mod[i].reshape(COND_ROWS * N_MOD, 1, d)
        j = i // 2
        if i % 2 == 0:
            h = mixer_in if i > 0 else ada_norm(x, norm_g[i, 0], mod_l, 0, 1, cond_row)
            q = project(h, w_q, j, out_dtypes=(BF16,),
                        head=(q_norm_g[j].reshape(1, HEAD_DIM) * scale, cos, sin))
            kf, kb = project(h, w_k, j, out_dtypes=(F32, BF16),
                             head=(k_norm_g[j].reshape(1, HEAD_DIM), cos, sin))
            vf, vb = project(h, w_v, j, out_dtypes=(F32, BF16))
            kvw = n_kv * HEAD_DIM
            new_k.append(kf[:n_prompt].reshape(batch, seq, n_kv, HEAD_DIM))
            new_v.append(vf[:n_prompt].reshape(batch, seq, n_kv, HEAD_DIM))
            att_p = attention(q, [(kb, vb, (seq, HEAD_DIM), lambda r, h: (r, h))],
                              n_kv=n_kv, n_req=batch, q_row0=0, lq=seq, tq=256)
            past = cache_k.shape[2]
            lat0 = n_prompt // n_lat
            att_s = attention(
                q,
                [(cache_k.reshape(n_req, -1, past, kvw), cache_v.reshape(n_req, -1, past, kvw),
                  (None, None, past, HEAD_DIM), lambda r, h, j=j: (r, j, 0, h)),
                 (kb, vb, (n_lat, HEAD_DIM), lambda r, h: (lat0 + r, h))],
                n_kv=n_kv, n_req=n_req, q_row0=n_prompt, lq=n_lat, tq=256)
            x = project((att_p, att_s), w_o, j, out_dtypes=(F32,), residual=(x, mod_l, 2, cond_row))
        else:
            u = mixer_in if i > 0 else ada_norm(x, norm_g[i, 0], mod_l, 0, 1, cond_row, mode="dft", cs=cs_chan)
            f_p = seq_dft(mats_p, u, n_req=batch, row0=0, tn=d)
            f_s = seq_dft(mats_s, u, n_req=n_req, row0=n_prompt)
            x = project((f_p, f_s), w_fourier, j, out_dtypes=(F32,), residual=(x, mod_l, 2, cond_row))

        h, aff = ada_norm(x, norm_g[i, 1], mod_l, 3, 4, cond_row, mode="router",
                          w_router_t=w_router[i].T)
        pick, tot, offs = route_select(aff, groups)
        xe = dispatch(h, pick, aff, tot, offs, slots)
        hid = expert_swiglu(xe, w_gate, w_up, i)
        ye = expert_down(hid, w_down, xe, i)
        if i + 1 < depth:
            follow = "plain" if (i + 1) % 2 == 0 else "dft"
            x, mixer_in = combine(x, ye, pick, tot, offs, mod_l, 5, cond_row, follow=follow, g=norm_g[i + 1, 0],
                                  mod_next=mod[i + 1].reshape(COND_ROWS * N_MOD, 1, d),
                                  cs=cs_chan if follow == "dft" else None)
        else:
            y_p, y_s = combine(x, ye, pick, tot, offs, mod_l, 5, cond_row, follow="final", g=final_norm_g,
                               first_rows=n_prompt)

    return (y_p.reshape(batch, seq, d), y_s.reshape(n_req, n_lat, d),
            jnp.stack(new_k, axis=1), jnp.stack(new_v, axis=1))
```

```python
import functools
import math

import jax
import jax.numpy as jnp
from jax import lax
from jax.experimental import pallas as pl
from jax.experimental.pallas import tpu as pltpu

F32 = jnp.float32
BF16 = jnp.bfloat16

GRID_W = 64
HEAD_DIM = 128
FGROUP_DIM = 128
ROPE_THETA = 10000.0
CAPACITY_FACTOR = 2
N_MOD = 6
EPS = 1e-6

LANES = 128
SUBLANES = 8
MXU_COLS = 256
COND_ROWS = 8
VMEM_LIMIT_BYTES = 56 * 1024 * 1024


def _params(*sem):
    return pltpu.CompilerParams(dimension_semantics=sem, vmem_limit_bytes=VMEM_LIMIT_BYTES)


def _tile(n, want):
    return math.gcd(n, want)


def _mod_kernel(c_ref, w_ref, b_ref, o_ref):
    c = c_ref[...]
    a = (c * jax.nn.sigmoid(c)).astype(BF16)
    o_ref[...] = jnp.dot(a, w_ref[...].astype(BF16), preferred_element_type=F32) + b_ref[...]


def modulation_all(cond, w_mod, b_mod):
    depth, d, n = w_mod.shape
    tn = _tile(n, 512)
    return pl.pallas_call(
        _mod_kernel,
        out_shape=jax.ShapeDtypeStruct((depth, COND_ROWS, n), F32),
        grid=(depth, n // tn),
        in_specs=[
            pl.BlockSpec((COND_ROWS, d), lambda l, j: (0, 0)),
            pl.BlockSpec((None, d, tn), lambda l, j: (l, 0, j)),
            pl.BlockSpec((None, 1, tn), lambda l, j: (l, 0, j)),
        ],
        out_specs=pl.BlockSpec((None, COND_ROWS, tn), lambda l, j: (l, 0, j)),
        compiler_params=_params("parallel", "parallel"),
        name="modulation",
    )(cond, w_mod, b_mod.reshape(depth, 1, n))


def _rms(x, g):
    return x * lax.rsqrt(jnp.mean(x * x, axis=-1, keepdims=True) + EPS) * g


def _split_rows(specs_of, first, rest_rows, tm):
    split = first.shape[0] // tm
    assert split * tm == first.shape[0] and rest_rows.shape[0] % tm == 0
    return (specs_of(lambda i: jnp.minimum(i, split - 1)), specs_of(lambda i: jnp.maximum(i - split, 0))), split


def _ada_kernel(*refs, mode, split):
    if split is None:
        x_ref, g_ref, sh_ref, sc_ref, *rest = refs
        _ada_apply(x_ref[...], g_ref, sh_ref, sc_ref, rest, mode)
        return
    x0_ref, x1_ref, g_ref, sh_ref, sc_ref, *rest = refs

    @pl.when(pl.program_id(0) < split)
    def _():
        _ada_apply(x0_ref[...], g_ref, sh_ref, sc_ref, rest, mode)

    @pl.when(pl.program_id(0) >= split)
    def _():
        _ada_apply(x1_ref[...], g_ref, sh_ref, sc_ref, rest, mode)


def _ada_apply(x, g_ref, sh_ref, sc_ref, rest, mode):
    h = _rms(x, g_ref[...]) * (1.0 + sc_ref[...]) + sh_ref[...]
    hb = h.astype(BF16)
    if mode == "plain":
        (h_ref,) = rest
        h_ref[...] = hb
    elif mode == "router":
        wr_ref, h_ref, aff_ref = rest
        h_ref[...] = hb
        logits = lax.dot_general(wr_ref[...].astype(BF16), hb, (((1,), (1,)), ((), ())),
                                 preferred_element_type=F32)
        m = jnp.max(logits, axis=0, keepdims=True)
        p = jnp.exp(logits - m)
        aff = p / jnp.sum(p, axis=0, keepdims=True)
        for blk in range(aff_ref.shape[0]):
            aff_ref[blk] = aff[:, blk * LANES:(blk + 1) * LANES]
    else:
        cs_ref, u_ref = rest
        n_groups = hb.shape[1] // FGROUP_DIM
        for grp in range(n_groups):
            lo = grp * FGROUP_DIM
            r = jnp.dot(hb[:, lo:lo + FGROUP_DIM], cs_ref[...], preferred_element_type=F32)
            u_ref[0, :, lo:lo + FGROUP_DIM] = r[:, :FGROUP_DIM].astype(BF16)
            u_ref[1, :, lo:lo + FGROUP_DIM] = r[:, FGROUP_DIM:].astype(BF16)


def ada_norm(x, g, mod_l, shift_idx, scale_idx, cond_row, *, mode="plain", w_router_t=None, cs=None):
    row = lambda i: (i, 0)
    if isinstance(x, tuple):
        d = x[0].shape[1]
        nt = x[0].shape[0] + x[1].shape[0]
        tm = _tile(math.gcd(x[0].shape[0], x[1].shape[0]), 256)
        x_specs, split = _split_rows(lambda blk: pl.BlockSpec((tm, d), lambda i: (blk(i), 0)), x[0], x[1], tm)
        x_specs, x_args = list(x_specs), list(x)
    else:
        nt, d = x.shape
        tm = _tile(nt, 256)
        x_specs, x_args, split = [pl.BlockSpec((tm, d), row)], [x], None
    grid = (nt // tm,)
    in_specs = x_specs + [
        pl.BlockSpec((1, d), lambda i: (0, 0)),
        pl.BlockSpec((None, 1, d), lambda i: (cond_row(i * tm) * N_MOD + shift_idx, 0, 0)),
        pl.BlockSpec((None, 1, d), lambda i: (cond_row(i * tm) * N_MOD + scale_idx, 0, 0)),
    ]
    args = x_args + [g.reshape(1, d), mod_l, mod_l]
    if mode == "plain":
        out_shape = jax.ShapeDtypeStruct((nt, d), BF16)
        out_specs = pl.BlockSpec((tm, d), row)
    elif mode == "router":
        e = w_router_t.shape[0]
        in_specs.append(pl.BlockSpec((e, d), lambda i: (0, 0)))
        args.append(w_router_t)
        out_shape = (jax.ShapeDtypeStruct((nt, d), BF16), jax.ShapeDtypeStruct((nt // LANES, e, LANES), F32))
        out_specs = (pl.BlockSpec((tm, d), row), pl.BlockSpec((tm // LANES, e, LANES), lambda i: (i, 0, 0)))
    else:
        in_specs.append(pl.BlockSpec(cs.shape, lambda i: (0, 0)))
        args.append(cs)
        out_shape = jax.ShapeDtypeStruct((2, nt, d), BF16)
        out_specs = pl.BlockSpec((2, tm, d), lambda i: (0, i, 0))
    return pl.pallas_call(
        functools.partial(_ada_kernel, mode=mode, split=split),
        out_shape=out_shape, grid=grid, in_specs=in_specs, out_specs=out_specs,
        compiler_params=_params("parallel"), name="ada_norm_" + mode,
    )(*args)


def _stationary_weights(w_hbm, w_f32, w_bf16, sems, block, n_blocks, slicer, inner_axis):
    def copy(b, which):
        return pltpu.make_async_copy(slicer(w_hbm[which], b), w_f32.at[which], sems.at[which])

    @pl.when(pl.program_id(inner_axis) == 0)
    def _():
        @pl.when(block == 0)
        def _():
            for which in range(len(w_hbm)):
                copy(block, which).start()

        for which in range(len(w_hbm)):
            copy(block, which).wait()
            w_bf16[which] = w_f32[which].astype(BF16)

        @pl.when(block + 1 < n_blocks)
        def _():
            for which in range(len(w_hbm)):
                copy(block + 1, which).start()


def _weight_scratch(n_w, k, tn):
    return [pltpu.VMEM((n_w, k, tn), F32), pltpu.VMEM((n_w, k, tn), BF16), pltpu.SemaphoreType.DMA((n_w,))]


def _project_weights(w_ref, w_f32, w_bf16, sems, layer):
    tn = w_f32.shape[2]
    _stationary_weights(
        [w_ref], w_f32, w_bf16, sems, pl.program_id(0), pl.num_programs(0),
        lambda ref, b: ref.at[layer, :, pl.ds(pl.multiple_of(b * tn, tn), tn)], 1)


def _mm_plain_kernel(a_ref, w_ref, *rest, n_out, layer):
    outs, (w_f32, w_bf16, sems) = rest[:n_out], rest[n_out:]
    _project_weights(w_ref, w_f32, w_bf16, sems, layer)
    acc = jnp.dot(a_ref[...], w_bf16[0], preferred_element_type=F32)
    for o_ref in outs:
        o_ref[...] = acc.astype(o_ref.dtype)


def _swap32(y):
    lane = lax.broadcasted_iota(jnp.int32, y.shape, 1)
    first = (lane % 64) < 32
    return jnp.where(first, pltpu.roll(y, 96, 1), pltpu.roll(y, 32, 1))


def _mm_headnorm_kernel(a_ref, w_ref, g_ref, cos_ref, sin_ref, *rest, n_out, layer):
    outs, (w_f32, w_bf16, sems) = rest[:n_out], rest[n_out:]
    _project_weights(w_ref, w_f32, w_bf16, sems, layer)
    a = a_ref[...]
    cos, sin = cos_ref[...], sin_ref[...]
    tn = w_bf16.shape[2]
    step = math.gcd(tn, MXU_COLS)
    for c0 in range(0, tn, step):
        acc = jnp.dot(a, w_bf16[0, :, c0:c0 + step], preferred_element_type=F32)
        for lo in range(0, step, HEAD_DIM):
            y = _rms(acc[:, lo:lo + HEAD_DIM], g_ref[...])
            y = y * cos + _swap32(y) * sin
            for o_ref in outs:
                o_ref[:, c0 + lo:c0 + lo + HEAD_DIM] = y.astype(o_ref.dtype)


def _mm_residual_kernel(a0_ref, a1_ref, w_ref, *rest, layer, split, res_split):
    if res_split:
        res0_ref, res1_ref, gate_ref, o_ref, w_f32, w_bf16, sems = rest
    else:
        res0_ref, gate_ref, o_ref, w_f32, w_bf16, sems = rest
        res1_ref = res0_ref
    _project_weights(w_ref, w_f32, w_bf16, sems, layer)

    def emit(a_ref, res_ref):
        acc = jnp.dot(a_ref[...], w_bf16[0], preferred_element_type=F32)
        o_ref[...] = res_ref[...] + gate_ref[...] * acc

    @pl.when(pl.program_id(1) < split)
    def _():
        emit(a0_ref, res0_ref)

    @pl.when(pl.program_id(1) >= split)
    def _():
        emit(a1_ref, res1_ref)


def project(a, w_stack, layer, *, out_dtypes, head=None, residual=None, tm=512, tn=1024):
    n = w_stack.shape[-1]
    if residual is not None:
        a0, a1 = a
        k = a0.shape[1]
        m = a0.shape[0] + a1.shape[0]
        tm, tn = _tile(math.gcd(a0.shape[0], a1.shape[0]), tm), _tile(n, tn)
        split = a0.shape[0] // tm
        in_specs = [pl.BlockSpec((tm, k), lambda j, i: (jnp.minimum(i, split - 1), 0)),
                    pl.BlockSpec((tm, k), lambda j, i: (jnp.maximum(i - split, 0), 0)),
                    pl.BlockSpec(memory_space=pl.ANY)]
        args = [a0, a1, w_stack]
    else:
        m, k = a.shape
        tm, tn = _tile(m, tm), _tile(n, tn)
        in_specs = [pl.BlockSpec((tm, k), lambda j, i: (i, 0)), pl.BlockSpec(memory_space=pl.ANY)]
        args = [a, w_stack]
    grid = (n // tn, m // tm)
    out_block = pl.BlockSpec((tm, tn), lambda j, i: (i, j))
    if head is not None:
        g, cos, sin = head
        in_specs += [pl.BlockSpec((1, HEAD_DIM), lambda j, i: (0, 0)),
                     pl.BlockSpec((tm, HEAD_DIM), lambda j, i: (i, 0)),
                     pl.BlockSpec((tm, HEAD_DIM), lambda j, i: (i, 0))]
        args += [g, cos, sin]
        body = functools.partial(_mm_headnorm_kernel, n_out=len(out_dtypes), layer=layer)
    elif residual is not None:
        res, mod_l, gate_idx, cond_row = residual
        if isinstance(res, tuple):
            assert res[0].shape[0] == a0.shape[0]
            in_specs += [pl.BlockSpec((tm, tn), lambda j, i: (jnp.minimum(i, split - 1), j)),
                         pl.BlockSpec((tm, tn), lambda j, i: (jnp.maximum(i - split, 0), j))]
            args += list(res)
        else:
            in_specs.append(pl.BlockSpec((tm, tn), lambda j, i: (i, j)))
            args.append(res)
        in_specs.append(pl.BlockSpec((None, 1, tn), lambda j, i: (cond_row(i * tm) * N_MOD + gate_idx, 0, j)))
        args.append(mod_l)
        body = functools.partial(_mm_residual_kernel, layer=layer, split=split, res_split=isinstance(res, tuple))
    else:
        body = functools.partial(_mm_plain_kernel, n_out=len(out_dtypes), layer=layer)
    outs = pl.pallas_call(
        body,
        out_shape=tuple(jax.ShapeDtypeStruct((m, n), dt) for dt in out_dtypes),
        grid=grid, in_specs=in_specs, out_specs=tuple(out_block for _ in out_dtypes),
        scratch_shapes=_weight_scratch(1, k, tn),
        compiler_params=_params("arbitrary", "arbitrary"), name="project",
    )(*args)
    return outs if len(outs) > 1 else outs[0]


PICK_ROWS = 512


def _ones_where(mask, dtype):
    return jnp.where(mask, 1.0, 0.0).astype(dtype)


def _select_kernel(aff_ref, pick_ref, tot_ref, offs_ref, w_scr, o_scr, *, groups):
    n_exp = aff_ref.shape[1]
    li = lax.broadcasted_iota(jnp.int32, (LANES, LANES), 0)
    lj = lax.broadcasted_iota(jnp.int32, (LANES, LANES), 1)
    tri = _ones_where(li <= lj, BF16)

    def chunk_cumsum(mask, nb):
        w = jnp.dot(_ones_where(mask, BF16).reshape(nb * n_exp, LANES), tri,
                    preferred_element_type=F32).reshape(nb, n_exp, LANES)
        w_scr[0:nb] = w

        def body(j, run):
            o_scr[j] = jnp.broadcast_to(run, (n_exp, LANES))
            return run + w_scr[j][:, LANES - 1:LANES]

        lax.fori_loop(0, nb, body, jnp.zeros((n_exp, 1), F32))
        return w, o_scr[0:nb]

    for j0, nb, cap, slot0 in groups:
        keys = pltpu.bitcast(aff_ref[j0:j0 + nb], jnp.int32)

        def count(mask):
            return jnp.sum(jnp.sum(_ones_where(mask, F32), axis=0), axis=1, keepdims=True)

        def search(it, prefix):
            cand = prefix | lax.shift_left(jnp.int32(1), 30 - it)
            return jnp.where(count(keys >= cand[None]) >= cap, cand, prefix)

        thr = lax.fori_loop(0, 31, search, jnp.zeros((n_exp, 1), jnp.int32))[None]
        above, ties = keys > thr, keys == thr
        need = (cap - count(above))[None]
        w_eq, o_eq = chunk_cumsum(ties, nb)
        sel = above | (ties & (w_eq + o_eq <= need))
        w, offs = chunk_cumsum(sel, nb)
        tot = jnp.broadcast_to(w[:, :, LANES - 1:LANES], w.shape)
        lead = offs - SUBLANES * jnp.floor(offs / SUBLANES)
        seg = SUBLANES * jnp.floor((lead + tot + (SUBLANES - 1)) / SUBLANES)
        rows = nb * n_exp
        blk = math.gcd(rows, 256)
        bi = lax.broadcasted_iota(jnp.int32, (blk, blk), 0)
        bj = lax.broadcasted_iota(jnp.int32, (blk, blk), 1)
        lower = _ones_where((bi // n_exp == bj // n_exp) & (bj % n_exp < bi % n_exp), BF16)
        seg2 = seg.reshape(rows, LANES).astype(BF16)
        base = jnp.concatenate(
            [jnp.dot(lower, seg2[r0:r0 + blk], preferred_element_type=F32) for r0 in range(0, rows, blk)],
            axis=0).reshape(nb, n_exp, LANES)
        pick_ref[j0:j0 + nb] = jnp.where(sel, base + lead + w - 1.0, -1.0).astype(jnp.int32)
        tot_ref[j0:j0 + nb] = tot.astype(jnp.int32)
        offs_ref[j0:j0 + nb] = offs.astype(jnp.int32) + slot0


def route_select(aff, groups):
    nbt, e, _ = aff.shape
    shape = jax.ShapeDtypeStruct(aff.shape, jnp.int32)
    max_nb = max(g[1] for g in groups)
    pick, tot, offs = pl.pallas_call(
        functools.partial(_select_kernel, groups=groups),
        out_shape=(shape, shape, shape),
        scratch_shapes=[pltpu.VMEM((max_nb, e, LANES), F32), pltpu.VMEM((max_nb, e, LANES), F32)],
        compiler_params=pltpu.CompilerParams(vmem_limit_bytes=VMEM_LIMIT_BYTES), name="route_select",
    )(aff)
    return pick, tot[:, :, 0].reshape(nbt * e), offs[:, :, 0].reshape(nbt * e)


def _sub_blocks():
    first = (3 * PICK_ROWS // 4) // 16 * 16
    return ((0, first), (first, PICK_ROWS - first))


def _fill_owner_rows(tot_ref, offs_ref, j, n_exp, row0, pairs):
    def per_expert(e, base):
        _, _, seg = _segment(tot_ref, offs_ref, j, n_exp, e)
        lo = jnp.maximum(base, row0)
        hi = jnp.minimum(base + seg, row0 + PICK_ROWS)
        vals = [jnp.broadcast_to(src[pl.ds(e, 1), :], (SUBLANES, LANES)) for src, _ in pairs]

        def per_group(g, carry):
            row = pl.multiple_of(lo - row0 + g * SUBLANES, SUBLANES)
            for val, (_, dst) in zip(vals, pairs):
                dst[pl.ds(row, SUBLANES), :] = val
            return carry

        lax.fori_loop(0, jnp.maximum(hi - lo, 0) // SUBLANES, per_group, 0)
        return base + seg

    lax.fori_loop(0, n_exp, per_expert, jnp.int32(0))


def _segment(tot_ref, offs_ref, j, n_exp, e):
    t = tot_ref[j * n_exp + e]
    o = offs_ref[j * n_exp + e]
    r = o & (SUBLANES - 1)
    return o, r, ((r + t + SUBLANES - 1) // SUBLANES) * SUBLANES


def _for_each_group(tot_ref, offs_ref, j, n_exp, row0, rows, fn):
    def per_expert(e, base):
        o, r, seg = _segment(tot_ref, offs_ref, j, n_exp, e)
        lo = jnp.maximum(base, row0)
        hi = jnp.minimum(base + seg, row0 + rows)
        slot = (o - r) + (lo - base)

        def per_group(g, carry):
            fn(e, pl.multiple_of(slot + g * SUBLANES, SUBLANES), pl.multiple_of(lo - row0 + g * SUBLANES, SUBLANES))
            return carry

        lax.fori_loop(0, jnp.maximum(hi - lo, 0) // SUBLANES, per_group, 0)
        return base + seg

    lax.fori_loop(0, n_exp, per_expert, jnp.int32(0))


def _groups_in_round(tot_ref, offs_ref, j, n_exp, r):
    rows = _staging_rows(tot_ref, offs_ref, j, n_exp) - r * PICK_ROWS
    return jnp.clip(rows, 0, PICK_ROWS) // SUBLANES


def _staging_rows(tot_ref, offs_ref, j, n_exp):
    return lax.fori_loop(0, n_exp, lambda e, s: s + _segment(tot_ref, offs_ref, j, n_exp, e)[2], jnp.int32(0))


def _rounds(tot_ref, offs_ref, j, n_exp):
    return (_staging_rows(tot_ref, offs_ref, j, n_exp) + PICK_ROWS - 1) // PICK_ROWS


def _dispatch_kernel(tot_ref, offs_ref, h_ref, pick_ref, aff_ref, xe_ref, stage2, tail, owner_pick, owner_aff, sems):
    j = pl.program_id(0)
    buf = j % 2
    stage = stage2.at[buf]
    n_exp = pick_ref.shape[0]
    half = h_ref.shape[1] // 2
    total = _staging_rows(tot_ref, offs_ref, j, n_exp)
    n_rounds = (total + PICK_ROWS - 1) // PICK_ROWS

    @pl.when(j == 0)
    def _():
        tail[...] = jnp.zeros_like(tail)
        owner_aff[...] = jnp.zeros_like(owner_aff)

    def copy(b, e, slot, row):
        return pltpu.make_async_copy(stage2.at[b, pl.ds(row, SUBLANES)], xe_ref.at[e, pl.ds(slot, SUBLANES)],
                                     sems.at[b])

    def start_round(jj, r, b):
        _for_each_group(tot_ref, offs_ref, jj, n_exp, r * PICK_ROWS, PICK_ROWS,
                        lambda e, slot, row: copy(b, e, slot, row).start())

    def wait_round(jj, r, b):
        lax.fori_loop(0, _groups_in_round(tot_ref, offs_ref, jj, n_exp, r),
                      lambda g, c: (copy(b, 0, 0, 0).wait(), c)[1], 0)

    def splice_tails(row0):
        def per_expert(e, base):
            _, r, seg = _segment(tot_ref, offs_ref, j, n_exp, e)
            last = base + seg - SUBLANES

            @pl.when((r > 0) & (base >= row0) & (base < row0 + PICK_ROWS))
            def _():
                row = pl.multiple_of(base - row0, SUBLANES)
                keep = lax.broadcasted_iota(jnp.int32, (SUBLANES, 1), 0) < r
                stage[pl.ds(row, SUBLANES), :] = jnp.where(keep, tail[e], stage[pl.ds(row, SUBLANES), :])

            @pl.when((seg > 0) & (last >= row0) & (last < row0 + PICK_ROWS))
            def _():
                tail[e] = stage[pl.ds(pl.multiple_of(last - row0, SUBLANES), SUBLANES), :]

            return base + seg

        lax.fori_loop(0, n_exp, per_expert, jnp.int32(0))

    def build(r):
        row0 = r * PICK_ROWS
        owner_pick[...] = jnp.full(owner_pick.shape, -1, jnp.int32)
        _fill_owner_rows(tot_ref, offs_ref, j, n_exp, row0, ((pick_ref, owner_pick), (aff_ref, owner_aff)))
        for b0, n in _sub_blocks():
            @pl.when(total - row0 > b0)
            def _():
                ids = lax.broadcasted_iota(jnp.int32, (n, LANES), 0) + (row0 + b0)
                hit = ids == owner_pick[b0:b0 + n, :]
                rows = jnp.dot(_ones_where(hit, BF16), h_ref[...], preferred_element_type=F32)
                bits = pltpu.bitcast(rows, jnp.uint32)
                stage[b0:b0 + n, 0:half] = bits[:, half:] | (bits[:, :half] >> 16)
                g = jnp.sum(jnp.where(hit, owner_aff[b0:b0 + n, :], 0.0), axis=1, keepdims=True)
                stage[b0:b0 + n, half:half + LANES] = pltpu.bitcast(jnp.broadcast_to(g, (n, LANES)), jnp.uint32)
        splice_tails(row0)

    @pl.when(n_rounds > 0)
    def _():
        build(0)

    @pl.when(j > 0)
    def _():
        prev_rounds = _rounds(tot_ref, offs_ref, j - 1, n_exp)

        @pl.when(prev_rounds > 0)
        def _():
            wait_round(j - 1, prev_rounds - 1, 1 - buf)

    @pl.when(n_rounds > 0)
    def _():
        start_round(j, 0, buf)

    def later_round(r, carry):
        wait_round(j, r - 1, buf)
        build(r)
        start_round(j, r, buf)
        return carry

    lax.fori_loop(1, n_rounds, later_round, 0)

    @pl.when((j == pl.num_programs(0) - 1) & (n_rounds > 0))
    def _():
        wait_round(j, n_rounds - 1, buf)


def dispatch(h, pick, aff, tot, offs, slots):
    nt, d = h.shape
    nbt, e, _ = pick.shape
    width = d // 2 + LANES
    chunk = pl.BlockSpec((None, e, LANES), lambda j, *_: (j, 0, 0))
    return pl.pallas_call(
        _dispatch_kernel,
        out_shape=jax.ShapeDtypeStruct((e, slots, width), jnp.uint32),
        grid_spec=pltpu.PrefetchScalarGridSpec(
            num_scalar_prefetch=2, grid=(nbt,),
            in_specs=[pl.BlockSpec((LANES, d), lambda j, *_: (j, 0)), chunk, chunk],
            out_specs=pl.BlockSpec(memory_space=pl.ANY),
            scratch_shapes=[pltpu.VMEM((2, PICK_ROWS, width), jnp.uint32),
                            pltpu.VMEM((e, SUBLANES, width), jnp.uint32),
                            pltpu.VMEM((PICK_ROWS, LANES), jnp.int32), pltpu.VMEM((PICK_ROWS, LANES), F32),
                            pltpu.SemaphoreType.DMA((2,))]),
        compiler_params=_params("arbitrary"), name="dispatch",
    )(tot, offs, h, pick, aff)


def _combine_kernel(tot_ref, offs_ref, x_ref, pick_ref, gate_ref, ye_ref, *rest, slots, follow, split):
    n_in = {None: 0, "final": 1, "plain": 3, "dft": 4}[follow]
    follow_in, (o_ref, *follow_out), (stage2, acc_ref, owner_pick, sems) = rest[:n_in], rest[n_in:-4], rest[-4:]
    j = pl.program_id(0)
    buf = j % 2
    n_exp = pick_ref.shape[0]
    total = _staging_rows(tot_ref, offs_ref, j, n_exp)
    n_rounds = (total + PICK_ROWS - 1) // PICK_ROWS

    def copy(b, e, slot, row):
        return pltpu.make_async_copy(ye_ref.at[pl.ds(pl.multiple_of(e * slots + slot, SUBLANES), SUBLANES)],
                                     stage2.at[b, pl.ds(row, SUBLANES)], sems.at[b])

    def start_round(jj, r, b):
        _for_each_group(tot_ref, offs_ref, jj, n_exp, r * PICK_ROWS, PICK_ROWS,
                        lambda e, slot, row: copy(b, e, slot, row).start())

    def wait_round(jj, r, b):
        lax.fori_loop(0, _groups_in_round(tot_ref, offs_ref, jj, n_exp, r),
                      lambda g, c: (copy(b, 0, 0, 0).wait(), c)[1], 0)

    @pl.when((j == 0) & (n_rounds > 0))
    def _():
        start_round(0, 0, 0)

    @pl.when(j + 1 < pl.num_programs(0))
    def _():
        @pl.when(_rounds(tot_ref, offs_ref, j + 1, n_exp) > 0)
        def _():
            start_round(j + 1, 0, 1 - buf)

    acc_ref[...] = jnp.zeros_like(acc_ref)

    def one_round(r, carry):
        row0 = r * PICK_ROWS

        @pl.when(r > 0)
        def _():
            start_round(j, r, buf)

        wait_round(j, r, buf)
        owner_pick[...] = jnp.full(owner_pick.shape, -1, jnp.int32)
        _fill_owner_rows(tot_ref, offs_ref, j, n_exp, row0, ((pick_ref, owner_pick),))
        for b0, n in _sub_blocks():
            @pl.when(total - row0 > b0)
            def _():
                ids = lax.broadcasted_iota(jnp.int32, (n, LANES), 0) + (row0 + b0)
                onehot = _ones_where(ids == owner_pick[b0:b0 + n, :], F32).T.astype(BF16)
                live = ids[:, 0:1] < total
                y = jnp.where(live, stage2[buf, b0:b0 + n, :], 0.0)
                acc = acc_ref[...]
                for _ in range(2):
                    part = y.astype(BF16)
                    acc = acc + jnp.dot(onehot, part, preferred_element_type=F32)
                    y = y - part.astype(F32)
                acc_ref[...] = acc
        return carry

    lax.fori_loop(0, n_rounds, one_round, 0)
    x_new = x_ref[...] + gate_ref[...] * acc_ref[...]
    if follow == "final":
        y_out = _rms(x_new, follow_in[0][...])

        @pl.when(j < split)
        def _():
            o_ref[...] = y_out

        @pl.when(j >= split)
        def _():
            follow_out[0][...] = y_out
    else:
        o_ref[...] = x_new
        if follow is not None:
            g_ref, sh_ref, sc_ref = follow_in[:3]
            _ada_apply(x_new, g_ref, sh_ref, sc_ref, tuple(follow_in[3:]) + tuple(follow_out), follow)


def combine(x, ye, pick, tot, offs, mod_l, gate_idx, cond_row, *, follow=None, g=None, mod_next=None, cs=None,
            first_rows=None):
    nt, d = x.shape
    nbt, e, _ = pick.shape
    slots = ye.shape[1]
    rows = pl.BlockSpec((LANES, d), lambda j, *_: (j, 0))
    in_specs = [rows,
                pl.BlockSpec((None, e, LANES), lambda j, *_: (j, 0, 0)),
                pl.BlockSpec((None, 1, d), lambda j, *_: (cond_row(j * LANES) * N_MOD + gate_idx, 0, 0)),
                pl.BlockSpec(memory_space=pl.ANY)]
    args = [tot, offs, x, pick, mod_l, ye.reshape(e * slots, d)]
    out_shape, out_specs, split = [jax.ShapeDtypeStruct((nt, d), F32)], [rows], None
    if follow is not None:
        in_specs.append(pl.BlockSpec((1, d), lambda j, *_: (0, 0)))
        args.append(g.reshape(1, d))
    if follow == "final":
        split = first_rows // LANES
        assert split * LANES == first_rows
        out_shape = [jax.ShapeDtypeStruct((first_rows, d), F32), jax.ShapeDtypeStruct((nt - first_rows, d), F32)]
        out_specs = [pl.BlockSpec((LANES, d), lambda j, *_: (jnp.minimum(j, split - 1), 0)),
                     pl.BlockSpec((LANES, d), lambda j, *_: (jnp.maximum(j - split, 0), 0))]
    if follow in ("plain", "dft"):
        in_specs += [pl.BlockSpec((None, 1, d), lambda j, *_: (cond_row(j * LANES) * N_MOD + 0, 0, 0)),
                     pl.BlockSpec((None, 1, d), lambda j, *_: (cond_row(j * LANES) * N_MOD + 1, 0, 0))]
        args += [mod_next, mod_next]
    if follow == "plain":
        out_shape.append(jax.ShapeDtypeStruct((nt, d), BF16))
        out_specs.append(rows)
    if follow == "dft":
        in_specs.append(pl.BlockSpec(cs.shape, lambda j, *_: (0, 0)))
        args.append(cs)
        out_shape.append(jax.ShapeDtypeStruct((2, nt, d), BF16))
        out_specs.append(pl.BlockSpec((2, LANES, d), lambda j, *_: (0, j, 0)))
    outs = pl.pallas_call(
        functools.partial(_combine_kernel, slots=slots, follow=follow, split=split),
        out_shape=tuple(out_shape),
        grid_spec=pltpu.PrefetchScalarGridSpec(
            num_scalar_prefetch=2, grid=(nbt,), in_specs=in_specs, out_specs=tuple(out_specs),
            scratch_shapes=[pltpu.VMEM((2, PICK_ROWS, d), F32), pltpu.VMEM((LANES, d), F32),
                            pltpu.VMEM((PICK_ROWS, LANES), jnp.int32), pltpu.SemaphoreType.DMA((2,))]),
        compiler_params=_params("arbitrary"), name="combine",
    )(*args)
    return outs if len(outs) > 1 else outs[0]


def _unpack_rows(words):
    lo = pltpu.bitcast(words << 16, F32)
    hi = pltpu.bitcast(words & jnp.uint32(0xFFFF0000), F32)
    return jnp.concatenate([lo, hi], axis=1).astype(BF16)


def _expert_weights(w_refs, w_f32, w_bf16, sems, layer):
    tn = w_f32.shape[2]
    n_col = pl.num_programs(1)
    _stationary_weights(
        w_refs, w_f32, w_bf16, sems, pl.program_id(0) * n_col + pl.program_id(1), pl.num_programs(0) * n_col,
        lambda ref, b: ref.at[layer, b // n_col, :, pl.ds(pl.multiple_of((b % n_col) * tn, tn), tn)], 2)


def _swiglu_kernel(a_ref, wg_ref, wu_ref, o_ref, w_f32, w_bf16, sems, *, layer):
    _expert_weights([wg_ref, wu_ref], w_f32, w_bf16, sems, layer)
    a = _unpack_rows(a_ref[...])
    gte = jnp.dot(a, w_bf16[0], preferred_element_type=F32)
    up = jnp.dot(a, w_bf16[1], preferred_element_type=F32)
    o_ref[...] = (gte * jax.nn.sigmoid(gte) * up).astype(o_ref.dtype)


def expert_swiglu(xe, w_gate, w_up, layer, *, tm=512, tn=512):
    e, c, _ = xe.shape
    k, f = w_gate.shape[-2:]
    tm, tn = _tile(c, tm), _tile(f, tn)
    w_spec = pl.BlockSpec(memory_space=pl.ANY)
    return pl.pallas_call(
        functools.partial(_swiglu_kernel, layer=layer),
        out_shape=jax.ShapeDtypeStruct((e, c, f), BF16),
        grid=(e, f // tn, c // tm),
        in_specs=[pl.BlockSpec((None, tm, k // 2), lambda x, j, i: (x, i, 0)), w_spec, w_spec],
        out_specs=pl.BlockSpec((None, tm, tn), lambda x, j, i: (x, i, j)),
        scratch_shapes=_weight_scratch(2, k, tn),
        compiler_params=_params("arbitrary", "arbitrary", "arbitrary"), name="expert_swiglu",
    )(xe, w_gate, w_up)


def _down_kernel(a_ref, w_ref, gate_ref, o_ref, w_f32, w_bf16, sems, *, layer):
    _expert_weights([w_ref], w_f32, w_bf16, sems, layer)
    acc = jnp.dot(a_ref[...], w_bf16[0], preferred_element_type=F32)
    o_ref[...] = acc * pltpu.bitcast(gate_ref[...], F32)[:, 0:1]


def expert_down(hid, w_down, xe, layer, *, tm=512, tn=2048):
    e, c, f = hid.shape
    d = w_down.shape[-1]
    tm, tn = _tile(c, tm), _tile(d, tn)
    gate_blk = (xe.shape[-1] - LANES) // LANES
    return pl.pallas_call(
        functools.partial(_down_kernel, layer=layer),
        out_shape=jax.ShapeDtypeStruct((e, c, d), F32),
        grid=(e, d // tn, c // tm),
        in_specs=[pl.BlockSpec((None, tm, f), lambda x, j, i: (x, i, 0)),
                  pl.BlockSpec(memory_space=pl.ANY),
                  pl.BlockSpec((None, tm, LANES), lambda x, j, i: (x, i, gate_blk))],
        out_specs=pl.BlockSpec((None, tm, tn), lambda x, j, i: (x, i, j)),
        scratch_shapes=_weight_scratch(1, f, tn),
        compiler_params=_params("arbitrary", "arbitrary", "arbitrary"), name="expert_down",
    )(hid, w_down, xe)


def _attn_kernel(q_ref, *rest, group, n_seg, chunk):
    kv_refs, o_ref = rest[:2 * n_seg], rest[2 * n_seg]
    tq = q_ref.shape[0]
    q = jnp.concatenate([q_ref[:, g * HEAD_DIM:(g + 1) * HEAD_DIM] for g in range(group)], axis=0)
    m = jnp.full((group * tq, 1), -jnp.inf, F32)
    acc = jnp.zeros((group * tq, 2 * HEAD_DIM), F32)
    for seg in range(n_seg):
        k_ref, v_ref = kv_refs[2 * seg], kv_refs[2 * seg + 1]
        length = k_ref.shape[0]
        step = math.gcd(length, chunk)
        for lo in range(0, length, step):
            k = k_ref[lo:lo + step, :].astype(BF16)
            v = v_ref[lo:lo + step, :].astype(BF16)
            v1 = jnp.concatenate([v, jnp.ones_like(v)], axis=1)
            s = lax.dot_general(q, k, (((1,), (1,)), ((), ())), preferred_element_type=F32)
            m_new = jnp.maximum(m, jnp.max(s, axis=-1, keepdims=True))
            p = jnp.exp2(s - m_new).astype(BF16)
            acc = jnp.exp2(m - m_new) * acc + jnp.dot(p, v1, preferred_element_type=F32)
            m = m_new
    o = acc[:, :HEAD_DIM] / acc[:, HEAD_DIM:HEAD_DIM + 1]
    for g in range(group):
        o_ref[:, g * HEAD_DIM:(g + 1) * HEAD_DIM] = o[g * tq:(g + 1) * tq].astype(o_ref.dtype)


def attention(q, segments, *, n_kv, n_req, q_row0, lq, tq, chunk=512):
    group = q.shape[-1] // HEAD_DIM // n_kv
    tq = _tile(lq, tq)
    qb0, nqb = q_row0 // tq, lq // tq
    qw = group * HEAD_DIM
    in_specs = [pl.BlockSpec((tq, qw), lambda r, h, t: (qb0 + r * nqb + t, h))]
    args = [q]
    for k_arr, v_arr, block, imap in segments:
        spec = pl.BlockSpec(block, lambda r, h, t, imap=imap: imap(r, h))
        in_specs += [spec, spec]
        args += [k_arr, v_arr]
    return pl.pallas_call(
        functools.partial(_attn_kernel, group=group, n_seg=len(segments), chunk=chunk),
        out_shape=jax.ShapeDtypeStruct((n_req * lq, q.shape[-1]), BF16),
        grid=(n_req, n_kv, nqb),
        in_specs=in_specs,
        out_specs=pl.BlockSpec((tq, qw), lambda r, h, t: (r * nqb + t, h)),
        compiler_params=_params("parallel", "parallel", "arbitrary"), name="attention",
    )(*args)


def _seq_dft_kernel(m_ref, u_ref, o_ref, f_scr):
    length = u_ref.shape[1]
    half = length // 2
    blk = math.gcd(half, MXU_COLS)
    n_blk = length // blk

    @pl.when(pl.program_id(2) == 0)
    def _():
        r = lax.broadcasted_iota(jnp.int32, (blk, blk), 0)
        c = lax.broadcasted_iota(jnp.int32, (blk, blk), 1)
        mirror = _ones_where((r >= 1) & (c == blk - r), BF16)
        first = _ones_where((r == 0) & (c == 0), BF16)
        row_id = lax.broadcasted_iota(jnp.int32, (blk, 1), 0)
        for b in range(half // blk):
            lo = b * blk
            for plane, sign in ((0, 1.0), (1, -1.0)):
                src = (n_blk - 1 - b) * blk
                mir = jnp.dot(mirror, u_ref[plane, src:src + blk, :], preferred_element_type=F32)
                if b >= 1:
                    mir += jnp.dot(first, u_ref[plane, src + blk:src + 2 * blk, :], preferred_element_type=F32)
                f = u_ref[plane, lo:lo + blk, :].astype(F32) + sign * mir
                if b == 0 and plane == 1:
                    f = jnp.where(row_id == 0, u_ref[0, half:half + 1, :].astype(F32), f)
                f_scr[plane, lo:lo + blk, :] = f.astype(BF16)

    acc = jnp.dot(m_ref[0], f_scr[0], preferred_element_type=F32)
    acc += jnp.dot(m_ref[1], f_scr[1], preferred_element_type=F32)
    o_ref[...] = acc.astype(o_ref.dtype)


def seq_dft(mats, u, *, n_req, row0, tm=512, tn=512):
    length, half = mats.shape[1:]
    d = u.shape[-1]
    tm, tn = _tile(length, tm), _tile(d, tn)
    rb0 = row0 // length
    return pl.pallas_call(
        _seq_dft_kernel,
        out_shape=jax.ShapeDtypeStruct((n_req * length, d), BF16),
        grid=(n_req, d // tn, length // tm),
        in_specs=[pl.BlockSpec((2, tm, half), lambda r, j, i: (0, i, 0)),
                  pl.BlockSpec((2, length, tn), lambda r, j, i: (0, rb0 + r, j))],
        out_specs=pl.BlockSpec((tm, tn), lambda r, j, i: (r * (length // tm) + i, j)),
        scratch_shapes=[pltpu.VMEM((2, half, tn), BF16)],
        compiler_params=_params("parallel", "parallel", "arbitrary"), name="seq_dft",
    )(mats, u)


def _seq_dft_mats(length):
    cos, sin = _dft_mats(length)
    half = length // 2
    return jnp.stack([cos[:, :half], (-sin[:, :half]).at[:, 0].set(cos[:, half])]).astype(BF16)


def _dft_mats(length):
    n2 = 1 << ((length.bit_length() - 1) // 2)
    n1 = length // n2
    assert n1 * n2 == length
    j = jnp.arange(length, dtype=jnp.int32)[:, None]
    ang_a = ((j * jnp.arange(n1, dtype=jnp.int32)[None, :]) % n1).astype(F32) * (2.0 * math.pi / n1)
    ang_b = ((j * jnp.arange(n2, dtype=jnp.int32)[None, :]) % length).astype(F32) * (2.0 * math.pi / length)
    ca, sa = jnp.cos(ang_a)[:, :, None], jnp.sin(ang_a)[:, :, None]
    cb, sb = jnp.cos(ang_b)[:, None, :], jnp.sin(ang_b)[:, None, :]
    norm = 1.0 / math.sqrt(length)
    cos = (ca * cb - sa * sb).reshape(length, length) * norm
    sin = (sa * cb + ca * sb).reshape(length, length) * norm
    return cos, sin


def _rope_tables(n_prompt, n_req, n_lat):
    half = HEAD_DIM // 2
    rows = n_lat // GRID_W
    row_pos = jnp.repeat(jnp.arange(rows, dtype=F32), GRID_W)
    col_pos = jnp.tile(jnp.arange(GRID_W, dtype=F32), rows)
    inv_freq = ROPE_THETA ** (-jnp.arange(0, half, 2, dtype=F32) / half)

    def cs(pos):
        ang = pos[:, None] * inv_freq[None, :]
        return jnp.cos(ang), jnp.sin(ang)

    cr, sr = cs(row_pos)
    cc, sc = cs(col_pos)
    cos = jnp.concatenate([cr, cr, cc, cc], axis=-1)
    sin = jnp.concatenate([-sr, sr, -sc, sc], axis=-1)
    cos = jnp.concatenate([jnp.ones((n_prompt, HEAD_DIM), F32), jnp.tile(cos, (n_req, 1))], axis=0)
    sin = jnp.concatenate([jnp.zeros((n_prompt, HEAD_DIM), F32), jnp.tile(sin, (n_req, 1))], axis=0)
    return cos, sin


def kernel(x_prompt, x_sample, cache_k, cache_v, c, c_ctx, norm_g, w_mod, b_mod, w_q, w_k, w_v, w_o,
           q_norm_g, k_norm_g, w_fourier, w_router, w_gate, w_up, w_down, final_norm_g):
    batch, seq, d = x_prompt.shape
    n_req, n_lat, _ = x_sample.shape
    depth = w_mod.shape[0]
    n_kv = cache_k.shape[3]
    n_exp = w_router.shape[-1]
    n_prompt = batch * seq
    nt = n_prompt + n_req * n_lat
    assert n_req + 1 <= COND_ROWS
    assert n_prompt % n_lat == 0

    def cond_row(r):
        return jnp.where(r < n_prompt, 0, 1 + (r - n_prompt) // n_lat)

    cond = jnp.zeros((COND_ROWS, d), F32).at[0].set(c_ctx).at[1:1 + n_req].set(c)
    mod = modulation_all(cond, w_mod, b_mod)
    cos, sin = _rope_tables(n_prompt, n_req, n_lat)
    scale = math.log2(math.e) / math.sqrt(HEAD_DIM)

    cc, sc = _dft_mats(FGROUP_DIM)
    cs_chan = jnp.concatenate([cc, sc], axis=1).astype(BF16)
    mats_p = mats_s = None
    if depth > 1:
        mats_p, mats_s = _seq_dft_mats(seq), _seq_dft_mats(n_lat)

    x = (x_prompt.reshape(n_prompt, d), x_sample.reshape(n_req * n_lat, d))
    assert n_prompt % LANES == 0 and (n_req * n_lat) % LANES == 0
    cap_p = (CAPACITY_FACTOR * n_prompt) // n_exp
    cap_s = (CAPACITY_FACTOR * n_req * n_lat) // n_exp
    groups = ((0, n_prompt // LANES, cap_p, 0), (n_prompt // LANES, n_req * n_lat // LANES, cap_s, cap_p))
    slots = cap_p + cap_s
    assert cap_p % SUBLANES == 0 and cap_s % SUBLANES == 0
    new_k, new_v = [], []
    mixer_in = None
    for i in range(depth):
        mod_l = mod[i].reshape(COND_ROWS * N_MOD, 1, d)
        j = i // 2
        if i % 2 == 0:
            h = mixer_in if i > 0 else ada_norm(x, norm_g[i, 0], mod_l, 0, 1, cond_row)
            q = project(h, w_q, j, out_dtypes=(BF16,),
                        head=(q_norm_g[j].reshape(1, HEAD_DIM) * scale, cos, sin))
            kf, kb = project(h, w_k, j, out_dtypes=(F32, BF16),
                             head=(k_norm_g[j].reshape(1, HEAD_DIM), cos, sin))
            vf, vb = project(h, w_v, j, out_dtypes=(F32, BF16))
            kvw = n_kv * HEAD_DIM
            new_k.append(kf[:n_prompt].reshape(batch, seq, n_kv, HEAD_DIM))
            new_v.append(vf[:n_prompt].reshape(batch, seq, n_kv, HEAD_DIM))
            att_p = attention(q, [(kb, vb, (seq, HEAD_DIM), lambda r, h: (r, h))],
                              n_kv=n_kv, n_req=batch, q_row0=0, lq=seq, tq=256)
            past = cache_k.shape[2]
            lat0 = n_prompt // n_lat
            att_s = attention(
                q,
                [(cache_k.reshape(n_req, -1, past, kvw), cache_v.reshape(n_req, -1, past, kvw),
                  (None, None, past, HEAD_DIM), lambda r, h, j=j: (r, j, 0, h)),
                 (kb, vb, (n_lat, HEAD_DIM), lambda r, h: (lat0 + r, h))],
                n_kv=n_kv, n_req=n_req, q_row0=n_prompt, lq=n_lat, tq=256)
            x = project((att_p, att_s), w_o, j, out_dtypes=(F32,), residual=(x, mod_l, 2, cond_row))
        else:
            u = mixer_in if i > 0 else ada_norm(x, norm_g[i, 0], mod_l, 0, 1, cond_row, mode="dft", cs=cs_chan)
            f_p = seq_dft(mats_p, u, n_req=batch, row0=0, tn=d)
            f_s = seq_dft(mats_s, u, n_req=n_req, row0=n_prompt)
            x = project((f_p, f_s), w_fourier, j, out_dtypes=(F32,), residual=(x, mod_l, 2, cond_row))

        h, aff = ada_norm(x, norm_g[i, 1], mod_l, 3, 4, cond_row, mode="router",
                          w_router_t=w_router[i].T)
        pick, tot, offs = route_select(aff, groups)
        xe = dispatch(h, pick, aff, tot, offs, slots)
        hid = expert_swiglu(xe, w_gate, w_up, i)
        ye = expert_down(hid, w_down, xe, i)
        if i + 1 < depth:
            follow = "plain" if (i + 1) % 2 == 0 else "dft"
            x, mixer_in = combine(x, ye, pick, tot, offs, mod_l, 5, cond_row, follow=follow, g=norm_g[i + 1, 0],
                                  mod_next=mod[i + 1].reshape(COND_ROWS * N_MOD, 1, d),
                                  cs=cs_chan if follow == "dft" else None)
        else:
            y_p, y_s = combine(x, ye, pick, tot, offs, mod_l, 5, cond_row, follow="final", g=final_norm_g,
                               first_rows=n_prompt)

    return (y_p.reshape(batch, seq, d), y_s.reshape(n_req, n_lat, d),
            jnp.stack(new_k, axis=1), jnp.stack(new_v, axis=1))
```

```python
import functools
import math

import jax
import jax.numpy as jnp
from jax import lax
from jax.experimental import pallas as pl
from jax.experimental.pallas import tpu as pltpu

F32 = jnp.float32
BF16 = jnp.bfloat16

GRID_W = 64
HEAD_DIM = 128
FGROUP_DIM = 128
ROPE_THETA = 10000.0
CAPACITY_FACTOR = 2
N_MOD = 6
EPS = 1e-6

LANES = 128
SUBLANES = 8
MXU_COLS = 256
COND_ROWS = 8
VMEM_LIMIT_BYTES = 56 * 1024 * 1024


def _params(*sem):
    return pltpu.CompilerParams(dimension_semantics=sem, vmem_limit_bytes=VMEM_LIMIT_BYTES)


def _tile(n, want):
    return math.gcd(n, want)


def _mod_kernel(c_ref, w_ref, b_ref, o_ref):
    c = c_ref[...]
    a = (c * jax.nn.sigmoid(c)).astype(BF16)
    o_ref[...] = jnp.dot(a, w_ref[...].astype(BF16), preferred_element_type=F32) + b_ref[...]


def modulation_all(cond, w_mod, b_mod):
    depth, d, n = w_mod.shape
    tn = _tile(n, 512)
    return pl.pallas_call(
        _mod_kernel,
        out_shape=jax.ShapeDtypeStruct((depth, COND_ROWS, n), F32),
        grid=(depth, n // tn),
        in_specs=[
            pl.BlockSpec((COND_ROWS, d), lambda l, j: (0, 0)),
            pl.BlockSpec((None, d, tn), lambda l, j: (l, 0, j)),
            pl.BlockSpec((None, 1, tn), lambda l, j: (l, 0, j)),
        ],
        out_specs=pl.BlockSpec((None, COND_ROWS, tn), lambda l, j: (l, 0, j)),
        compiler_params=_params("parallel", "parallel"),
        name="modulation",
    )(cond, w_mod, b_mod.reshape(depth, 1, n))


def _rms(x, g):
    return x * lax.rsqrt(jnp.mean(x * x, axis=-1, keepdims=True) + EPS) * g


def _split_rows(specs_of, first, rest_rows, tm):
    split = first.shape[0] // tm
    assert split * tm == first.shape[0] and rest_rows.shape[0] % tm == 0
    return (specs_of(lambda i: jnp.minimum(i, split - 1)), specs_of(lambda i: jnp.maximum(i - split, 0))), split


def _ada_kernel(*refs, mode, split):
    if split is None:
        x_ref, g_ref, sh_ref, sc_ref, *rest = refs
        _ada_apply(x_ref[...], g_ref, sh_ref, sc_ref, rest, mode)
        return
    x0_ref, x1_ref, g_ref, sh_ref, sc_ref, *rest = refs

    @pl.when(pl.program_id(0) < split)
    def _():
        _ada_apply(x0_ref[...], g_ref, sh_ref, sc_ref, rest, mode)

    @pl.when(pl.program_id(0) >= split)
    def _():
        _ada_apply(x1_ref[...], g_ref, sh_ref, sc_ref, rest, mode)


def _ada_apply(x, g_ref, sh_ref, sc_ref, rest, mode):
    h = _rms(x, g_ref[...]) * (1.0 + sc_ref[...]) + sh_ref[...]
    hb = h.astype(BF16)
    if mode == "plain":
        (h_ref,) = rest
        h_ref[...] = hb
    elif mode == "router":
        wr_ref, h_ref, aff_ref = rest
        h_ref[...] = hb
        logits = lax.dot_general(wr_ref[...].astype(BF16), hb, (((1,), (1,)), ((), ())),
                                 preferred_element_type=F32)
        m = jnp.max(logits, axis=0, keepdims=True)
        p = jnp.exp(logits - m)
        aff = p / jnp.sum(p, axis=0, keepdims=True)
        for blk in range(aff_ref.shape[0]):
            aff_ref[blk] = aff[:, blk * LANES:(blk + 1) * LANES]
    else:
        cs_ref, u_ref = rest
        n_groups = hb.shape[1] // FGROUP_DIM
        for grp in range(n_groups):
            lo = grp * FGROUP_DIM
            r = jnp.dot(hb[:, lo:lo + FGROUP_DIM], cs_ref[...], preferred_element_type=F32)
            u_ref[0, :, lo:lo + FGROUP_DIM] = r[:, :FGROUP_DIM].astype(BF16)
            u_ref[1, :, lo:lo + FGROUP_DIM] = r[:, FGROUP_DIM:].astype(BF16)


def ada_norm(x, g, mod_l, shift_idx, scale_idx, cond_row, *, mode="plain", w_router_t=None, cs=None):
    row = lambda i: (i, 0)
    if isinstance(x, tuple):
        d = x[0].shape[1]
        nt = x[0].shape[0] + x[1].shape[0]
        tm = _tile(math.gcd(x[0].shape[0], x[1].shape[0]), 256)
        x_specs, split = _split_rows(lambda blk: pl.BlockSpec((tm, d), lambda i: (blk(i), 0)), x[0], x[1], tm)
        x_specs, x_args = list(x_specs), list(x)
    else:
        nt, d = x.shape
        tm = _tile(nt, 256)
        x_specs, x_args, split = [pl.BlockSpec((tm, d), row)], [x], None
    grid = (nt // tm,)
    in_specs = x_specs + [
        pl.BlockSpec((1, d), lambda i: (0, 0)),
        pl.BlockSpec((None, 1, d), lambda i: (cond_row(i * tm) * N_MOD + shift_idx, 0, 0)),
        pl.BlockSpec((None, 1, d), lambda i: (cond_row(i * tm) * N_MOD + scale_idx, 0, 0)),
    ]
    args = x_args + [g.reshape(1, d), mod_l, mod_l]
    if mode == "plain":
        out_shape = jax.ShapeDtypeStruct((nt, d), BF16)
        out_specs = pl.BlockSpec((tm, d), row)
    elif mode == "router":
        e = w_router_t.shape[0]
        in_specs.append(pl.BlockSpec((e, d), lambda i: (0, 0)))
        args.append(w_router_t)
        out_shape = (jax.ShapeDtypeStruct((nt, d), BF16), jax.ShapeDtypeStruct((nt // LANES, e, LANES), F32))
        out_specs = (pl.BlockSpec((tm, d), row), pl.BlockSpec((tm // LANES, e, LANES), lambda i: (i, 0, 0)))
    else:
        in_specs.append(pl.BlockSpec(cs.shape, lambda i: (0, 0)))
        args.append(cs)
        out_shape = jax.ShapeDtypeStruct((2, nt, d), BF16)
        out_specs = pl.BlockSpec((2, tm, d), lambda i: (0, i, 0))
    return pl.pallas_call(
        functools.partial(_ada_kernel, mode=mode, split=split),
        out_shape=out_shape, grid=grid, in_specs=in_specs, out_specs=out_specs,
        compiler_params=_params("parallel"), name="ada_norm_" + mode,
    )(*args)


def _stationary_weights(w_hbm, w_f32, w_bf16, sems, block, n_blocks, slicer, inner_axis):
    def copy(b, which):
        return pltpu.make_async_copy(slicer(w_hbm[which], b), w_f32.at[which], sems.at[which])

    @pl.when(pl.program_id(inner_axis) == 0)
    def _():
        @pl.when(block == 0)
        def _():
            for which in range(len(w_hbm)):
                copy(block, which).start()

        for which in range(len(w_hbm)):
            copy(block, which).wait()
            w_bf16[which] = w_f32[which].astype(BF16)

        @pl.when(block + 1 < n_blocks)
        def _():
            for which in range(len(w_hbm)):
                copy(block + 1, which).start()


def _weight_scratch(n_w, k, tn):
    return [pltpu.VMEM((n_w, k, tn), F32), pltpu.VMEM((n_w, k, tn), BF16), pltpu.SemaphoreType.DMA((n_w,))]


def _project_weights(w_ref, w_f32, w_bf16, sems, layer):
    tn = w_f32.shape[2]
    _stationary_weights(
        [w_ref], w_f32, w_bf16, sems, pl.program_id(0), pl.num_programs(0),
        lambda ref, b: ref.at[layer, :, pl.ds(pl.multiple_of(b * tn, tn), tn)], 1)


def _mm_plain_kernel(a_ref, w_ref, *rest, n_out, layer):
    outs, (w_f32, w_bf16, sems) = rest[:n_out], rest[n_out:]
    _project_weights(w_ref, w_f32, w_bf16, sems, layer)
    acc = jnp.dot(a_ref[...], w_bf16[0], preferred_element_type=F32)
    for o_ref in outs:
        o_ref[...] = acc.astype(o_ref.dtype)


def _swap32(y):
    lane = lax.broadcasted_iota(jnp.int32, y.shape, 1)
    first = (lane % 64) < 32
    return jnp.where(first, pltpu.roll(y, 96, 1), pltpu.roll(y, 32, 1))


def _mm_headnorm_kernel(a_ref, w_ref, g_ref, cos_ref, sin_ref, *rest, n_out, layer):
    outs, (w_f32, w_bf16, sems) = rest[:n_out], rest[n_out:]
    _project_weights(w_ref, w_f32, w_bf16, sems, layer)
    a = a_ref[...]
    cos, sin = cos_ref[...], sin_ref[...]
    tn = w_bf16.shape[2]
    step = math.gcd(tn, MXU_COLS)
    for c0 in range(0, tn, step):
        acc = jnp.dot(a, w_bf16[0, :, c0:c0 + step], preferred_element_type=F32)
        for lo in range(0, step, HEAD_DIM):
            y = _rms(acc[:, lo:lo + HEAD_DIM], g_ref[...])
            y = y * cos + _swap32(y) * sin
            for o_ref in outs:
                o_ref[:, c0 + lo:c0 + lo + HEAD_DIM] = y.astype(o_ref.dtype)


def _mm_residual_kernel(a0_ref, a1_ref, w_ref, *rest, layer, split, res_split):
    if res_split:
        res0_ref, res1_ref, gate_ref, o_ref, w_f32, w_bf16, sems = rest
    else:
        res0_ref, gate_ref, o_ref, w_f32, w_bf16, sems = rest
        res1_ref = res0_ref
    _project_weights(w_ref, w_f32, w_bf16, sems, layer)

    def emit(a_ref, res_ref):
        acc = jnp.dot(a_ref[...], w_bf16[0], preferred_element_type=F32)
        o_ref[...] = res_ref[...] + gate_ref[...] * acc

    @pl.when(pl.program_id(1) < split)
    def _():
        emit(a0_ref, res0_ref)

    @pl.when(pl.program_id(1) >= split)
    def _():
        emit(a1_ref, res1_ref)


def project(a, w_stack, layer, *, out_dtypes, head=None, residual=None, tm=512, tn=1024):
    n = w_stack.shape[-1]
    if residual is not None:
        a0, a1 = a
        k = a0.shape[1]
        m = a0.shape[0] + a1.shape[0]
        tm, tn = _tile(math.gcd(a0.shape[0], a1.shape[0]), tm), _tile(n, tn)
        split = a0.shape[0] // tm
        in_specs = [pl.BlockSpec((tm, k), lambda j, i: (jnp.minimum(i, split - 1), 0)),
                    pl.BlockSpec((tm, k), lambda j, i: (jnp.maximum(i - split, 0), 0)),
                    pl.BlockSpec(memory_space=pl.ANY)]
        args = [a0, a1, w_stack]
    else:
        m, k = a.shape
        tm, tn = _tile(m, tm), _tile(n, tn)
        in_specs = [pl.BlockSpec((tm, k), lambda j, i: (i, 0)), pl.BlockSpec(memory_space=pl.ANY)]
        args = [a, w_stack]
    grid = (n // tn, m // tm)
    out_block = pl.BlockSpec((tm, tn), lambda j, i: (i, j))
    if head is not None:
        g, cos, sin = head
        in_specs += [pl.BlockSpec((1, HEAD_DIM), lambda j, i: (0, 0)),
                     pl.BlockSpec((tm, HEAD_DIM), lambda j, i: (i, 0)),
                     pl.BlockSpec((tm, HEAD_DIM), lambda j, i: (i, 0))]
        args += [g, cos, sin]
        body = functools.partial(_mm_headnorm_kernel, n_out=len(out_dtypes), layer=layer)
    elif residual is not None:
        res, mod_l, gate_idx, cond_row = residual
        if isinstance(res, tuple):
            assert res[0].shape[0] == a0.shape[0]
            in_specs += [pl.BlockSpec((tm, tn), lambda j, i: (jnp.minimum(i, split - 1), j)),
                         pl.BlockSpec((tm, tn), lambda j, i: (jnp.maximum(i - split, 0), j))]
            args += list(res)
        else:
            in_specs.append(pl.BlockSpec((tm, tn), lambda j, i: (i, j)))
            args.append(res)
        in_specs.append(pl.BlockSpec((None, 1, tn), lambda j, i: (cond_row(i * tm) * N_MOD + gate_idx, 0, j)))
        args.append(mod_l)
        body = functools.partial(_mm_residual_kernel, layer=layer, split=split, res_split=isinstance(res, tuple))
    else:
        body = functools.partial(_mm_plain_kernel, n_out=len(out_dtypes), layer=layer)
    outs = pl.pallas_call(
        body,
        out_shape=tuple(jax.ShapeDtypeStruct((m, n), dt) for dt in out_dtypes),
        grid=grid, in_specs=in_specs, out_specs=tuple(out_block for _ in out_dtypes),
        scratch_shapes=_weight_scratch(1, k, tn),
        compiler_params=_params("arbitrary", "arbitrary"), name="project",
    )(*args)
    return outs if len(outs) > 1 else outs[0]


PICK_ROWS = 512


def _ones_where(mask, dtype):
    return jnp.where(mask, 1.0, 0.0).astype(dtype)


def _select_kernel(aff_ref, pick_ref, tot_ref, offs_ref, w_scr, o_scr, *, groups):
    n_exp = aff_ref.shape[1]
    li = lax.broadcasted_iota(jnp.int32, (LANES, LANES), 0)
    lj = lax.broadcasted_iota(jnp.int32, (LANES, LANES), 1)
    tri = _ones_where(li <= lj, BF16)

    def chunk_cumsum(mask, nb):
        w = jnp.dot(_ones_where(mask, BF16).reshape(nb * n_exp, LANES), tri,
                    preferred_element_type=F32).reshape(nb, n_exp, LANES)
        w_scr[0:nb] = w

        def body(j, run):
            o_scr[j] = jnp.broadcast_to(run, (n_exp, LANES))
            return run + w_scr[j][:, LANES - 1:LANES]

        lax.fori_loop(0, nb, body, jnp.zeros((n_exp, 1), F32))
        return w, o_scr[0:nb]

    for j0, nb, cap, slot0 in groups:
        keys = pltpu.bitcast(aff_ref[j0:j0 + nb], jnp.int32)

        def count(mask):
            return jnp.sum(jnp.sum(_ones_where(mask, F32), axis=0), axis=1, keepdims=True)

        def search(it, prefix):
            cand = prefix | lax.shift_left(jnp.int32(1), 30 - it)
            return jnp.where(count(keys >= cand[None]) >= cap, cand, prefix)

        thr = lax.fori_loop(0, 31, search, jnp.zeros((n_exp, 1), jnp.int32))[None]
        above, ties = keys > thr, keys == thr
        need = (cap - count(above))[None]
        w_eq, o_eq = chunk_cumsum(ties, nb)
        sel = above | (ties & (w_eq + o_eq <= need))
        w, offs = chunk_cumsum(sel, nb)
        tot = jnp.broadcast_to(w[:, :, LANES - 1:LANES], w.shape)
        lead = offs - SUBLANES * jnp.floor(offs / SUBLANES)
        seg = SUBLANES * jnp.floor((lead + tot + (SUBLANES - 1)) / SUBLANES)
        rows = nb * n_exp
        blk = math.gcd(rows, 256)
        bi = lax.broadcasted_iota(jnp.int32, (blk, blk), 0)
        bj = lax.broadcasted_iota(jnp.int32, (blk, blk), 1)
        lower = _ones_where((bi // n_exp == bj // n_exp) & (bj % n_exp < bi % n_exp), BF16)
        seg2 = seg.reshape(rows, LANES).astype(BF16)
        base = jnp.concatenate(
            [jnp.dot(lower, seg2[r0:r0 + blk], preferred_element_type=F32) for r0 in range(0, rows, blk)],
            axis=0).reshape(nb, n_exp, LANES)
        pick_ref[j0:j0 + nb] = jnp.where(sel, base + lead + w - 1.0, -1.0).astype(jnp.int32)
        tot_ref[j0:j0 + nb] = tot.astype(jnp.int32)
        offs_ref[j0:j0 + nb] = offs.astype(jnp.int32) + slot0


def route_select(aff, groups):
    nbt, e, _ = aff.shape
    shape = jax.ShapeDtypeStruct(aff.shape, jnp.int32)
    max_nb = max(g[1] for g in groups)
    pick, tot, offs = pl.pallas_call(
        functools.partial(_select_kernel, groups=groups),
        out_shape=(shape, shape, shape),
        scratch_shapes=[pltpu.VMEM((max_nb, e, LANES), F32), pltpu.VMEM((max_nb, e, LANES), F32)],
        compiler_params=pltpu.CompilerParams(vmem_limit_bytes=VMEM_LIMIT_BYTES), name="route_select",
    )(aff)
    return pick, tot[:, :, 0].reshape(nbt * e), offs[:, :, 0].reshape(nbt * e)


def _sub_blocks():
    first = (3 * PICK_ROWS // 4) // 16 * 16
    return ((0, first), (first, PICK_ROWS - first))


def _fill_owner_rows(tot_ref, offs_ref, j, n_exp, row0, pairs):
    def per_expert(e, base):
        _, _, seg = _segment(tot_ref, offs_ref, j, n_exp, e)
        lo = jnp.maximum(base, row0)
        hi = jnp.minimum(base + seg, row0 + PICK_ROWS)
        vals = [jnp.broadcast_to(src[pl.ds(e, 1), :], (SUBLANES, LANES)) for src, _ in pairs]

        def per_group(g, carry):
            row = pl.multiple_of(lo - row0 + g * SUBLANES, SUBLANES)
            for val, (_, dst) in zip(vals, pairs):
                dst[pl.ds(row, SUBLANES), :] = val
            return carry

        lax.fori_loop(0, jnp.maximum(hi - lo, 0) // SUBLANES, per_group, 0)
        return base + seg

    lax.fori_loop(0, n_exp, per_expert, jnp.int32(0))


def _segment(tot_ref, offs_ref, j, n_exp, e):
    t = tot_ref[j * n_exp + e]
    o = offs_ref[j * n_exp + e]
    r = o & (SUBLANES - 1)
    return o, r, ((r + t + SUBLANES - 1) // SUBLANES) * SUBLANES


def _for_each_piece(tot_ref, offs_ref, j, n_exp, row0, rows, fn):
    max_groups = (SUBLANES - 1 + LANES + SUBLANES - 1) // SUBLANES

    def per_expert(e, base):
        o, r, seg = _segment(tot_ref, offs_ref, j, n_exp, e)
        lo = jnp.maximum(base, row0)
        hi = jnp.minimum(base + seg, row0 + rows)
        n = jnp.maximum(hi - lo, 0)
        slot = (o - r) + (lo - base)
        done = jnp.int32(0)
        for bit in range(max_groups.bit_length() - 1, -1, -1):
            size = SUBLANES << bit

            @pl.when((n & size) != 0)
            def _():
                fn(e, pl.multiple_of(slot + done, SUBLANES), pl.multiple_of(lo - row0 + done, SUBLANES), size)

            done = done + (n & size)
        return base + seg

    lax.fori_loop(0, n_exp, per_expert, jnp.int32(0))


def _staging_rows(tot_ref, offs_ref, j, n_exp):
    return lax.fori_loop(0, n_exp, lambda e, s: s + _segment(tot_ref, offs_ref, j, n_exp, e)[2], jnp.int32(0))


def _rounds(tot_ref, offs_ref, j, n_exp):
    return (_staging_rows(tot_ref, offs_ref, j, n_exp) + PICK_ROWS - 1) // PICK_ROWS


def _dispatch_kernel(tot_ref, offs_ref, h_ref, pick_ref, aff_ref, xe_ref, stage2, tail, owner_pick, owner_aff, sems):
    j = pl.program_id(0)
    buf = j % 2
    stage = stage2.at[buf]
    n_exp = pick_ref.shape[0]
    half = h_ref.shape[1] // 2
    total = _staging_rows(tot_ref, offs_ref, j, n_exp)
    n_rounds = (total + PICK_ROWS - 1) // PICK_ROWS

    @pl.when(j == 0)
    def _():
        tail[...] = jnp.zeros_like(tail)
        owner_aff[...] = jnp.zeros_like(owner_aff)

    def copy(b, e, slot, row, size):
        return pltpu.make_async_copy(stage2.at[b, pl.ds(row, size)], xe_ref.at[e, pl.ds(slot, size)], sems.at[b])

    def start_round(jj, r, b):
        _for_each_piece(tot_ref, offs_ref, jj, n_exp, r * PICK_ROWS, PICK_ROWS,
                        lambda e, slot, row, size: copy(b, e, slot, row, size).start())

    def wait_round(jj, r, b):
        _for_each_piece(tot_ref, offs_ref, jj, n_exp, r * PICK_ROWS, PICK_ROWS,
                        lambda e, slot, row, size: copy(b, e, slot, row, size).wait())

    def splice_tails(row0):
        def per_expert(e, base):
            _, r, seg = _segment(tot_ref, offs_ref, j, n_exp, e)
            last = base + seg - SUBLANES

            @pl.when((r > 0) & (base >= row0) & (base < row0 + PICK_ROWS))
            def _():
                row = pl.multiple_of(base - row0, SUBLANES)
                keep = lax.broadcasted_iota(jnp.int32, (SUBLANES, 1), 0) < r
                stage[pl.ds(row, SUBLANES), :] = jnp.where(keep, tail[e], stage[pl.ds(row, SUBLANES), :])

            @pl.when((seg > 0) & (last >= row0) & (last < row0 + PICK_ROWS))
            def _():
                tail[e] = stage[pl.ds(pl.multiple_of(last - row0, SUBLANES), SUBLANES), :]

            return base + seg

        lax.fori_loop(0, n_exp, per_expert, jnp.int32(0))

    def build(r):
        row0 = r * PICK_ROWS
        owner_pick[...] = jnp.full(owner_pick.shape, -1, jnp.int32)
        _fill_owner_rows(tot_ref, offs_ref, j, n_exp, row0, ((pick_ref, owner_pick), (aff_ref, owner_aff)))
        for b0, n in _sub_blocks():
            @pl.when(total - row0 > b0)
            def _():
                ids = lax.broadcasted_iota(jnp.int32, (n, LANES), 0) + (row0 + b0)
                hit = ids == owner_pick[b0:b0 + n, :]
                rows = jnp.dot(_ones_where(hit, BF16), h_ref[...], preferred_element_type=F32)
                bits = pltpu.bitcast(rows, jnp.uint32)
                stage[b0:b0 + n, 0:half] = bits[:, half:] | (bits[:, :half] >> 16)
                g = jnp.sum(jnp.where(hit, owner_aff[b0:b0 + n, :], 0.0), axis=1, keepdims=True)
                stage[b0:b0 + n, half:half + LANES] = pltpu.bitcast(jnp.broadcast_to(g, (n, LANES)), jnp.uint32)
        splice_tails(row0)

    @pl.when(n_rounds > 0)
    def _():
        build(0)

    @pl.when(j > 0)
    def _():
        prev_rounds = _rounds(tot_ref, offs_ref, j - 1, n_exp)

        @pl.when(prev_rounds > 0)
        def _():
            wait_round(j - 1, prev_rounds - 1, 1 - buf)

    @pl.when(n_rounds > 0)
    def _():
        start_round(j, 0, buf)

    def later_round(r, carry):
        wait_round(j, r - 1, buf)
        build(r)
        start_round(j, r, buf)
        return carry

    lax.fori_loop(1, n_rounds, later_round, 0)

    @pl.when((j == pl.num_programs(0) - 1) & (n_rounds > 0))
    def _():
        wait_round(j, n_rounds - 1, buf)


def dispatch(h, pick, aff, tot, offs, slots):
    nt, d = h.shape
    nbt, e, _ = pick.shape
    width = d // 2 + LANES
    chunk = pl.BlockSpec((None, e, LANES), lambda j, *_: (j, 0, 0))
    return pl.pallas_call(
        _dispatch_kernel,
        out_shape=jax.ShapeDtypeStruct((e, slots, width), jnp.uint32),
        grid_spec=pltpu.PrefetchScalarGridSpec(
            num_scalar_prefetch=2, grid=(nbt,),
            in_specs=[pl.BlockSpec((LANES, d), lambda j, *_: (j, 0)), chunk, chunk],
            out_specs=pl.BlockSpec(memory_space=pl.ANY),
            scratch_shapes=[pltpu.VMEM((2, PICK_ROWS, width), jnp.uint32),
                            pltpu.VMEM((e, SUBLANES, width), jnp.uint32),
                            pltpu.VMEM((PICK_ROWS, LANES), jnp.int32), pltpu.VMEM((PICK_ROWS, LANES), F32),
                            pltpu.SemaphoreType.DMA((2,))]),
        compiler_params=_params("arbitrary"), name="dispatch",
    )(tot, offs, h, pick, aff)


def _combine_kernel(tot_ref, offs_ref, x_ref, pick_ref, gate_ref, ye_ref, *rest, slots, follow, split):
    n_in = {None: 0, "final": 1, "plain": 3, "dft": 4}[follow]
    follow_in, (o_ref, *follow_out), (stage2, acc_ref, owner_pick, sems) = rest[:n_in], rest[n_in:-4], rest[-4:]
    j = pl.program_id(0)
    buf = j % 2
    n_exp = pick_ref.shape[0]
    total = _staging_rows(tot_ref, offs_ref, j, n_exp)
    n_rounds = (total + PICK_ROWS - 1) // PICK_ROWS

    def copy(b, e, slot, row, size):
        return pltpu.make_async_copy(ye_ref.at[pl.ds(e * slots + slot, size)],
                                     stage2.at[b, pl.ds(row, size)], sems.at[b])

    def start_round(jj, r, b):
        _for_each_piece(tot_ref, offs_ref, jj, n_exp, r * PICK_ROWS, PICK_ROWS,
                        lambda e, slot, row, size: copy(b, e, slot, row, size).start())

    def wait_round(jj, r, b):
        _for_each_piece(tot_ref, offs_ref, jj, n_exp, r * PICK_ROWS, PICK_ROWS,
                        lambda e, slot, row, size: copy(b, e, slot, row, size).wait())

    @pl.when((j == 0) & (n_rounds > 0))
    def _():
        start_round(0, 0, 0)

    @pl.when(j + 1 < pl.num_programs(0))
    def _():
        @pl.when(_rounds(tot_ref, offs_ref, j + 1, n_exp) > 0)
        def _():
            start_round(j + 1, 0, 1 - buf)

    acc_ref[...] = jnp.zeros_like(acc_ref)

    def one_round(r, carry):
        row0 = r * PICK_ROWS

        @pl.when(r > 0)
        def _():
            start_round(j, r, buf)

        wait_round(j, r, buf)
        owner_pick[...] = jnp.full(owner_pick.shape, -1, jnp.int32)
        _fill_owner_rows(tot_ref, offs_ref, j, n_exp, row0, ((pick_ref, owner_pick),))
        for b0, n in _sub_blocks():
            @pl.when(total - row0 > b0)
            def _():
                ids = lax.broadcasted_iota(jnp.int32, (n, LANES), 0) + (row0 + b0)
                onehot = _ones_where(ids == owner_pick[b0:b0 + n, :], F32).T.astype(BF16)
                live = ids[:, 0:1] < total
                y = jnp.where(live, stage2[buf, b0:b0 + n, :], 0.0)
                acc = acc_ref[...]
                for _ in range(2):
                    part = y.astype(BF16)
                    acc = acc + jnp.dot(onehot, part, preferred_element_type=F32)
                    y = y - part.astype(F32)
                acc_ref[...] = acc
        return carry

    lax.fori_loop(0, n_rounds, one_round, 0)
    x_new = x_ref[...] + gate_ref[...] * acc_ref[...]
    if follow == "final":
        y_out = _rms(x_new, follow_in[0][...])

        @pl.when(j < split)
        def _():
            o_ref[...] = y_out

        @pl.when(j >= split)
        def _():
            follow_out[0][...] = y_out
    else:
        o_ref[...] = x_new
        if follow is not None:
            g_ref, sh_ref, sc_ref = follow_in[:3]
            _ada_apply(x_new, g_ref, sh_ref, sc_ref, tuple(follow_in[3:]) + tuple(follow_out), follow)


def combine(x, ye, pick, tot, offs, mod_l, gate_idx, cond_row, *, follow=None, g=None, mod_next=None, cs=None,
            first_rows=None):
    nt, d = x.shape
    nbt, e, _ = pick.shape
    slots = ye.shape[1]
    rows = pl.BlockSpec((LANES, d), lambda j, *_: (j, 0))
    in_specs = [rows,
                pl.BlockSpec((None, e, LANES), lambda j, *_: (j, 0, 0)),
                pl.BlockSpec((None, 1, d), lambda j, *_: (cond_row(j * LANES) * N_MOD + gate_idx, 0, 0)),
                pl.BlockSpec(memory_space=pl.ANY)]
    args = [tot, offs, x, pick, mod_l, ye.reshape(e * slots, d)]
    out_shape, out_specs, split = [jax.ShapeDtypeStruct((nt, d), F32)], [rows], None
    if follow is not None:
        in_specs.append(pl.BlockSpec((1, d), lambda j, *_: (0, 0)))
        args.append(g.reshape(1, d))
    if follow == "final":
        split = first_rows // LANES
        assert split * LANES == first_rows
        out_shape = [jax.ShapeDtypeStruct((first_rows, d), F32), jax.ShapeDtypeStruct((nt - first_rows, d), F32)]
        out_specs = [pl.BlockSpec((LANES, d), lambda j, *_: (jnp.minimum(j, split - 1), 0)),
                     pl.BlockSpec((LANES, d), lambda j, *_: (jnp.maximum(j - split, 0), 0))]
    if follow in ("plain", "dft"):
        in_specs += [pl.BlockSpec((None, 1, d), lambda j, *_: (cond_row(j * LANES) * N_MOD + 0, 0, 0)),
                     pl.BlockSpec((None, 1, d), lambda j, *_: (cond_row(j * LANES) * N_MOD + 1, 0, 0))]
        args += [mod_next, mod_next]
    if follow == "plain":
        out_shape.append(jax.ShapeDtypeStruct((nt, d), BF16))
        out_specs.append(rows)
    if follow == "dft":
        in_specs.append(pl.BlockSpec(cs.shape, lambda j, *_: (0, 0)))
        args.append(cs)
        out_shape.append(jax.ShapeDtypeStruct((2, nt, d), BF16))
        out_specs.append(pl.BlockSpec((2, LANES, d), lambda j, *_: (0, j, 0)))
    outs = pl.pallas_call(
        functools.partial(_combine_kernel, slots=slots, follow=follow, split=split),
        out_shape=tuple(out_shape),
        grid_spec=pltpu.PrefetchScalarGridSpec(
            num_scalar_prefetch=2, grid=(nbt,), in_specs=in_specs, out_specs=tuple(out_specs),
            scratch_shapes=[pltpu.VMEM((2, PICK_ROWS, d), F32), pltpu.VMEM((LANES, d), F32),
                            pltpu.VMEM((PICK_ROWS, LANES), jnp.int32), pltpu.SemaphoreType.DMA((2,))]),
        compiler_params=_params("arbitrary"), name="combine",
    )(*args)
    return outs if len(outs) > 1 else outs[0]


def _unpack_rows(words):
    lo = pltpu.bitcast(words << 16, F32)
    hi = pltpu.bitcast(words & jnp.uint32(0xFFFF0000), F32)
    return jnp.concatenate([lo, hi], axis=1).astype(BF16)


def _expert_weights(w_refs, w_f32, w_bf16, sems, layer):
    tn = w_f32.shape[2]
    n_col = pl.num_programs(1)
    _stationary_weights(
        w_refs, w_f32, w_bf16, sems, pl.program_id(0) * n_col + pl.program_id(1), pl.num_programs(0) * n_col,
        lambda ref, b: ref.at[layer, b // n_col, :, pl.ds(pl.multiple_of((b % n_col) * tn, tn), tn)], 2)


def _swiglu_kernel(a_ref, wg_ref, wu_ref, o_ref, w_f32, w_bf16, sems, *, layer):
    _expert_weights([wg_ref, wu_ref], w_f32, w_bf16, sems, layer)
    a = _unpack_rows(a_ref[...])
    gte = jnp.dot(a, w_bf16[0], preferred_element_type=F32)
    up = jnp.dot(a, w_bf16[1], preferred_element_type=F32)
    o_ref[...] = (gte * jax.nn.sigmoid(gte) * up).astype(o_ref.dtype)


def expert_swiglu(xe, w_gate, w_up, layer, *, tm=512, tn=512):
    e, c, _ = xe.shape
    k, f = w_gate.shape[-2:]
    tm, tn = _tile(c, tm), _tile(f, tn)
    w_spec = pl.BlockSpec(memory_space=pl.ANY)
    return pl.pallas_call(
        functools.partial(_swiglu_kernel, layer=layer),
        out_shape=jax.ShapeDtypeStruct((e, c, f), BF16),
        grid=(e, f // tn, c // tm),
        in_specs=[pl.BlockSpec((None, tm, k // 2), lambda x, j, i: (x, i, 0)), w_spec, w_spec],
        out_specs=pl.BlockSpec((None, tm, tn), lambda x, j, i: (x, i, j)),
        scratch_shapes=_weight_scratch(2, k, tn),
        compiler_params=_params("arbitrary", "arbitrary", "arbitrary"), name="expert_swiglu",
    )(xe, w_gate, w_up)


def _down_kernel(a_ref, w_ref, gate_ref, o_ref, w_f32, w_bf16, sems, *, layer):
    _expert_weights([w_ref], w_f32, w_bf16, sems, layer)
    acc = jnp.dot(a_ref[...], w_bf16[0], preferred_element_type=F32)
    o_ref[...] = acc * pltpu.bitcast(gate_ref[...], F32)[:, 0:1]


def expert_down(hid, w_down, xe, layer, *, tm=512, tn=2048):
    e, c, f = hid.shape
    d = w_down.shape[-1]
    tm, tn = _tile(c, tm), _tile(d, tn)
    gate_blk = (xe.shape[-1] - LANES) // LANES
    return pl.pallas_call(
        functools.partial(_down_kernel, layer=layer),
        out_shape=jax.ShapeDtypeStruct((e, c, d), F32),
        grid=(e, d // tn, c // tm),
        in_specs=[pl.BlockSpec((None, tm, f), lambda x, j, i: (x, i, 0)),
                  pl.BlockSpec(memory_space=pl.ANY),
                  pl.BlockSpec((None, tm, LANES), lambda x, j, i: (x, i, gate_blk))],
        out_specs=pl.BlockSpec((None, tm, tn), lambda x, j, i: (x, i, j)),
        scratch_shapes=_weight_scratch(1, f, tn),
        compiler_params=_params("arbitrary", "arbitrary", "arbitrary"), name="expert_down",
    )(hid, w_down, xe)


def _attn_kernel(q_ref, *rest, group, n_seg, chunk):
    kv_refs, o_ref = rest[:2 * n_seg], rest[2 * n_seg]
    tq = q_ref.shape[0]
    q = jnp.concatenate([q_ref[:, g * HEAD_DIM:(g + 1) * HEAD_DIM] for g in range(group)], axis=0)
    m = jnp.full((group * tq, 1), -jnp.inf, F32)
    acc = jnp.zeros((group * tq, 2 * HEAD_DIM), F32)
    for seg in range(n_seg):
        k_ref, v_ref = kv_refs[2 * seg], kv_refs[2 * seg + 1]
        length = k_ref.shape[0]
        step = math.gcd(length, chunk)
        for lo in range(0, length, step):
            k = k_ref[lo:lo + step, :].astype(BF16)
            v = v_ref[lo:lo + step, :].astype(BF16)
            v1 = jnp.concatenate([v, jnp.ones_like(v)], axis=1)
            s = lax.dot_general(q, k, (((1,), (1,)), ((), ())), preferred_element_type=F32)
            m_new = jnp.maximum(m, jnp.max(s, axis=-1, keepdims=True))
            p = jnp.exp2(s - m_new).astype(BF16)
            acc = jnp.exp2(m - m_new) * acc + jnp.dot(p, v1, preferred_element_type=F32)
            m = m_new
    o = acc[:, :HEAD_DIM] / acc[:, HEAD_DIM:HEAD_DIM + 1]
    for g in range(group):
        o_ref[:, g * HEAD_DIM:(g + 1) * HEAD_DIM] = o[g * tq:(g + 1) * tq].astype(o_ref.dtype)


def attention(q, segments, *, n_kv, n_req, q_row0, lq, tq, chunk=512):
    group = q.shape[-1] // HEAD_DIM // n_kv
    tq = _tile(lq, tq)
    qb0, nqb = q_row0 // tq, lq // tq
    qw = group * HEAD_DIM
    in_specs = [pl.BlockSpec((tq, qw), lambda r, h, t: (qb0 + r * nqb + t, h))]
    args = [q]
    for k_arr, v_arr, block, imap in segments:
        spec = pl.BlockSpec(block, lambda r, h, t, imap=imap: imap(r, h))
        in_specs += [spec, spec]
        args += [k_arr, v_arr]
    return pl.pallas_call(
        functools.partial(_attn_kernel, group=group, n_seg=len(segments), chunk=chunk),
        out_shape=jax.ShapeDtypeStruct((n_req * lq, q.shape[-1]), BF16),
        grid=(n_req, n_kv, nqb),
        in_specs=in_specs,
        out_specs=pl.BlockSpec((tq, qw), lambda r, h, t: (r * nqb + t, h)),
        compiler_params=_params("parallel", "parallel", "arbitrary"), name="attention",
    )(*args)


def _seq_dft_kernel(m_ref, u_ref, o_ref, f_scr):
    length = u_ref.shape[1]
    half = length // 2
    blk = math.gcd(half, MXU_COLS)
    n_blk = length // blk

    @pl.when(pl.program_id(2) == 0)
    def _():
        r = lax.broadcasted_iota(jnp.int32, (blk, blk), 0)
        c = lax.broadcasted_iota(jnp.int32, (blk, blk), 1)
        mirror = _ones_where((r >= 1) & (c == blk - r), BF16)
        first = _ones_where((r == 0) & (c == 0), BF16)
        row_id = lax.broadcasted_iota(jnp.int32, (blk, 1), 0)
        for b in range(half // blk):
            lo = b * blk
            for plane, sign in ((0, 1.0), (1, -1.0)):
                src = (n_blk - 1 - b) * blk
                mir = jnp.dot(mirror, u_ref[plane, src:src + blk, :], preferred_element_type=F32)
                if b >= 1:
                    mir += jnp.dot(first, u_ref[plane, src + blk:src + 2 * blk, :], preferred_element_type=F32)
                f = u_ref[plane, lo:lo + blk, :].astype(F32) + sign * mir
                if b == 0 and plane == 1:
                    f = jnp.where(row_id == 0, u_ref[0, half:half + 1, :].astype(F32), f)
                f_scr[plane, lo:lo + blk, :] = f.astype(BF16)

    acc = jnp.dot(m_ref[0], f_scr[0], preferred_element_type=F32)
    acc += jnp.dot(m_ref[1], f_scr[1], preferred_element_type=F32)
    o_ref[...] = acc.astype(o_ref.dtype)


def seq_dft(mats, u, *, n_req, row0, tm=512, tn=512):
    length, half = mats.shape[1:]
    d = u.shape[-1]
    tm, tn = _tile(length, tm), _tile(d, tn)
    rb0 = row0 // length
    return pl.pallas_call(
        _seq_dft_kernel,
        out_shape=jax.ShapeDtypeStruct((n_req * length, d), BF16),
        grid=(n_req, d // tn, length // tm),
        in_specs=[pl.BlockSpec((2, tm, half), lambda r, j, i: (0, i, 0)),
                  pl.BlockSpec((2, length, tn), lambda r, j, i: (0, rb0 + r, j))],
        out_specs=pl.BlockSpec((tm, tn), lambda r, j, i: (r * (length // tm) + i, j)),
        scratch_shapes=[pltpu.VMEM((2, half, tn), BF16)],
        compiler_params=_params("parallel", "parallel", "arbitrary"), name="seq_dft",
    )(mats, u)


def _seq_dft_mats(length):
    cos, sin = _dft_mats(length)
    half = length // 2
    return jnp.stack([cos[:, :half], (-sin[:, :half]).at[:, 0].set(cos[:, half])]).astype(BF16)


def _dft_mats(length):
    n2 = 1 << ((length.bit_length() - 1) // 2)
    n1 = length // n2
    assert n1 * n2 == length
    j = jnp.arange(length, dtype=jnp.int32)[:, None]
    ang_a = ((j * jnp.arange(n1, dtype=jnp.int32)[None, :]) % n1).astype(F32) * (2.0 * math.pi / n1)
    ang_b = ((j * jnp.arange(n2, dtype=jnp.int32)[None, :]) % length).astype(F32) * (2.0 * math.pi / length)
    ca, sa = jnp.cos(ang_a)[:, :, None], jnp.sin(ang_a)[:, :, None]
    cb, sb = jnp.cos(ang_b)[:, None, :], jnp.sin(ang_b)[:, None, :]
    norm = 1.0 / math.sqrt(length)
    cos = (ca * cb - sa * sb).reshape(length, length) * norm
    sin = (sa * cb + ca * sb).reshape(length, length) * norm
    return cos, sin


def _rope_tables(n_prompt, n_req, n_lat):
    half = HEAD_DIM // 2
    rows = n_lat // GRID_W
    row_pos = jnp.repeat(jnp.arange(rows, dtype=F32), GRID_W)
    col_pos = jnp.tile(jnp.arange(GRID_W, dtype=F32), rows)
    inv_freq = ROPE_THETA ** (-jnp.arange(0, half, 2, dtype=F32) / half)

    def cs(pos):
        ang = pos[:, None] * inv_freq[None, :]
        return jnp.cos(ang), jnp.sin(ang)

    cr, sr = cs(row_pos)
    cc, sc = cs(col_pos)
    cos = jnp.concatenate([cr, cr, cc, cc], axis=-1)
    sin = jnp.concatenate([-sr, sr, -sc, sc], axis=-1)
    cos = jnp.concatenate([jnp.ones((n_prompt, HEAD_DIM), F32), jnp.tile(cos, (n_req, 1))], axis=0)
    sin = jnp.concatenate([jnp.zeros((n_prompt, HEAD_DIM), F32), jnp.tile(sin, (n_req, 1))], axis=0)
    return cos, sin


def kernel(x_prompt, x_sample, cache_k, cache_v, c, c_ctx, norm_g, w_mod, b_mod, w_q, w_k, w_v, w_o,
           q_norm_g, k_norm_g, w_fourier, w_router, w_gate, w_up, w_down, final_norm_g):
    batch, seq, d = x_prompt.shape
    n_req, n_lat, _ = x_sample.shape
    depth = w_mod.shape[0]
    n_kv = cache_k.shape[3]
    n_exp = w_router.shape[-1]
    n_prompt = batch * seq
    nt = n_prompt + n_req * n_lat
    assert n_req + 1 <= COND_ROWS
    assert n_prompt % n_lat == 0

    def cond_row(r):
        return jnp.where(r < n_prompt, 0, 1 + (r - n_prompt) // n_lat)

    cond = jnp.zeros((COND_ROWS, d), F32).at[0].set(c_ctx).at[1:1 + n_req].set(c)
    mod = modulation_all(cond, w_mod, b_mod)
    cos, sin = _rope_tables(n_prompt, n_req, n_lat)
    scale = math.log2(math.e) / math.sqrt(HEAD_DIM)

    cc, sc = _dft_mats(FGROUP_DIM)
    cs_chan = jnp.concatenate([cc, sc], axis=1).astype(BF16)
    mats_p = mats_s = None
    if depth > 1:
        mats_p, mats_s = _seq_dft_mats(seq), _seq_dft_mats(n_lat)

    x = (x_prompt.reshape(n_prompt, d), x_sample.reshape(n_req * n_lat, d))
    assert n_prompt % LANES == 0 and (n_req * n_lat) % LANES == 0
    cap_p = (CAPACITY_FACTOR * n_prompt) // n_exp
    cap_s = (CAPACITY_FACTOR * n_req * n_lat) // n_exp
    groups = ((0, n_prompt // LANES, cap_p, 0), (n_prompt // LANES, n_req * n_lat // LANES, cap_s, cap_p))
    slots = cap_p + cap_s
    assert cap_p % SUBLANES == 0 and cap_s % SUBLANES == 0
    new_k, new_v = [], []
    mixer_in = None
    for i in range(depth):
        mod_l = mod[i].reshape(COND_ROWS * N_MOD, 1, d)
        j = i // 2
        if i % 2 == 0:
            h = mixer_in if i > 0 else ada_norm(x, norm_g[i, 0], mod_l, 0, 1, cond_row)
            q = project(h, w_q, j, out_dtypes=(BF16,),
                        head=(q_norm_g[j].reshape(1, HEAD_DIM) * scale, cos, sin))
            kf, kb = project(h, w_k, j, out_dtypes=(F32, BF16),
                             head=(k_norm_g[j].reshape(1, HEAD_DIM), cos, sin))
            vf, vb = project(h, w_v, j, out_dtypes=(F32, BF16))
            kvw = n_kv * HEAD_DIM
            new_k.append(kf[:n_prompt].reshape(batch, seq, n_kv, HEAD_DIM))
            new_v.append(vf[:n_prompt].reshape(batch, seq, n_kv, HEAD_DIM))
            att_p = attention(q, [(kb, vb, (seq, HEAD_DIM), lambda r, h: (r, h))],
                              n_kv=n_kv, n_req=batch, q_row0=0, lq=seq, tq=256)
            past = cache_k.shape[2]
            lat0 = n_prompt // n_lat
            att_s = attention(
                q,
                [(cache_k.reshape(n_req, -1, past, kvw), cache_v.reshape(n_req, -1, past, kvw),
                  (None, None, past, HEAD_DIM), lambda r, h, j=j: (r, j, 0, h)),
                 (kb, vb, (n_lat, HEAD_DIM), lambda r, h: (lat0 + r, h))],
                n_kv=n_kv, n_req=n_req, q_row0=n_prompt, lq=n_lat, tq=512)
            x = project((att_p, att_s), w_o, j, out_dtypes=(F32,), residual=(x, mod_l, 2, cond_row))
        else:
            u = mixer_in if i > 0 else ada_norm(x, norm_g[i, 0], mod_l, 0, 1, cond_row, mode="dft", cs=cs_chan)
            f_p = seq_dft(mats_p, u, n_req=batch, row0=0, tn=d)
            f_s = seq_dft(mats_s, u, n_req=n_req, row0=n_prompt)
            x = project((f_p, f_s), w_fourier, j, out_dtypes=(F32,), residual=(x, mod_l, 2, cond_row))

        h, aff = ada_norm(x, norm_g[i, 1], mod_l, 3, 4, cond_row, mode="router",
                          w_router_t=w_router[i].T)
        pick, tot, offs = route_select(aff, groups)
        xe = dispatch(h, pick, aff, tot, offs, slots)
        hid = expert_swiglu(xe, w_gate, w_up, i)
        ye = expert_down(hid, w_down, xe, i)
        if i + 1 < depth:
            follow = "plain" if (i + 1) % 2 == 0 else "dft"
            x, mixer_in = combine(x, ye, pick, tot, offs, mod_l, 5, cond_row, follow=follow, g=norm_g[i + 1, 0],
                                  mod_next=mod[i + 1].reshape(COND_ROWS * N_MOD, 1, d),
                                  cs=cs_chan if follow == "dft" else None)
        else:
            y_p, y_s = combine(x, ye, pick, tot, offs, mod_l, 5, cond_row, follow="final", g=final_norm_g,
                               first_rows=n_prompt)

    return (y_p.reshape(batch, seq, d), y_s.reshape(n_req, n_lat, d),
            jnp.stack(new_k, axis=1), jnp.stack(new_v, axis=1))
```

```python
import functools
import math

import jax
import jax.numpy as jnp
from jax import lax
from jax.experimental import pallas as pl
from jax.experimental.pallas import tpu as pltpu

F32 = jnp.float32
BF16 = jnp.bfloat16

GRID_W = 64
HEAD_DIM = 128
FGROUP_DIM = 128
ROPE_THETA = 10000.0
CAPACITY_FACTOR = 2
N_MOD = 6
EPS = 1e-6

LANES = 128
SUBLANES = 8
MXU_COLS = 256
COND_ROWS = 8
VMEM_LIMIT_BYTES = 56 * 1024 * 1024


def _params(*sem):
    return pltpu.CompilerParams(dimension_semantics=sem, vmem_limit_bytes=VMEM_LIMIT_BYTES)


def _tile(n, want):
    return math.gcd(n, want)


def _mod_kernel(c_ref, w_ref, b_ref, o_ref):
    c = c_ref[...]
    a = (c * jax.nn.sigmoid(c)).astype(BF16)
    o_ref[...] = jnp.dot(a, w_ref[...].astype(BF16), preferred_element_type=F32) + b_ref[...]


def modulation_all(cond, w_mod, b_mod):
    depth, d, n = w_mod.shape
    tn = _tile(n, 512)
    return pl.pallas_call(
        _mod_kernel,
        out_shape=jax.ShapeDtypeStruct((depth, COND_ROWS, n), F32),
        grid=(depth, n // tn),
        in_specs=[
            pl.BlockSpec((COND_ROWS, d), lambda l, j: (0, 0)),
            pl.BlockSpec((None, d, tn), lambda l, j: (l, 0, j)),
            pl.BlockSpec((None, 1, tn), lambda l, j: (l, 0, j)),
        ],
        out_specs=pl.BlockSpec((None, COND_ROWS, tn), lambda l, j: (l, 0, j)),
        compiler_params=_params("parallel", "parallel"),
        name="modulation",
    )(cond, w_mod, b_mod.reshape(depth, 1, n))


def _rms(x, g):
    return x * lax.rsqrt(jnp.mean(x * x, axis=-1, keepdims=True) + EPS) * g


def _split_rows(specs_of, first, rest_rows, tm):
    split = first.shape[0] // tm
    assert split * tm == first.shape[0] and rest_rows.shape[0] % tm == 0
    return (specs_of(lambda i: jnp.minimum(i, split - 1)), specs_of(lambda i: jnp.maximum(i - split, 0))), split


def _ada_kernel(*refs, mode, split):
    if split is None:
        x_ref, g_ref, sh_ref, sc_ref, *rest = refs
        _ada_apply(x_ref[...], g_ref, sh_ref, sc_ref, rest, mode)
        return
    x0_ref, x1_ref, g_ref, sh_ref, sc_ref, *rest = refs

    @pl.when(pl.program_id(0) < split)
    def _():
        _ada_apply(x0_ref[...], g_ref, sh_ref, sc_ref, rest, mode)

    @pl.when(pl.program_id(0) >= split)
    def _():
        _ada_apply(x1_ref[...], g_ref, sh_ref, sc_ref, rest, mode)


def _ada_apply(x, g_ref, sh_ref, sc_ref, rest, mode):
    h = _rms(x, g_ref[...]) * (1.0 + sc_ref[...]) + sh_ref[...]
    hb = h.astype(BF16)
    if mode == "plain":
        (h_ref,) = rest
        h_ref[...] = hb
    elif mode == "router":
        wr_ref, h_ref, aff_ref = rest
        h_ref[...] = hb
        logits = lax.dot_general(wr_ref[...].astype(BF16), hb, (((1,), (1,)), ((), ())),
                                 preferred_element_type=F32)
        m = jnp.max(logits, axis=0, keepdims=True)
        p = jnp.exp(logits - m)
        aff = p / jnp.sum(p, axis=0, keepdims=True)
        for blk in range(aff_ref.shape[0]):
            aff_ref[blk] = aff[:, blk * LANES:(blk + 1) * LANES]
    else:
        cs_ref, u_ref = rest
        n_groups = hb.shape[1] // FGROUP_DIM
        for grp in range(n_groups):
            lo = grp * FGROUP_DIM
            r = jnp.dot(hb[:, lo:lo + FGROUP_DIM], cs_ref[...], preferred_element_type=F32)
            u_ref[0, :, lo:lo + FGROUP_DIM] = r[:, :FGROUP_DIM].astype(BF16)
            u_ref[1, :, lo:lo + FGROUP_DIM] = r[:, FGROUP_DIM:].astype(BF16)


def ada_norm(x, g, mod_l, shift_idx, scale_idx, cond_row, *, mode="plain", w_router_t=None, cs=None):
    row = lambda i: (i, 0)
    if isinstance(x, tuple):
        d = x[0].shape[1]
        nt = x[0].shape[0] + x[1].shape[0]
        tm = _tile(math.gcd(x[0].shape[0], x[1].shape[0]), 256)
        x_specs, split = _split_rows(lambda blk: pl.BlockSpec((tm, d), lambda i: (blk(i), 0)), x[0], x[1], tm)
        x_specs, x_args = list(x_specs), list(x)
    else:
        nt, d = x.shape
        tm = _tile(nt, 256)
        x_specs, x_args, split = [pl.BlockSpec((tm, d), row)], [x], None
    grid = (nt // tm,)
    in_specs = x_specs + [
        pl.BlockSpec((1, d), lambda i: (0, 0)),
        pl.BlockSpec((None, 1, d), lambda i: (cond_row(i * tm) * N_MOD + shift_idx, 0, 0)),
        pl.BlockSpec((None, 1, d), lambda i: (cond_row(i * tm) * N_MOD + scale_idx, 0, 0)),
    ]
    args = x_args + [g.reshape(1, d), mod_l, mod_l]
    if mode == "plain":
        out_shape = jax.ShapeDtypeStruct((nt, d), BF16)
        out_specs = pl.BlockSpec((tm, d), row)
    elif mode == "router":
        e = w_router_t.shape[0]
        in_specs.append(pl.BlockSpec((e, d), lambda i: (0, 0)))
        args.append(w_router_t)
        out_shape = (jax.ShapeDtypeStruct((nt, d), BF16), jax.ShapeDtypeStruct((nt // LANES, e, LANES), F32))
        out_specs = (pl.BlockSpec((tm, d), row), pl.BlockSpec((tm // LANES, e, LANES), lambda i: (i, 0, 0)))
    else:
        in_specs.append(pl.BlockSpec(cs.shape, lambda i: (0, 0)))
        args.append(cs)
        out_shape = jax.ShapeDtypeStruct((2, nt, d), BF16)
        out_specs = pl.BlockSpec((2, tm, d), lambda i: (0, i, 0))
    return pl.pallas_call(
        functools.partial(_ada_kernel, mode=mode, split=split),
        out_shape=out_shape, grid=grid, in_specs=in_specs, out_specs=out_specs,
        compiler_params=_params("parallel"), name="ada_norm_" + mode,
    )(*args)


def _stationary_weights(w_hbm, w_f32, w_bf16, sems, block, n_blocks, slicer, inner_axis):
    def copy(b, which):
        return pltpu.make_async_copy(slicer(w_hbm[which], b), w_f32.at[which], sems.at[which])

    @pl.when(pl.program_id(inner_axis) == 0)
    def _():
        @pl.when(block == 0)
        def _():
            for which in range(len(w_hbm)):
                copy(block, which).start()

        for which in range(len(w_hbm)):
            copy(block, which).wait()
            w_bf16[which] = w_f32[which].astype(BF16)

        @pl.when(block + 1 < n_blocks)
        def _():
            for which in range(len(w_hbm)):
                copy(block + 1, which).start()


def _weight_scratch(n_w, k, tn):
    return [pltpu.VMEM((n_w, k, tn), F32), pltpu.VMEM((n_w, k, tn), BF16), pltpu.SemaphoreType.DMA((n_w,))]


def _project_weights(w_ref, w_f32, w_bf16, sems, layer):
    tn = w_f32.shape[2]
    _stationary_weights(
        [w_ref], w_f32, w_bf16, sems, pl.program_id(0), pl.num_programs(0),
        lambda ref, b: ref.at[layer, :, pl.ds(pl.multiple_of(b * tn, tn), tn)], 1)


def _mm_plain_kernel(a_ref, w_ref, *rest, n_out, layer):
    outs, (w_f32, w_bf16, sems) = rest[:n_out], rest[n_out:]
    _project_weights(w_ref, w_f32, w_bf16, sems, layer)
    acc = jnp.dot(a_ref[...], w_bf16[0], preferred_element_type=F32)
    for o_ref in outs:
        o_ref[...] = acc.astype(o_ref.dtype)


def _swap32(y):
    lane = lax.broadcasted_iota(jnp.int32, y.shape, 1)
    first = (lane % 64) < 32
    return jnp.where(first, pltpu.roll(y, 96, 1), pltpu.roll(y, 32, 1))


def _mm_headnorm_kernel(a_ref, w_ref, g_ref, cos_ref, sin_ref, *rest, n_out, layer):
    outs, (w_f32, w_bf16, sems) = rest[:n_out], rest[n_out:]
    _project_weights(w_ref, w_f32, w_bf16, sems, layer)
    a = a_ref[...]
    cos, sin = cos_ref[...], sin_ref[...]
    tn = w_bf16.shape[2]
    step = math.gcd(tn, MXU_COLS)
    for c0 in range(0, tn, step):
        acc = jnp.dot(a, w_bf16[0, :, c0:c0 + step], preferred_element_type=F32)
        for lo in range(0, step, HEAD_DIM):
            y = _rms(acc[:, lo:lo + HEAD_DIM], g_ref[...])
            y = y * cos + _swap32(y) * sin
            for o_ref in outs:
                o_ref[:, c0 + lo:c0 + lo + HEAD_DIM] = y.astype(o_ref.dtype)


def _mm_residual_kernel(a0_ref, a1_ref, w_ref, *rest, layer, split, res_split):
    if res_split:
        res0_ref, res1_ref, gate_ref, o_ref, w_f32, w_bf16, sems = rest
    else:
        res0_ref, gate_ref, o_ref, w_f32, w_bf16, sems = rest
        res1_ref = res0_ref
    _project_weights(w_ref, w_f32, w_bf16, sems, layer)

    def emit(a_ref, res_ref):
        acc = jnp.dot(a_ref[...], w_bf16[0], preferred_element_type=F32)
        o_ref[...] = res_ref[...] + gate_ref[...] * acc

    @pl.when(pl.program_id(1) < split)
    def _():
        emit(a0_ref, res0_ref)

    @pl.when(pl.program_id(1) >= split)
    def _():
        emit(a1_ref, res1_ref)


def project(a, w_stack, layer, *, out_dtypes, head=None, residual=None, tm=512, tn=1024):
    n = w_stack.shape[-1]
    if residual is not None:
        a0, a1 = a
        k = a0.shape[1]
        m = a0.shape[0] + a1.shape[0]
        tm, tn = _tile(math.gcd(a0.shape[0], a1.shape[0]), tm), _tile(n, tn)
        split = a0.shape[0] // tm
        in_specs = [pl.BlockSpec((tm, k), lambda j, i: (jnp.minimum(i, split - 1), 0)),
                    pl.BlockSpec((tm, k), lambda j, i: (jnp.maximum(i - split, 0), 0)),
                    pl.BlockSpec(memory_space=pl.ANY)]
        args = [a0, a1, w_stack]
    else:
        m, k = a.shape
        tm, tn = _tile(m, tm), _tile(n, tn)
        in_specs = [pl.BlockSpec((tm, k), lambda j, i: (i, 0)), pl.BlockSpec(memory_space=pl.ANY)]
        args = [a, w_stack]
    grid = (n // tn, m // tm)
    out_block = pl.BlockSpec((tm, tn), lambda j, i: (i, j))
    if head is not None:
        g, cos, sin = head
        in_specs += [pl.BlockSpec((1, HEAD_DIM), lambda j, i: (0, 0)),
                     pl.BlockSpec((tm, HEAD_DIM), lambda j, i: (i, 0)),
                     pl.BlockSpec((tm, HEAD_DIM), lambda j, i: (i, 0))]
        args += [g, cos, sin]
        body = functools.partial(_mm_headnorm_kernel, n_out=len(out_dtypes), layer=layer)
    elif residual is not None:
        res, mod_l, gate_idx, cond_row = residual
        if isinstance(res, tuple):
            assert res[0].shape[0] == a0.shape[0]
            in_specs += [pl.BlockSpec((tm, tn), lambda j, i: (jnp.minimum(i, split - 1), j)),
                         pl.BlockSpec((tm, tn), lambda j, i: (jnp.maximum(i - split, 0), j))]
            args += list(res)
        else:
            in_specs.append(pl.BlockSpec((tm, tn), lambda j, i: (i, j)))
            args.append(res)
        in_specs.append(pl.BlockSpec((None, 1, tn), lambda j, i: (cond_row(i * tm) * N_MOD + gate_idx, 0, j)))
        args.append(mod_l)
        body = functools.partial(_mm_residual_kernel, layer=layer, split=split, res_split=isinstance(res, tuple))
    else:
        body = functools.partial(_mm_plain_kernel, n_out=len(out_dtypes), layer=layer)
    outs = pl.pallas_call(
        body,
        out_shape=tuple(jax.ShapeDtypeStruct((m, n), dt) for dt in out_dtypes),
        grid=grid, in_specs=in_specs, out_specs=tuple(out_block for _ in out_dtypes),
        scratch_shapes=_weight_scratch(1, k, tn),
        compiler_params=_params("arbitrary", "arbitrary"), name="project",
    )(*args)
    return outs if len(outs) > 1 else outs[0]


PICK_ROWS = 512


def _ones_where(mask, dtype):
    return jnp.where(mask, 1.0, 0.0).astype(dtype)


def _select_kernel(aff_ref, pick_ref, tot_ref, offs_ref, w_scr, o_scr, *, groups):
    n_exp = aff_ref.shape[1]
    li = lax.broadcasted_iota(jnp.int32, (LANES, LANES), 0)
    lj = lax.broadcasted_iota(jnp.int32, (LANES, LANES), 1)
    tri = _ones_where(li <= lj, BF16)

    def chunk_cumsum(mask, nb):
        w = jnp.dot(_ones_where(mask, BF16).reshape(nb * n_exp, LANES), tri,
                    preferred_element_type=F32).reshape(nb, n_exp, LANES)
        w_scr[0:nb] = w

        def body(j, run):
            o_scr[j] = jnp.broadcast_to(run, (n_exp, LANES))
            return run + w_scr[j][:, LANES - 1:LANES]

        lax.fori_loop(0, nb, body, jnp.zeros((n_exp, 1), F32))
        return w, o_scr[0:nb]

    for j0, nb, cap, slot0 in groups:
        keys = pltpu.bitcast(aff_ref[j0:j0 + nb], jnp.int32)

        def count(mask):
            return jnp.sum(jnp.sum(_ones_where(mask, F32), axis=0), axis=1, keepdims=True)

        def search(it, prefix):
            cand = prefix | lax.shift_left(jnp.int32(1), 30 - it)
            return jnp.where(count(keys >= cand[None]) >= cap, cand, prefix)

        thr = lax.fori_loop(0, 31, search, jnp.zeros((n_exp, 1), jnp.int32))[None]
        above, ties = keys > thr, keys == thr
        need = (cap - count(above))[None]
        w_eq, o_eq = chunk_cumsum(ties, nb)
        sel = above | (ties & (w_eq + o_eq <= need))
        w, offs = chunk_cumsum(sel, nb)
        tot = jnp.broadcast_to(w[:, :, LANES - 1:LANES], w.shape)
        lead = offs - SUBLANES * jnp.floor(offs / SUBLANES)
        seg = SUBLANES * jnp.floor((lead + tot + (SUBLANES - 1)) / SUBLANES)
        rows = nb * n_exp
        blk = math.gcd(rows, 256)
        bi = lax.broadcasted_iota(jnp.int32, (blk, blk), 0)
        bj = lax.broadcasted_iota(jnp.int32, (blk, blk), 1)
        lower = _ones_where((bi // n_exp == bj // n_exp) & (bj % n_exp < bi % n_exp), BF16)
        seg2 = seg.reshape(rows, LANES).astype(BF16)
        base = jnp.concatenate(
            [jnp.dot(lower, seg2[r0:r0 + blk], preferred_element_type=F32) for r0 in range(0, rows, blk)],
            axis=0).reshape(nb, n_exp, LANES)
        pick_ref[j0:j0 + nb] = jnp.where(sel, base + lead + w - 1.0, -1.0).astype(jnp.int32)
        tot_ref[j0:j0 + nb] = tot.astype(jnp.int32)
        offs_ref[j0:j0 + nb] = offs.astype(jnp.int32) + slot0


def route_select(aff, groups):
    nbt, e, _ = aff.shape
    shape = jax.ShapeDtypeStruct(aff.shape, jnp.int32)
    max_nb = max(g[1] for g in groups)
    pick, tot, offs = pl.pallas_call(
        functools.partial(_select_kernel, groups=groups),
        out_shape=(shape, shape, shape),
        scratch_shapes=[pltpu.VMEM((max_nb, e, LANES), F32), pltpu.VMEM((max_nb, e, LANES), F32)],
        compiler_params=pltpu.CompilerParams(vmem_limit_bytes=VMEM_LIMIT_BYTES), name="route_select",
    )(aff)
    return pick, tot[:, :, 0].reshape(nbt * e), offs[:, :, 0].reshape(nbt * e)


def _sub_blocks():
    first = (3 * PICK_ROWS // 4) // 16 * 16
    return ((0, first), (first, PICK_ROWS - first))


def _fill_owner_rows(tot_ref, offs_ref, j, n_exp, row0, pairs):
    def per_expert(e, base):
        _, _, seg = _segment(tot_ref, offs_ref, j, n_exp, e)
        lo = jnp.maximum(base, row0)
        hi = jnp.minimum(base + seg, row0 + PICK_ROWS)
        vals = [jnp.broadcast_to(src[pl.ds(e, 1), :], (SUBLANES, LANES)) for src, _ in pairs]

        def per_group(g, carry):
            row = pl.multiple_of(lo - row0 + g * SUBLANES, SUBLANES)
            for val, (_, dst) in zip(vals, pairs):
                dst[pl.ds(row, SUBLANES), :] = val
            return carry

        lax.fori_loop(0, jnp.maximum(hi - lo, 0) // SUBLANES, per_group, 0)
        return base + seg

    lax.fori_loop(0, n_exp, per_expert, jnp.int32(0))


def _segment(tot_ref, offs_ref, j, n_exp, e):
    t = tot_ref[j * n_exp + e]
    o = offs_ref[j * n_exp + e]
    r = o & (SUBLANES - 1)
    return o, r, ((r + t + SUBLANES - 1) // SUBLANES) * SUBLANES


def _piece_priority(size):
    return (size // SUBLANES).bit_length() % 2


def _for_each_piece(tot_ref, offs_ref, j, n_exp, row0, rows, fn):
    max_groups = (SUBLANES - 1 + LANES + SUBLANES - 1) // SUBLANES

    def per_expert(e, base):
        o, r, seg = _segment(tot_ref, offs_ref, j, n_exp, e)
        lo = jnp.maximum(base, row0)
        hi = jnp.minimum(base + seg, row0 + rows)
        n = jnp.maximum(hi - lo, 0)
        slot = (o - r) + (lo - base)
        done = jnp.int32(0)
        for bit in range(max_groups.bit_length() - 1, -1, -1):
            size = SUBLANES << bit

            @pl.when((n & size) != 0)
            def _():
                fn(e, pl.multiple_of(slot + done, SUBLANES), pl.multiple_of(lo - row0 + done, SUBLANES), size)

            done = done + (n & size)
        return base + seg

    lax.fori_loop(0, n_exp, per_expert, jnp.int32(0))


def _staging_rows(tot_ref, offs_ref, j, n_exp):
    return lax.fori_loop(0, n_exp, lambda e, s: s + _segment(tot_ref, offs_ref, j, n_exp, e)[2], jnp.int32(0))


def _rounds(tot_ref, offs_ref, j, n_exp):
    return (_staging_rows(tot_ref, offs_ref, j, n_exp) + PICK_ROWS - 1) // PICK_ROWS


def _dispatch_kernel(tot_ref, offs_ref, h_ref, pick_ref, aff_ref, xe_ref, stage2, tail, owner_pick, owner_aff, sems):
    j = pl.program_id(0)
    buf = j % 2
    stage = stage2.at[buf]
    n_exp = pick_ref.shape[0]
    half = h_ref.shape[1] // 2
    total = _staging_rows(tot_ref, offs_ref, j, n_exp)
    n_rounds = (total + PICK_ROWS - 1) // PICK_ROWS

    @pl.when(j == 0)
    def _():
        tail[...] = jnp.zeros_like(tail)
        owner_aff[...] = jnp.zeros_like(owner_aff)

    def copy(b, e, slot, row, size):
        return pltpu.make_async_copy(stage2.at[b, pl.ds(row, size)], xe_ref.at[e, pl.ds(slot, size)], sems.at[b])

    def start_round(jj, r, b):
        _for_each_piece(tot_ref, offs_ref, jj, n_exp, r * PICK_ROWS, PICK_ROWS,
                        lambda e, slot, row, size: copy(b, e, slot, row, size).start(priority=_piece_priority(size)))

    def wait_round(jj, r, b):
        _for_each_piece(tot_ref, offs_ref, jj, n_exp, r * PICK_ROWS, PICK_ROWS,
                        lambda e, slot, row, size: copy(b, e, slot, row, size).wait())

    def splice_tails(row0):
        def per_expert(e, base):
            _, r, seg = _segment(tot_ref, offs_ref, j, n_exp, e)
            last = base + seg - SUBLANES

            @pl.when((r > 0) & (base >= row0) & (base < row0 + PICK_ROWS))
            def _():
                row = pl.multiple_of(base - row0, SUBLANES)
                keep = lax.broadcasted_iota(jnp.int32, (SUBLANES, 1), 0) < r
                stage[pl.ds(row, SUBLANES), :] = jnp.where(keep, tail[e], stage[pl.ds(row, SUBLANES), :])

            @pl.when((seg > 0) & (last >= row0) & (last < row0 + PICK_ROWS))
            def _():
                tail[e] = stage[pl.ds(pl.multiple_of(last - row0, SUBLANES), SUBLANES), :]

            return base + seg

        lax.fori_loop(0, n_exp, per_expert, jnp.int32(0))

    def build(r):
        row0 = r * PICK_ROWS
        owner_pick[...] = jnp.full(owner_pick.shape, -1, jnp.int32)
        _fill_owner_rows(tot_ref, offs_ref, j, n_exp, row0, ((pick_ref, owner_pick), (aff_ref, owner_aff)))
        for b0, n in _sub_blocks():
            @pl.when(total - row0 > b0)
            def _():
                ids = lax.broadcasted_iota(jnp.int32, (n, LANES), 0) + (row0 + b0)
                hit = ids == owner_pick[b0:b0 + n, :]
                rows = jnp.dot(_ones_where(hit, BF16), h_ref[...], preferred_element_type=F32)
                bits = pltpu.bitcast(rows, jnp.uint32)
                stage[b0:b0 + n, 0:half] = bits[:, half:] | (bits[:, :half] >> 16)
                g = jnp.sum(jnp.where(hit, owner_aff[b0:b0 + n, :], 0.0), axis=1, keepdims=True)
                stage[b0:b0 + n, half:half + LANES] = pltpu.bitcast(jnp.broadcast_to(g, (n, LANES)), jnp.uint32)
        splice_tails(row0)

    @pl.when(n_rounds > 0)
    def _():
        build(0)

    @pl.when(j > 0)
    def _():
        prev_rounds = _rounds(tot_ref, offs_ref, j - 1, n_exp)

        @pl.when(prev_rounds > 0)
        def _():
            wait_round(j - 1, prev_rounds - 1, 1 - buf)

    @pl.when(n_rounds > 0)
    def _():
        start_round(j, 0, buf)

    def later_round(r, carry):
        wait_round(j, r - 1, buf)
        build(r)
        start_round(j, r, buf)
        return carry

    lax.fori_loop(1, n_rounds, later_round, 0)

    @pl.when((j == pl.num_programs(0) - 1) & (n_rounds > 0))
    def _():
        wait_round(j, n_rounds - 1, buf)


def dispatch(h, pick, aff, tot, offs, slots):
    nt, d = h.shape
    nbt, e, _ = pick.shape
    width = d // 2 + LANES
    chunk = pl.BlockSpec((None, e, LANES), lambda j, *_: (j, 0, 0))
    return pl.pallas_call(
        _dispatch_kernel,
        out_shape=jax.ShapeDtypeStruct((e, slots, width), jnp.uint32),
        grid_spec=pltpu.PrefetchScalarGridSpec(
            num_scalar_prefetch=2, grid=(nbt,),
            in_specs=[pl.BlockSpec((LANES, d), lambda j, *_: (j, 0)), chunk, chunk],
            out_specs=pl.BlockSpec(memory_space=pl.ANY),
            scratch_shapes=[pltpu.VMEM((2, PICK_ROWS, width), jnp.uint32),
                            pltpu.VMEM((e, SUBLANES, width), jnp.uint32),
                            pltpu.VMEM((PICK_ROWS, LANES), jnp.int32), pltpu.VMEM((PICK_ROWS, LANES), F32),
                            pltpu.SemaphoreType.DMA((2,))]),
        compiler_params=_params("arbitrary"), name="dispatch",
    )(tot, offs, h, pick, aff)


def _combine_kernel(tot_ref, offs_ref, x_ref, pick_ref, gate_ref, ye_ref, *rest, slots, follow, split):
    n_in = {None: 0, "final": 1, "plain": 3, "dft": 4}[follow]
    follow_in, (o_ref, *follow_out), (stage2, acc_ref, owner_pick, sems) = rest[:n_in], rest[n_in:-4], rest[-4:]
    j = pl.program_id(0)
    buf = j % 2
    n_exp = pick_ref.shape[0]
    total = _staging_rows(tot_ref, offs_ref, j, n_exp)
    n_rounds = (total + PICK_ROWS - 1) // PICK_ROWS

    def copy(b, e, slot, row, size):
        return pltpu.make_async_copy(ye_ref.at[pl.ds(e * slots + slot, size)],
                                     stage2.at[b, pl.ds(row, size)], sems.at[b])

    def start_round(jj, r, b):
        _for_each_piece(tot_ref, offs_ref, jj, n_exp, r * PICK_ROWS, PICK_ROWS,
                        lambda e, slot, row, size: copy(b, e, slot, row, size).start(priority=_piece_priority(size)))

    def wait_round(jj, r, b):
        _for_each_piece(tot_ref, offs_ref, jj, n_exp, r * PICK_ROWS, PICK_ROWS,
                        lambda e, slot, row, size: copy(b, e, slot, row, size).wait())

    @pl.when((j == 0) & (n_rounds > 0))
    def _():
        start_round(0, 0, 0)

    @pl.when(j + 1 < pl.num_programs(0))
    def _():
        @pl.when(_rounds(tot_ref, offs_ref, j + 1, n_exp) > 0)
        def _():
            start_round(j + 1, 0, 1 - buf)

    acc_ref[...] = jnp.zeros_like(acc_ref)

    def one_round(r, carry):
        row0 = r * PICK_ROWS

        @pl.when(r > 0)
        def _():
            start_round(j, r, buf)

        wait_round(j, r, buf)
        owner_pick[...] = jnp.full(owner_pick.shape, -1, jnp.int32)
        _fill_owner_rows(tot_ref, offs_ref, j, n_exp, row0, ((pick_ref, owner_pick),))
        for b0, n in _sub_blocks():
            @pl.when(total - row0 > b0)
            def _():
                ids = lax.broadcasted_iota(jnp.int32, (n, LANES), 0) + (row0 + b0)
                onehot = _ones_where(ids == owner_pick[b0:b0 + n, :], F32).T.astype(BF16)
                live = ids[:, 0:1] < total
                y = jnp.where(live, stage2[buf, b0:b0 + n, :], 0.0)
                acc = acc_ref[...]
                for _ in range(2):
                    part = y.astype(BF16)
                    acc = acc + jnp.dot(onehot, part, preferred_element_type=F32)
                    y = y - part.astype(F32)
                acc_ref[...] = acc
        return carry

    lax.fori_loop(0, n_rounds, one_round, 0)
    x_new = x_ref[...] + gate_ref[...] * acc_ref[...]
    if follow == "final":
        y_out = _rms(x_new, follow_in[0][...])

        @pl.when(j < split)
        def _():
            o_ref[...] = y_out

        @pl.when(j >= split)
        def _():
            follow_out[0][...] = y_out
    else:
        o_ref[...] = x_new
        if follow is not None:
            g_ref, sh_ref, sc_ref = follow_in[:3]
            _ada_apply(x_new, g_ref, sh_ref, sc_ref, tuple(follow_in[3:]) + tuple(follow_out), follow)


def combine(x, ye, pick, tot, offs, mod_l, gate_idx, cond_row, *, follow=None, g=None, mod_next=None, cs=None,
            first_rows=None):
    nt, d = x.shape
    nbt, e, _ = pick.shape
    slots = ye.shape[1]
    rows = pl.BlockSpec((LANES, d), lambda j, *_: (j, 0))
    in_specs = [rows,
                pl.BlockSpec((None, e, LANES), lambda j, *_: (j, 0, 0)),
                pl.BlockSpec((None, 1, d), lambda j, *_: (cond_row(j * LANES) * N_MOD + gate_idx, 0, 0)),
                pl.BlockSpec(memory_space=pl.ANY)]
    args = [tot, offs, x, pick, mod_l, ye.reshape(e * slots, d)]
    out_shape, out_specs, split = [jax.ShapeDtypeStruct((nt, d), F32)], [rows], None
    if follow is not None:
        in_specs.append(pl.BlockSpec((1, d), lambda j, *_: (0, 0)))
        args.append(g.reshape(1, d))
    if follow == "final":
        split = first_rows // LANES
        assert split * LANES == first_rows
        out_shape = [jax.ShapeDtypeStruct((first_rows, d), F32), jax.ShapeDtypeStruct((nt - first_rows, d), F32)]
        out_specs = [pl.BlockSpec((LANES, d), lambda j, *_: (jnp.minimum(j, split - 1), 0)),
                     pl.BlockSpec((LANES, d), lambda j, *_: (jnp.maximum(j - split, 0), 0))]
    if follow in ("plain", "dft"):
        in_specs += [pl.BlockSpec((None, 1, d), lambda j, *_: (cond_row(j * LANES) * N_MOD + 0, 0, 0)),
                     pl.BlockSpec((None, 1, d), lambda j, *_: (cond_row(j * LANES) * N_MOD + 1, 0, 0))]
        args += [mod_next, mod_next]
    if follow == "plain":
        out_shape.append(jax.ShapeDtypeStruct((nt, d), BF16))
        out_specs.append(rows)
    if follow == "dft":
        in_specs.append(pl.BlockSpec(cs.shape, lambda j, *_: (0, 0)))
        args.append(cs)
        out_shape.append(jax.ShapeDtypeStruct((2, nt, d), BF16))
        out_specs.append(pl.BlockSpec((2, LANES, d), lambda j, *_: (0, j, 0)))
    outs = pl.pallas_call(
        functools.partial(_combine_kernel, slots=slots, follow=follow, split=split),
        out_shape=tuple(out_shape),
        grid_spec=pltpu.PrefetchScalarGridSpec(
            num_scalar_prefetch=2, grid=(nbt,), in_specs=in_specs, out_specs=tuple(out_specs),
            scratch_shapes=[pltpu.VMEM((2, PICK_ROWS, d), F32), pltpu.VMEM((LANES, d), F32),
                            pltpu.VMEM((PICK_ROWS, LANES), jnp.int32), pltpu.SemaphoreType.DMA((2,))]),
        compiler_params=_params("arbitrary"), name="combine",
    )(*args)
    return outs if len(outs) > 1 else outs[0]


def _unpack_rows(words):
    lo = pltpu.bitcast(words << 16, F32)
    hi = pltpu.bitcast(words & jnp.uint32(0xFFFF0000), F32)
    return jnp.concatenate([lo, hi], axis=1).astype(BF16)


def _expert_weights(w_refs, w_f32, w_bf16, sems, layer):
    tn = w_f32.shape[2]
    n_col = pl.num_programs(1)
    _stationary_weights(
        w_refs, w_f32, w_bf16, sems, pl.program_id(0) * n_col + pl.program_id(1), pl.num_programs(0) * n_col,
        lambda ref, b: ref.at[layer, b // n_col, :, pl.ds(pl.multiple_of((b % n_col) * tn, tn), tn)], 2)


def _swiglu_kernel(a_ref, wg_ref, wu_ref, o_ref, w_f32, w_bf16, sems, *, layer):
    _expert_weights([wg_ref, wu_ref], w_f32, w_bf16, sems, layer)
    a = _unpack_rows(a_ref[...])
    gte = jnp.dot(a, w_bf16[0], preferred_element_type=F32)
    up = jnp.dot(a, w_bf16[1], preferred_element_type=F32)
    o_ref[...] = (gte * jax.nn.sigmoid(gte) * up).astype(o_ref.dtype)


def expert_swiglu(xe, w_gate, w_up, layer, *, tm=512, tn=512):
    e, c, _ = xe.shape
    k, f = w_gate.shape[-2:]
    tm, tn = _tile(c, tm), _tile(f, tn)
    w_spec = pl.BlockSpec(memory_space=pl.ANY)
    return pl.pallas_call(
        functools.partial(_swiglu_kernel, layer=layer),
        out_shape=jax.ShapeDtypeStruct((e, c, f), BF16),
        grid=(e, f // tn, c // tm),
        in_specs=[pl.BlockSpec((None, tm, k // 2), lambda x, j, i: (x, i, 0)), w_spec, w_spec],
        out_specs=pl.BlockSpec((None, tm, tn), lambda x, j, i: (x, i, j)),
        scratch_shapes=_weight_scratch(2, k, tn),
        compiler_params=_params("arbitrary", "arbitrary", "arbitrary"), name="expert_swiglu",
    )(xe, w_gate, w_up)


def _down_kernel(a_ref, w_ref, gate_ref, o_ref, w_f32, w_bf16, sems, *, layer):
    _expert_weights([w_ref], w_f32, w_bf16, sems, layer)
    acc = jnp.dot(a_ref[...], w_bf16[0], preferred_element_type=F32)
    o_ref[...] = acc * pltpu.bitcast(gate_ref[...], F32)[:, 0:1]


def expert_down(hid, w_down, xe, layer, *, tm=512, tn=2048):
    e, c, f = hid.shape
    d = w_down.shape[-1]
    tm, tn = _tile(c, tm), _tile(d, tn)
    gate_blk = (xe.shape[-1] - LANES) // LANES
    return pl.pallas_call(
        functools.partial(_down_kernel, layer=layer),
        out_shape=jax.ShapeDtypeStruct((e, c, d), F32),
        grid=(e, d // tn, c // tm),
        in_specs=[pl.BlockSpec((None, tm, f), lambda x, j, i: (x, i, 0)),
                  pl.BlockSpec(memory_space=pl.ANY),
                  pl.BlockSpec((None, tm, LANES), lambda x, j, i: (x, i, gate_blk))],
        out_specs=pl.BlockSpec((None, tm, tn), lambda x, j, i: (x, i, j)),
        scratch_shapes=_weight_scratch(1, f, tn),
        compiler_params=_params("arbitrary", "arbitrary", "arbitrary"), name="expert_down",
    )(hid, w_down, xe)


def _attn_kernel(q_ref, *rest, group, n_seg, chunk):
    kv_refs, o_ref = rest[:2 * n_seg], rest[2 * n_seg]
    tq = q_ref.shape[0]
    q = jnp.concatenate([q_ref[:, g * HEAD_DIM:(g + 1) * HEAD_DIM] for g in range(group)], axis=0)
    m = jnp.full((group * tq, 1), -jnp.inf, F32)
    acc = jnp.zeros((group * tq, 2 * HEAD_DIM), F32)
    for seg in range(n_seg):
        k_ref, v_ref = kv_refs[2 * seg], kv_refs[2 * seg + 1]
        length = k_ref.shape[0]
        step = math.gcd(length, chunk)
        for lo in range(0, length, step):
            k = k_ref[lo:lo + step, :].astype(BF16)
            v = v_ref[lo:lo + step, :].astype(BF16)
            v1 = jnp.concatenate([v, jnp.ones_like(v)], axis=1)
            s = lax.dot_general(q, k, (((1,), (1,)), ((), ())), preferred_element_type=F32)
            m_new = jnp.maximum(m, jnp.max(s, axis=-1, keepdims=True))
            p = jnp.exp2(s - m_new).astype(BF16)
            acc = jnp.exp2(m - m_new) * acc + jnp.dot(p, v1, preferred_element_type=F32)
            m = m_new
    o = acc[:, :HEAD_DIM] / acc[:, HEAD_DIM:HEAD_DIM + 1]
    for g in range(group):
        o_ref[:, g * HEAD_DIM:(g + 1) * HEAD_DIM] = o[g * tq:(g + 1) * tq].astype(o_ref.dtype)


def attention(q, segments, *, n_kv, n_req, q_row0, lq, tq, chunk=512):
    group = q.shape[-1] // HEAD_DIM // n_kv
    tq = _tile(lq, tq)
    qb0, nqb = q_row0 // tq, lq // tq
    qw = group * HEAD_DIM
    in_specs = [pl.BlockSpec((tq, qw), lambda r, h, t: (qb0 + r * nqb + t, h))]
    args = [q]
    for k_arr, v_arr, block, imap in segments:
        spec = pl.BlockSpec(block, lambda r, h, t, imap=imap: imap(r, h))
        in_specs += [spec, spec]
        args += [k_arr, v_arr]
    return pl.pallas_call(
        functools.partial(_attn_kernel, group=group, n_seg=len(segments), chunk=chunk),
        out_shape=jax.ShapeDtypeStruct((n_req * lq, q.shape[-1]), BF16),
        grid=(n_req, n_kv, nqb),
        in_specs=in_specs,
        out_specs=pl.BlockSpec((tq, qw), lambda r, h, t: (r * nqb + t, h)),
        compiler_params=_params("parallel", "parallel", "arbitrary"), name="attention",
    )(*args)


def _seq_dft_kernel(m_ref, u_ref, o_ref, f_scr):
    length = u_ref.shape[1]
    half = length // 2
    blk = math.gcd(half, MXU_COLS)
    n_blk = length // blk

    @pl.when(pl.program_id(2) == 0)
    def _():
        r = lax.broadcasted_iota(jnp.int32, (blk, blk), 0)
        c = lax.broadcasted_iota(jnp.int32, (blk, blk), 1)
        mirror = _ones_where((r >= 1) & (c == blk - r), BF16)
        first = _ones_where((r == 0) & (c == 0), BF16)
        row_id = lax.broadcasted_iota(jnp.int32, (blk, 1), 0)
        for b in range(half // blk):
            lo = b * blk
            for plane, sign in ((0, 1.0), (1, -1.0)):
                src = (n_blk - 1 - b) * blk
                mir = jnp.dot(mirror, u_ref[plane, src:src + blk, :], preferred_element_type=F32)
                if b >= 1:
                    mir += jnp.dot(first, u_ref[plane, src + blk:src + 2 * blk, :], preferred_element_type=F32)
                f = u_ref[plane, lo:lo + blk, :].astype(F32) + sign * mir
                if b == 0 and plane == 1:
                    f = jnp.where(row_id == 0, u_ref[0, half:half + 1, :].astype(F32), f)
                f_scr[plane, lo:lo + blk, :] = f.astype(BF16)

    acc = jnp.dot(m_ref[0], f_scr[0], preferred_element_type=F32)
    acc += jnp.dot(m_ref[1], f_scr[1], preferred_element_type=F32)
    o_ref[...] = acc.astype(o_ref.dtype)


def seq_dft(mats, u, *, n_req, row0, tm=512, tn=512):
    length, half = mats.shape[1:]
    d = u.shape[-1]
    tm, tn = _tile(length, tm), _tile(d, tn)
    rb0 = row0 // length
    return pl.pallas_call(
        _seq_dft_kernel,
        out_shape=jax.ShapeDtypeStruct((n_req * length, d), BF16),
        grid=(n_req, d // tn, length // tm),
        in_specs=[pl.BlockSpec((2, tm, half), lambda r, j, i: (0, i, 0)),
                  pl.BlockSpec((2, length, tn), lambda r, j, i: (0, rb0 + r, j))],
        out_specs=pl.BlockSpec((tm, tn), lambda r, j, i: (r * (length // tm) + i, j)),
        scratch_shapes=[pltpu.VMEM((2, half, tn), BF16)],
        compiler_params=_params("parallel", "parallel", "arbitrary"), name="seq_dft",
    )(mats, u)


def _seq_dft_mats(length):
    cos, sin = _dft_mats(length)
    half = length // 2
    return jnp.stack([cos[:, :half], (-sin[:, :half]).at[:, 0].set(cos[:, half])]).astype(BF16)


def _dft_mats(length):
    n2 = 1 << ((length.bit_length() - 1) // 2)
    n1 = length // n2
    assert n1 * n2 == length
    j = jnp.arange(length, dtype=jnp.int32)[:, None]
    ang_a = ((j * jnp.arange(n1, dtype=jnp.int32)[None, :]) % n1).astype(F32) * (2.0 * math.pi / n1)
    ang_b = ((j * jnp.arange(n2, dtype=jnp.int32)[None, :]) % length).astype(F32) * (2.0 * math.pi / length)
    ca, sa = jnp.cos(ang_a)[:, :, None], jnp.sin(ang_a)[:, :, None]
    cb, sb = jnp.cos(ang_b)[:, None, :], jnp.sin(ang_b)[:, None, :]
    norm = 1.0 / math.sqrt(length)
    cos = (ca * cb - sa * sb).reshape(length, length) * norm
    sin = (sa * cb + ca * sb).reshape(length, length) * norm
    return cos, sin


def _rope_tables(n_prompt, n_req, n_lat):
    half = HEAD_DIM // 2
    rows = n_lat // GRID_W
    row_pos = jnp.repeat(jnp.arange(rows, dtype=F32), GRID_W)
    col_pos = jnp.tile(jnp.arange(GRID_W, dtype=F32), rows)
    inv_freq = ROPE_THETA ** (-jnp.arange(0, half, 2, dtype=F32) / half)

    def cs(pos):
        ang = pos[:, None] * inv_freq[None, :]
        return jnp.cos(ang), jnp.sin(ang)

    cr, sr = cs(row_pos)
    cc, sc = cs(col_pos)
    cos = jnp.concatenate([cr, cr, cc, cc], axis=-1)
    sin = jnp.concatenate([-sr, sr, -sc, sc], axis=-1)
    cos = jnp.concatenate([jnp.ones((n_prompt, HEAD_DIM), F32), jnp.tile(cos, (n_req, 1))], axis=0)
    sin = jnp.concatenate([jnp.zeros((n_prompt, HEAD_DIM), F32), jnp.tile(sin, (n_req, 1))], axis=0)
    return cos, sin


def kernel(x_prompt, x_sample, cache_k, cache_v, c, c_ctx, norm_g, w_mod, b_mod, w_q, w_k, w_v, w_o,
           q_norm_g, k_norm_g, w_fourier, w_router, w_gate, w_up, w_down, final_norm_g):
    batch, seq, d = x_prompt.shape
    n_req, n_lat, _ = x_sample.shape
    depth = w_mod.shape[0]
    n_kv = cache_k.shape[3]
    n_exp = w_router.shape[-1]
    n_prompt = batch * seq
    nt = n_prompt + n_req * n_lat
    assert n_req + 1 <= COND_ROWS
    assert n_prompt % n_lat == 0

    def cond_row(r):
        return jnp.where(r < n_prompt, 0, 1 + (r - n_prompt) // n_lat)

    cond = jnp.zeros((COND_ROWS, d), F32).at[0].set(c_ctx).at[1:1 + n_req].set(c)
    mod = modulation_all(cond, w_mod, b_mod)
    cos, sin = _rope_tables(n_prompt, n_req, n_lat)
    scale = math.log2(math.e) / math.sqrt(HEAD_DIM)

    cc, sc = _dft_mats(FGROUP_DIM)
    cs_chan = jnp.concatenate([cc, sc], axis=1).astype(BF16)
    mats_p = mats_s = None
    if depth > 1:
        mats_p, mats_s = _seq_dft_mats(seq), _seq_dft_mats(n_lat)

    x = (x_prompt.reshape(n_prompt, d), x_sample.reshape(n_req * n_lat, d))
    assert n_prompt % LANES == 0 and (n_req * n_lat) % LANES == 0
    cap_p = (CAPACITY_FACTOR * n_prompt) // n_exp
    cap_s = (CAPACITY_FACTOR * n_req * n_lat) // n_exp
    groups = ((0, n_prompt // LANES, cap_p, 0), (n_prompt // LANES, n_req * n_lat // LANES, cap_s, cap_p))
    slots = cap_p + cap_s
    assert cap_p % SUBLANES == 0 and cap_s % SUBLANES == 0
    new_k, new_v = [], []
    mixer_in = None
    for i in range(depth):
        mod_l = mod[i].reshape(COND_ROWS * N_MOD, 1, d)
        j = i // 2
        if i % 2 == 0:
            h = mixer_in if i > 0 else ada_norm(x, norm_g[i, 0], mod_l, 0, 1, cond_row)
            q = project(h, w_q, j, out_dtypes=(BF16,),
                        head=(q_norm_g[j].reshape(1, HEAD_DIM) * scale, cos, sin))
            kf, kb = project(h, w_k, j, out_dtypes=(F32, BF16),
                             head=(k_norm_g[j].reshape(1, HEAD_DIM), cos, sin))
            vf, vb = project(h, w_v, j, out_dtypes=(F32, BF16))
            kvw = n_kv * HEAD_DIM
            new_k.append(kf[:n_prompt].reshape(batch, seq, n_kv, HEAD_DIM))
            new_v.append(vf[:n_prompt].reshape(batch, seq, n_kv, HEAD_DIM))
            att_p = attention(q, [(kb, vb, (seq, HEAD_DIM), lambda r, h: (r, h))],
                              n_kv=n_kv, n_req=batch, q_row0=0, lq=seq, tq=256)
            past = cache_k.shape[2]
            lat0 = n_prompt // n_lat
            att_s = attention(
                q,
                [(cache_k.reshape(n_req, -1, past, kvw), cache_v.reshape(n_req, -1, past, kvw),
                  (None, None, past, HEAD_DIM), lambda r, h, j=j: (r, j, 0, h)),
                 (kb, vb, (n_lat, HEAD_DIM), lambda r, h: (lat0 + r, h))],
                n_kv=n_kv, n_req=n_req, q_row0=n_prompt, lq=n_lat, tq=512)
            x = project((att_p, att_s), w_o, j, out_dtypes=(F32,), residual=(x, mod_l, 2, cond_row))
        else:
            u = mixer_in if i > 0 else ada_norm(x, norm_g[i, 0], mod_l, 0, 1, cond_row, mode="dft", cs=cs_chan)
            f_p = seq_dft(mats_p, u, n_req=batch, row0=0, tn=d)
            f_s = seq_dft(mats_s, u, n_req=n_req, row0=n_prompt)
            x = project((f_p, f_s), w_fourier, j, out_dtypes=(F32,), residual=(x, mod_l, 2, cond_row))

        h, aff = ada_norm(x, norm_g[i, 1], mod_l, 3, 4, cond_row, mode="router",
                          w_router_t=w_router[i].T)
        pick, tot, offs = route_select(aff, groups)
        xe = dispatch(h, pick, aff, tot, offs, slots)
        hid = expert_swiglu(xe, w_gate, w_up, i)
        ye = expert_down(hid, w_down, xe, i)
        if i + 1 < depth:
            follow = "plain" if (i + 1) % 2 == 0 else "dft"
            x, mixer_in = combine(x, ye, pick, tot, offs, mod_l, 5, cond_row, follow=follow, g=norm_g[i + 1, 0],
                                  mod_next=mod[i + 1].reshape(COND_ROWS * N_MOD, 1, d),
                                  cs=cs_chan if follow == "dft" else None)
        else:
            y_p, y_s = combine(x, ye, pick, tot, offs, mod_l, 5, cond_row, follow="final", g=final_norm_g,
                               first_rows=n_prompt)

    return (y_p.reshape(batch, seq, d), y_s.reshape(n_req, n_lat, d),
            jnp.stack(new_k, axis=1), jnp.stack(new_v, axis=1))
```
